```python
import math
import jax, jax.numpy as jnp
from jax import lax
import numpy as np

D_MODEL = 1024
BATCH = 8
SEQ = 4096
DEPTH = 1

CHUNK = 64
N_META = 16

N_HEADS = 8
HEAD_DIM = 64
ATTN_WIDTH = N_HEADS * HEAD_DIM
Q_BLOCK = 128

POOL_WINDOWS = (2, 4, 8, 16)
POOL_GROUPS = len(POOL_WINDOWS)
POOL_WIDTH = D_MODEL // 2
POOL_GROUP_DIM = POOL_WIDTH // POOL_GROUPS

N_BRANCHES = 2

Q_OFF = 0
K_OFF = Q_OFF + ATTN_WIDTH
V_OFF = K_OFF + ATTN_WIDTH
F_OFF = V_OFF + ATTN_WIDTH
P_OFF = F_OFF + N_HEADS
G_OFF = P_OFF + POOL_WIDTH
IN_WIDTH = G_OFF + N_BRANCHES * D_MODEL

N_EXPERT_GROUPS = 4
EXPERTS_PER_GROUP = 8
EXPERT_HIDDEN = 256
TOP_K_EXPERTS = 2

RMS_EPS = 1e-6

kernel_name = "hybrid_fox_pool_hiermoe_block"


def rmsnorm(x, g):
    xf = x.astype(jnp.float32)
    y = xf * lax.rsqrt(jnp.mean(xf * xf, axis=-1, keepdims=True) + RMS_EPS)
    return (y * g.astype(jnp.float32)).astype(x.dtype)


def forgetting_attention(q, k, v, fcum):
    L = q.shape[2]
    n_blocks = -(-L // Q_BLOCK)
    pad = n_blocks * Q_BLOCK - L
    q = jnp.pad(q, ((0, 0), (0, 0), (0, pad), (0, 0)))
    k = jnp.pad(k, ((0, 0), (0, 0), (0, pad), (0, 0)))
    v = jnp.pad(v, ((0, 0), (0, 0), (0, pad), (0, 0)))
    fcum = jnp.pad(fcum, ((0, 0), (0, 0), (0, pad)))
    scale = 1.0 / math.sqrt(HEAD_DIM)
    outs = []
    for i in range(n_blocks):
        lo, hi = i * Q_BLOCK, (i + 1) * Q_BLOCK
        qb, kb, vb = q[:, :, lo:hi], k[:, :, :hi], v[:, :, :hi]
        s = jnp.einsum('bhqd,bhkd->bhqk', qb, kb).astype(jnp.float32) * scale
        s = s + (fcum[:, :, lo:hi, None] - fcum[:, :, None, :hi])
        mask = jnp.arange(hi)[None, :] <= (lo + jnp.arange(Q_BLOCK))[:, None]
        s = jnp.where(mask[None, None], s, -jnp.inf)
        p = jax.nn.softmax(s, axis=-1).astype(vb.dtype)
        outs.append(jnp.einsum('bhqk,bhkd->bhqd', p, vb))
    return jnp.concatenate(outs, axis=2)[:, :, :L]


def multiscale_pool(u, w_pool, pool_scale):
    B, L, C = u.shape
    ug = u.reshape(B, L, POOL_GROUPS, POOL_GROUP_DIM)
    cs = jnp.cumsum(ug.astype(jnp.float32), axis=1)
    cs = jnp.concatenate([jnp.zeros_like(cs[:, :1]), cs], axis=1)
    hi = jnp.arange(1, L + 1)
    windows = jnp.array(POOL_WINDOWS, dtype=jnp.int32)
    lo = jnp.maximum(hi[:, None] - windows[None, :], 0)
    grp = jnp.arange(POOL_GROUPS)[None, :]
    count = (hi[:, None] - lo).astype(jnp.float32)[None, :, :, None]
    mean = (cs[:, 1:] - cs[:, lo, grp]) / count
    mixed = (mean - ug.astype(jnp.float32)).astype(u.dtype)
    y = jnp.einsum('blgc,gcd->blgd', mixed, w_pool).reshape(B, L, C)
    return y * pool_scale


def hierarchical_moe(x2, w_group, b_group, w_router, b_router, w1, w3, w2):
    T = x2.shape[0]
    g_logits = (x2 @ w_group).astype(jnp.float32) + b_group.astype(jnp.float32)
    g_prob = jax.nn.softmax(g_logits, axis=-1)
    g_top, g_idx = lax.top_k(g_prob, 1)
    e_logits = ((x2 @ w_router).astype(jnp.float32) + b_router.astype(jnp.float32)
                ).reshape(T, N_EXPERT_GROUPS, EXPERTS_PER_GROUP)
    e_sel = jnp.take_along_axis(e_logits, g_idx[:, :, None], axis=1)[:, 0]
    e_prob = jax.nn.softmax(e_sel, axis=-1)
    e_top, e_idx = lax.top_k(e_prob, TOP_K_EXPERTS)
    e_w = e_top / jnp.sum(e_top, axis=-1, keepdims=True) * g_top
    w_within = jnp.sum(jax.nn.one_hot(e_idx, EXPERTS_PER_GROUP, dtype=jnp.float32)
                       * e_w[..., None], axis=1)
    combine = (jax.nn.one_hot(g_idx[:, 0], N_EXPERT_GROUPS, dtype=jnp.float32)[:, :, None]
               * w_within[:, None, :]).astype(x2.dtype)
    y = jnp.zeros_like(x2)
    for g in range(N_EXPERT_GROUPS):
        h1 = jnp.einsum('td,edf->tef', x2, w1[g])
        h3 = jnp.einsum('td,edf->tef', x2, w3[g])
        hh = jax.nn.silu(h1) * h3 * combine[:, g, :, None]
        y = y + jnp.einsum('tef,efd->td', hh, w2[g])
    return y


def setup_inputs(seed: int = 0) -> dict:
    key = jax.random.key(seed)
    ks = jax.random.split(key, 20)
    f32 = jnp.float32
    D, G, E, F = D_MODEL, N_EXPERT_GROUPS, EXPERTS_PER_GROUP, EXPERT_HIDDEN
    nrm = lambda k, shape, fan_in: jax.random.normal(k, shape, f32) * (fan_in ** -0.5)
    return {
        "x": jax.random.normal(ks[0], (BATCH, SEQ, D), f32),
        "meta_tokens": jax.random.normal(ks[1], (N_META, D), f32),
        "norm_mix_g": 1.0 + 0.05 * jax.random.normal(ks[2], (DEPTH, D), f32),
        "w_in": nrm(ks[3], (DEPTH, D, IN_WIDTH), D),
        "b_forget": jax.random.uniform(ks[4], (DEPTH, N_HEADS), f32, 3.0, 6.0),
        "q_norm_g": 1.0 + 0.05 * jax.random.normal(ks[5], (DEPTH, HEAD_DIM), f32),
        "k_norm_g": 1.0 + 0.05 * jax.random.normal(ks[6], (DEPTH, HEAD_DIM), f32),
        "w_up_attn": nrm(ks[7], (DEPTH, ATTN_WIDTH, D), ATTN_WIDTH),
        "w_pool": nrm(ks[8], (DEPTH, POOL_GROUPS, POOL_GROUP_DIM, POOL_GROUP_DIM), POOL_GROUP_DIM),
        "pool_scale": 1.0 + 0.1 * jax.random.normal(ks[9], (DEPTH, POOL_WIDTH), f32),
        "w_up_pool": nrm(ks[10], (DEPTH, POOL_WIDTH, D), POOL_WIDTH),
        "w_out": nrm(ks[11], (DEPTH, D, D), D),
        "norm_ffn_g": 1.0 + 0.05 * jax.random.normal(ks[12], (DEPTH, D), f32),
        "w_group": nrm(ks[13], (DEPTH, D, G), D),
        "b_group": 0.01 * jax.random.normal(ks[14], (DEPTH, G), f32),
        "w_router": nrm(ks[15], (DEPTH, D, G * E), D),
        "b_router": 0.01 * jax.random.normal(ks[16], (DEPTH, G * E), f32),
        "w1": nrm(ks[17], (DEPTH, G, E, D, F), D),
        "w3": nrm(ks[18], (DEPTH, G, E, D, F), D),
        "w2": nrm(ks[19], (DEPTH, G, E, F, D), F),
    }


def reference(x, meta_tokens, norm_mix_g, w_in, b_forget, q_norm_g, k_norm_g, w_up_attn,
              w_pool, pool_scale, w_up_pool, w_out, norm_ffn_g, w_group, b_group, w_router,
              b_router, w1, w3, w2):
    B = x.shape[0]
    meta = jnp.broadcast_to(meta_tokens.astype(x.dtype)[None], (B, N_META, D_MODEL))
    h = jnp.concatenate([meta, x], axis=1)
    L = h.shape[1]
    for l in range(DEPTH):
        hn = rmsnorm(h, norm_mix_g[l])
        proj = hn @ w_in[l]
        q = rmsnorm(proj[..., Q_OFF:K_OFF].reshape(B, L, N_HEADS, HEAD_DIM), q_norm_g[l])
        k = rmsnorm(proj[..., K_OFF:V_OFF].reshape(B, L, N_HEADS, HEAD_DIM), k_norm_g[l])
        v = proj[..., V_OFF:F_OFF].reshape(B, L, N_HEADS, HEAD_DIM)
        log_f = jax.nn.log_sigmoid(proj[..., F_OFF:P_OFF].astype(jnp.float32)
                                   + b_forget[l].astype(jnp.float32))
        fcum = jnp.cumsum(log_f, axis=1).transpose(0, 2, 1)
        o = forgetting_attention(q.transpose(0, 2, 1, 3), k.transpose(0, 2, 1, 3),
                                 v.transpose(0, 2, 1, 3), fcum)
        y_attn = o.transpose(0, 2, 1, 3).reshape(B, L, ATTN_WIDTH) @ w_up_attn[l]
        y_pool = multiscale_pool(proj[..., P_OFF:G_OFF], w_pool[l], pool_scale[l]) @ w_up_pool[l]
        gates = jax.nn.sigmoid(proj[..., G_OFF:].astype(jnp.float32)).astype(h.dtype)
        gates = gates.reshape(B, L, N_BRANCHES, D_MODEL)
        merged = gates[:, :, 0] * y_attn + gates[:, :, 1] * y_pool
        h = h + merged @ w_out[l]
        hn = rmsnorm(h, norm_ffn_g[l]).reshape(B * L, D_MODEL)
        y_ffn = hierarchical_moe(hn, w_group[l], b_group[l], w_router[l], b_router[l],
                                 w1[l], w3[l], w2[l])
        h = h + y_ffn.reshape(B, L, D_MODEL)
    return h[:, N_META:]
```

```python
import functools
import math

import jax
import jax.numpy as jnp
from jax import lax
from jax.experimental import pallas as pl
from jax.experimental.pallas import tpu as pltpu

F32 = jnp.float32
BF16 = jnp.bfloat16

D_MODEL = 1024
N_META = 16
N_HEADS = 8
HEAD_DIM = 64
ATTN_WIDTH = N_HEADS * HEAD_DIM
POOL_WINDOWS = (2, 4, 8, 16)
POOL_WIDTH = 512
POOL_GROUP_DIM = POOL_WIDTH // len(POOL_WINDOWS)
MAX_WINDOW = max(POOL_WINDOWS)
N_GROUPS = 4
N_PER_GROUP = 8
N_EXPERTS = N_GROUPS * N_PER_GROUP
EXPERT_HIDDEN = 256
RMS_EPS = 1e-6

LANES = 128
AUG_ROWS = 16
HEAD_ROWS = HEAD_DIM + AUG_ROWS
QK_ROWS = N_HEADS * HEAD_ROWS
ROUTER_LANE0 = N_GROUPS
VMEM_LIMIT = 56 * 1024 * 1024

_TN = (((0,), (0,)), ((), ()))


def _const_spec(shape):
    zeros = (0,) * len(shape)
    return pl.BlockSpec(shape, lambda *_: zeros, pipeline_mode=pl.Buffered(1))


def _split3(x):
    hi = x.astype(BF16).astype(F32)
    r = x - hi
    mid = r.astype(BF16).astype(F32)
    lo = (r - mid).astype(BF16).astype(F32)
    return hi, mid, lo


def _inproj_kernel(x_ref, g_ref, wqkv_ref, wf_ref, bf_ref, wp_ref, wg_ref, gq_ref, gk_ref,
                   selq_ref, selk_ref, wpool_ref, pscale_ref, uprev_ref, fprev_ref,
                   qta_ref, kta_ref, vt_ref, pooled_ref, gates_ref, utail_ref, ftail_ref,
                   ubuf, fcarry, *, tm, n_valid, first_pos):
    j = pl.program_id(1)

    @pl.when(j == 0)
    def _():
        ubuf[0:MAX_WINDOW, :] = uprev_ref[...]
        fcarry[...] = fprev_ref[...]

    x = x_ref[...]
    ms = jnp.mean(x * x, axis=-1, keepdims=True)
    hn = (x * lax.rsqrt(ms + RMS_EPS) * g_ref[...]).astype(BF16)

    qkv = jnp.dot(hn, wqkv_ref[...], preferred_element_type=F32)
    qt = qkv[:, 0:ATTN_WIDTH].T
    kt = qkv[:, ATTN_WIDTH:2 * ATTN_WIDTH].T
    vt_ref[...] = qkv[:, 2 * ATTN_WIDTH:3 * ATTN_WIDTH].T.astype(BF16)
    for h in range(N_HEADS):
        for src, gain_ref, dst in ((qt, gq_ref, qta_ref), (kt, gk_ref, kta_ref)):
            xh = src[h * HEAD_DIM:(h + 1) * HEAD_DIM, :]
            ssq = jnp.mean(xh * xh, axis=0, keepdims=True)
            xn = xh * lax.rsqrt(ssq + RMS_EPS) * gain_ref[...]
            dst[h * HEAD_ROWS:h * HEAD_ROWS + HEAD_DIM, :] = xn.astype(BF16)

    f = jnp.dot(hn, wf_ref[...], preferred_element_type=F32) + bf_ref[...]
    z = f.T[0:N_HEADS, :]
    logf = jnp.minimum(z, 0.0) - jnp.log1p(jnp.exp(-jnp.abs(z)))
    parts = jnp.concatenate(_split3(logf), axis=0).astype(BF16)
    r_i = lax.broadcasted_iota(jnp.int32, (tm, tm), 0)
    c_i = lax.broadcasted_iota(jnp.int32, (tm, tm), 1)
    tri = jnp.where(r_i <= c_i, 1.0, 0.0).astype(BF16)
    cs3 = jnp.dot(parts, tri, preferred_element_type=F32)
    cs = cs3[0:8] + cs3[8:16] + cs3[16:24]
    fc = pltpu.repeat(fcarry[...], tm // LANES, axis=1) + cs
    last = jnp.broadcast_to(fc[:, n_valid - 1:n_valid], (N_HEADS, LANES))
    fcarry[...] = last
    ftail_ref[...] = last

    pieces = jnp.concatenate(
        _split3(fc) + (jnp.ones((8, tm), F32), jnp.zeros((LANES - 32, tm), F32)),
        axis=0).astype(BF16)
    fq = jnp.dot(selq_ref[...], pieces, preferred_element_type=F32)
    fk = jnp.dot(selk_ref[...], pieces, preferred_element_type=F32)
    for h in range(N_HEADS):
        lo_r = h * HEAD_ROWS + HEAD_DIM
        qta_ref[lo_r:lo_r + AUG_ROWS, :] = fq[h * AUG_ROWS:(h + 1) * AUG_ROWS, :].astype(BF16)
        kta_ref[lo_r:lo_r + AUG_ROWS, :] = fk[h * AUG_ROWS:(h + 1) * AUG_ROWS, :].astype(BF16)

    u = jnp.dot(hn, wp_ref[...], preferred_element_type=F32)
    ubuf[MAX_WINDOW:MAX_WINDOW + tm, :] = u
    mixed = []
    for g, w in enumerate(POOL_WINDOWS):
        c0 = g * POOL_GROUP_DIM
        acc = u[:, c0:c0 + POOL_GROUP_DIM]
        for s in range(1, w):
            acc = acc + ubuf[MAX_WINDOW - s:MAX_WINDOW - s + tm, c0:c0 + POOL_GROUP_DIM]
        if first_pos + 1 >= w:
            mean = acc * (1.0 / w)
        else:
            pos = first_pos + j * tm + lax.broadcasted_iota(jnp.int32, (tm, POOL_GROUP_DIM), 0)
            mean = acc / jnp.minimum(pos + 1, w).astype(F32)
        mixed.append(mean - u[:, c0:c0 + POOL_GROUP_DIM])
    mixed = jnp.concatenate(mixed, axis=1).astype(BF16)
    y = jnp.dot(mixed, wpool_ref[...], preferred_element_type=F32) * pscale_ref[...]
    pooled_ref[...] = y.astype(BF16)
    tail = u[n_valid - MAX_WINDOW:n_valid, :]
    ubuf[0:MAX_WINDOW, :] = tail
    utail_ref[...] = tail

    gl = jnp.dot(hn, wg_ref[...], preferred_element_type=F32)
    gates_ref[...] = jax.nn.sigmoid(gl).astype(BF16)


def _inproj(x3, p, uprev, fprev, *, tm, n_valid, first_pos):
    nb, seq, _ = x3.shape
    nt = seq // tm
    gq = jnp.broadcast_to((p["gq"] * (1.0 / math.sqrt(HEAD_DIM)))[:, None], (HEAD_DIM, tm))
    gk = jnp.broadcast_to(p["gk"][:, None], (HEAD_DIM, tm))
    tok = lambda width: pl.BlockSpec((None, tm, width), lambda b, j: (b, j, 0))
    chan = lambda rows: pl.BlockSpec((None, rows, tm), lambda b, j: (b, 0, j))
    kern = functools.partial(_inproj_kernel, tm=tm, n_valid=n_valid, first_pos=first_pos)
    return pl.pallas_call(
        kern,
        grid=(nb, nt),
        in_specs=[
            tok(D_MODEL),
            _const_spec((1, D_MODEL)),
            _const_spec((D_MODEL, 3 * ATTN_WIDTH)),
            _const_spec((D_MODEL, LANES)),
            _const_spec((1, LANES)),
            _const_spec((D_MODEL, POOL_WIDTH)),
            _const_spec((D_MODEL, 2 * D_MODEL)),
            _const_spec((HEAD_DIM, tm)),
            _const_spec((HEAD_DIM, tm)),
            _const_spec((LANES, LANES)),
            _const_spec((LANES, LANES)),
            _const_spec((POOL_WIDTH, POOL_WIDTH)),
            _const_spec((1, POOL_WIDTH)),
            _const_spec((MAX_WINDOW, POOL_WIDTH)),
            _const_spec((N_HEADS, LANES)),
        ],
        out_specs=[
            chan(QK_ROWS), chan(QK_ROWS), chan(ATTN_WIDTH),
            tok(POOL_WIDTH), tok(2 * D_MODEL),
            pl.BlockSpec((MAX_WINDOW, POOL_WIDTH), lambda b, j: (0, 0)),
            pl.BlockSpec((N_HEADS, LANES), lambda b, j: (0, 0)),
        ],
        out_shape=[
            jax.ShapeDtypeStruct((nb, QK_ROWS, seq), BF16),
            jax.ShapeDtypeStruct((nb, QK_ROWS, seq), BF16),
            jax.ShapeDtypeStruct((nb, ATTN_WIDTH, seq), BF16),
            jax.ShapeDtypeStruct((nb, seq, POOL_WIDTH), BF16),
            jax.ShapeDtypeStruct((nb, seq, 2 * D_MODEL), BF16),
            jax.ShapeDtypeStruct((MAX_WINDOW, POOL_WIDTH), F32),
            jax.ShapeDtypeStruct((N_HEADS, LANES), F32),
        ],
        scratch_shapes=[
            pltpu.VMEM((MAX_WINDOW + tm, POOL_WIDTH), F32),
            pltpu.VMEM((N_HEADS, LANES), F32),
        ],
        compiler_params=pltpu.CompilerParams(
            dimension_semantics=("arbitrary", "arbitrary"), vmem_limit_bytes=VMEM_LIMIT),
        name="inproj",
    )(x3, p["g_mix"], p["wqkv"], p["wf"], p["bf"], p["wp"], p["wg"], gq, gk,
      p["selq"], p["selk"], p["wpool"], p["pscale"], uprev, fprev)


def _attn_kernel(qta_ref, kta_ref, vt_ref, ktam_ref, vtm_ref, ot_ref, *, tq):
    qi = pl.program_id(1)
    neg_inf = jnp.float32(-jnp.inf)

    def head_body(h, carry):
        r0 = pl.multiple_of(h * HEAD_ROWS, AUG_ROWS)
        v0 = pl.multiple_of(h * HEAD_DIM, HEAD_DIM)
        qa = qta_ref[pl.ds(r0, HEAD_ROWS), :]

        def scores(ka):
            return lax.dot_general(ka, qa, _TN, preferred_element_type=F32)

        def weighted_values(v, p):
            va = jnp.concatenate([v, jnp.ones((AUG_ROWS, v.shape[1]), BF16)], axis=0)
            return jnp.dot(va, p.astype(BF16), preferred_element_type=F32)

        s = scores(ktam_ref[pl.ds(r0, HEAD_ROWS), :])
        row = lax.broadcasted_iota(jnp.int32, s.shape, 0)
        s = jnp.where(row < N_META, s, neg_inf)
        m = jnp.max(s, axis=0, keepdims=True)
        acc = weighted_values(vtm_ref[pl.ds(v0, HEAD_DIM), :], jnp.exp(s - m))

        def step(s, v, m, acc):
            m_new = jnp.maximum(m, jnp.max(s, axis=0, keepdims=True))
            alpha = jnp.exp(m - m_new)
            acc = alpha * acc + weighted_values(v, jnp.exp(s - m_new))
            return m_new, acc

        def kv_body(i, c):
            s0 = pl.multiple_of(i * tq, tq)
            s = scores(kta_ref[pl.ds(r0, HEAD_ROWS), pl.ds(s0, tq)])
            return step(s, vt_ref[pl.ds(v0, HEAD_DIM), pl.ds(s0, tq)], *c)

        m, acc = lax.fori_loop(0, qi, kv_body, (m, acc))

        s0 = pl.multiple_of(qi * tq, tq)
        s = scores(kta_ref[pl.ds(r0, HEAD_ROWS), pl.ds(s0, tq)])
        r_i = lax.broadcasted_iota(jnp.int32, s.shape, 0)
        c_i = lax.broadcasted_iota(jnp.int32, s.shape, 1)
        s = jnp.where(r_i <= c_i, s, neg_inf)
        m, acc = step(s, vt_ref[pl.ds(v0, HEAD_DIM), pl.ds(s0, tq)], m, acc)

        o = acc[0:HEAD_DIM, :] / acc[HEAD_DIM:HEAD_DIM + 1, :]
        ot_ref[pl.ds(v0, HEAD_DIM), :] = o.astype(BF16)
        return carry

    lax.fori_loop(0, N_HEADS, head_body, 0)


def _attention(qta, kta, vt, kta_m, vt_m, *, tq):
    nb, _, seq = qta.shape
    return pl.pallas_call(
        functools.partial(_attn_kernel, tq=tq),
        grid=(nb, seq // tq),
        in_specs=[
            pl.BlockSpec((None, QK_ROWS, tq), lambda b, i: (b, 0, i)),
            pl.BlockSpec((None, QK_ROWS, seq), lambda b, i: (b, 0, 0)),
            pl.BlockSpec((None, ATTN_WIDTH, seq), lambda b, i: (b, 0, 0)),
            pl.BlockSpec((None, QK_ROWS, LANES), lambda b, i: (0, 0, 0)),
            pl.BlockSpec((None, ATTN_WIDTH, LANES), lambda b, i: (0, 0, 0)),
        ],
        out_specs=pl.BlockSpec((None, ATTN_WIDTH, tq), lambda b, i: (b, 0, i)),
        out_shape=jax.ShapeDtypeStruct((nb, ATTN_WIDTH, seq), BF16),
        compiler_params=pltpu.CompilerParams(
            dimension_semantics=("arbitrary", "arbitrary"), vmem_limit_bytes=VMEM_LIMIT),
        name="fox_attention",
    )(qta, kta, vt, kta_m, vt_m)


def _post_kernel(ot_ref, pooled_ref, gates_ref, x_ref, wua_ref, wup_ref, wout_ref, g_ref,
                 wr_ref, br_ref, h2_ref, hn2_ref, comb_ref):
    y_attn = lax.dot_general(ot_ref[...], wua_ref[...], _TN, preferred_element_type=F32)
    y_pool = jnp.dot(pooled_ref[...], wup_ref[...], preferred_element_type=F32)
    merged = (gates_ref[:, 0:D_MODEL].astype(F32) * y_attn
              + gates_ref[:, D_MODEL:2 * D_MODEL].astype(F32) * y_pool)
    h2 = x_ref[...] + jnp.dot(merged.astype(BF16), wout_ref[...], preferred_element_type=F32)
    h2_ref[...] = h2
    ms = jnp.mean(h2 * h2, axis=-1, keepdims=True)
    hn2 = (h2 * lax.rsqrt(ms + RMS_EPS) * g_ref[...]).astype(BF16)
    hn2_ref[...] = hn2

    logits = jnp.dot(hn2, wr_ref[...], preferred_element_type=F32) + br_ref[...]
    lane = lax.broadcasted_iota(jnp.int32, logits.shape, 1)
    neg_inf = jnp.float32(-jnp.inf)

    def softmax_over(mask):
        z = jnp.where(mask, logits, neg_inf)
        e = jnp.exp(z - jnp.max(z, axis=1, keepdims=True))
        return jnp.where(mask, e / jnp.sum(e, axis=1, keepdims=True), -1.0)

    def top1(prob):
        top = jnp.max(prob, axis=1, keepdims=True)
        idx = jnp.min(jnp.where(prob == top, lane, LANES), axis=1, keepdims=True)
        return top, idx

    g_top, g_idx = top1(softmax_over(lane < N_GROUPS))
    e_lo = ROUTER_LANE0 + N_PER_GROUP * g_idx
    e_prob = softmax_over((lane >= e_lo) & (lane < e_lo + N_PER_GROUP))
    p1, i1 = top1(e_prob)
    p2, i2 = top1(jnp.where(lane == i1, -1.0, e_prob))
    denom = p1 + p2
    w1 = p1 / denom * g_top
    w2 = p2 / denom * g_top
    comb_ref[...] = (jnp.where(lane == i1 - ROUTER_LANE0, w1, 0.0)
                     + jnp.where(lane == i2 - ROUTER_LANE0, w2, 0.0))


def _post(ot, pooled, gates, x, p, *, tm):
    nb, seq, _ = x.shape
    tok = lambda width: pl.BlockSpec((None, tm, width), lambda b, j: (b, j, 0))
    return pl.pallas_call(
        _post_kernel,
        grid=(nb, seq // tm),
        in_specs=[
            pl.BlockSpec((None, ATTN_WIDTH, tm), lambda b, j: (b, 0, j)),
            tok(POOL_WIDTH), tok(2 * D_MODEL), tok(D_MODEL),
            _const_spec((ATTN_WIDTH, D_MODEL)),
            _const_spec((POOL_WIDTH, D_MODEL)),
            _const_spec((D_MODEL, D_MODEL)),
            _const_spec((1, D_MODEL)),
            _const_spec((D_MODEL, LANES)),
            _const_spec((1, LANES)),
        ],
        out_specs=[tok(D_MODEL), tok(D_MODEL), tok(LANES)],
        out_shape=[
            jax.ShapeDtypeStruct((nb, seq, D_MODEL), F32),
            jax.ShapeDtypeStruct((nb, seq, D_MODEL), BF16),
            jax.ShapeDtypeStruct((nb, seq, LANES), F32),
        ],
        compiler_params=pltpu.CompilerParams(
            dimension_semantics=("arbitrary", "arbitrary"), vmem_limit_bytes=VMEM_LIMIT),
        name="post_router",
    )(ot, pooled, gates, x, p["wua"], p["wup"], p["wout"], p["g_ffn"], p["wr"], p["br"])


def _moe_kernel(x_ref, h2_ref, comb_ref, w13_ref, w2_ref, out_ref):
    e = pl.program_id(1)

    @pl.when(e == 0)
    def _():
        out_ref[...] = h2_ref[...]

    h = jnp.dot(x_ref[...], w13_ref[...], preferred_element_type=F32)
    h1 = h[:, 0:EXPERT_HIDDEN]
    h3 = h[:, EXPERT_HIDDEN:2 * EXPERT_HIDDEN]
    comb = comb_ref[...]
    lane = lax.broadcasted_iota(jnp.int32, comb.shape, 1)
    ce = jnp.sum(jnp.where(lane == e, comb, 0.0), axis=1, keepdims=True)
    hh = h1 * jax.nn.sigmoid(h1) * h3 * ce
    out_ref[...] += jnp.dot(hh.astype(BF16), w2_ref[...], preferred_element_type=F32)


def _moe(hn2, h2, comb, p, *, tm):
    t = hn2.shape[0]
    tok = lambda width: pl.BlockSpec((tm, width), lambda i, e: (i, 0))
    return pl.pallas_call(
        _moe_kernel,
        grid=(t // tm, N_EXPERTS),
        in_specs=[
            tok(D_MODEL), tok(D_MODEL), tok(LANES),
            pl.BlockSpec((None, D_MODEL, 2 * EXPERT_HIDDEN), lambda i, e: (e, 0, 0)),
            pl.BlockSpec((None, EXPERT_HIDDEN, D_MODEL), lambda i, e: (e, 0, 0)),
        ],
        out_specs=tok(D_MODEL),
        out_shape=jax.ShapeDtypeStruct((t, D_MODEL), F32),
        compiler_params=pltpu.CompilerParams(
            dimension_semantics=("arbitrary", "arbitrary"), vmem_limit_bytes=VMEM_LIMIT),
        name="moe",
    )(hn2, h2, comb, p["w13"], p["w2"])


def _bias_selectors():
    selq = [[0.0] * LANES for _ in range(LANES)]
    selk = [[0.0] * LANES for _ in range(LANES)]
    ones_row = 3 * N_HEADS
    for h in range(N_HEADS):
        for part in range(3):
            selq[h * AUG_ROWS + part][part * N_HEADS + h] = 1.0
            selq[h * AUG_ROWS + 3 + part][ones_row] = 1.0
            selk[h * AUG_ROWS + part][ones_row] = 1.0
            selk[h * AUG_ROWS + 3 + part][part * N_HEADS + h] = -1.0
    return jnp.array(selq, BF16), jnp.array(selk, BF16)


def _prepare_params(norm_mix_g, w_in, b_forget, q_norm_g, k_norm_g, w_up_attn, w_pool,
                    pool_scale, w_up_pool, w_out, norm_ffn_g, w_group, b_group, w_router,
                    b_router, w1, w3, w2):
    aw, pw = ATTN_WIDTH, POOL_WIDTH
    f_off = 3 * aw
    p_off = f_off + N_HEADS
    g_off = p_off + pw
    selq, selk = _bias_selectors()
    pad_lanes = lambda a: jnp.pad(a, ((0, 0), (0, LANES - a.shape[1])))
    wpool = jnp.zeros((pw, pw), F32)
    for g in range(len(POOL_WINDOWS)):
        sl = slice(g * POOL_GROUP_DIM, (g + 1) * POOL_GROUP_DIM)
        wpool = wpool.at[sl, sl].set(w_pool[g])
    d, f = D_MODEL, EXPERT_HIDDEN
    return {
        "g_mix": norm_mix_g.reshape(1, d),
        "wqkv": w_in[:, 0:f_off].astype(BF16),
        "wf": pad_lanes(w_in[:, f_off:p_off]).astype(BF16),
        "bf": pad_lanes(b_forget.reshape(1, N_HEADS)),
        "wp": w_in[:, p_off:g_off].astype(BF16),
        "wg": w_in[:, g_off:].astype(BF16),
        "gq": q_norm_g, "gk": k_norm_g,
        "selq": selq, "selk": selk,
        "wpool": wpool.astype(BF16),
        "pscale": pool_scale.reshape(1, pw),
        "wua": w_up_attn.astype(BF16),
        "wup": w_up_pool.astype(BF16),
        "wout": w_out.astype(BF16),
        "g_ffn": norm_ffn_g.reshape(1, d),
        "wr": pad_lanes(jnp.concatenate([w_group, w_router], axis=1)).astype(BF16),
        "br": pad_lanes(jnp.concatenate([b_group, b_router]).reshape(1, -1)),
        "w13": jnp.concatenate([w1, w3], axis=-1).reshape(N_EXPERTS, d, 2 * f).astype(BF16),
        "w2": w2.reshape(N_EXPERTS, f, d).astype(BF16),
    }


def kernel(x, meta_tokens, norm_mix_g, w_in, b_forget, q_norm_g, k_norm_g, w_up_attn, w_pool,
           pool_scale, w_up_pool, w_out, norm_ffn_g, w_group, b_group, w_router, b_router,
           w1, w3, w2):
    nb, seq, d = x.shape
    p = _prepare_params(norm_mix_g[0], w_in[0], b_forget[0], q_norm_g[0], k_norm_g[0],
                        w_up_attn[0], w_pool[0], pool_scale[0], w_up_pool[0], w_out[0],
                        norm_ffn_g[0], w_group[0], b_group[0], w_router[0], b_router[0],
                        w1[0], w3[0], w2[0])

    meta = jnp.pad(meta_tokens.astype(x.dtype), ((0, LANES - N_META), (0, 0)))[None]
    _, kta_m, vt_m, _, _, u_meta, f_meta = _inproj(
        meta, p, jnp.zeros((MAX_WINDOW, POOL_WIDTH), F32), jnp.zeros((N_HEADS, LANES), F32),
        tm=LANES, n_valid=N_META, first_pos=0)

    tm = 512
    qta, kta, vt, pooled, gates, _, _ = _inproj(
        x, p, u_meta, f_meta, tm=tm, n_valid=tm, first_pos=N_META)
    ot = _attention(qta, kta, vt, kta_m, vt_m, tq=512)
    h2, hn2, comb = _post(ot, pooled, gates, x, p, tm=512)
    t = nb * seq
    out = _moe(hn2.reshape(t, d), h2.reshape(t, d), comb.reshape(t, LANES), p, tm=1024)
    return out.reshape(nb, seq, d)
```

```python
import functools
import math

import jax
import jax.numpy as jnp
from jax import lax
from jax.experimental import pallas as pl
from jax.experimental.pallas import tpu as pltpu

F32 = jnp.float32
BF16 = jnp.bfloat16

D_MODEL = 1024
N_META = 16
N_HEADS = 8
HEAD_DIM = 64
ATTN_WIDTH = N_HEADS * HEAD_DIM
POOL_WINDOWS = (2, 4, 8, 16)
POOL_WIDTH = 512
POOL_GROUP_DIM = POOL_WIDTH // len(POOL_WINDOWS)
MAX_WINDOW = max(POOL_WINDOWS)
N_GROUPS = 4
N_PER_GROUP = 8
N_EXPERTS = N_GROUPS * N_PER_GROUP
EXPERT_HIDDEN = 256
RMS_EPS = 1e-6
LOG2_E = math.log2(math.e)

LANES = 128
AUG_ROWS = 16
HEAD_ROWS = HEAD_DIM + AUG_ROWS
QK_ROWS = N_HEADS * HEAD_ROWS
ROUTER_LANE0 = N_GROUPS
VMEM_LIMIT = 56 * 1024 * 1024

_TN = (((0,), (0,)), ((), ()))


def _const_spec(shape):
    zeros = (0,) * len(shape)
    return pl.BlockSpec(shape, lambda *_: zeros, pipeline_mode=pl.Buffered(1))


def _split3(x):
    hi = x.astype(BF16).astype(F32)
    r = x - hi
    mid = r.astype(BF16).astype(F32)
    lo = (r - mid).astype(BF16).astype(F32)
    return hi, mid, lo


def _inproj_kernel(x_ref, g_ref, wqkv_ref, wf_ref, bf_ref, wp_ref, wg_ref, gq_ref, gk_ref,
                   selq_ref, selk_ref, wpool_ref, pscale_ref, uprev_ref, fprev_ref,
                   qta_ref, ka_ref, vt_ref, pooled_ref, gates_ref, utail_ref, ftail_ref,
                   ubuf, fcarry, *, tm, n_valid, first_pos):
    j = pl.program_id(1)

    @pl.when(j == 0)
    def _():
        ubuf[0:MAX_WINDOW, :] = uprev_ref[...]
        fcarry[...] = fprev_ref[...]

    x = x_ref[...]
    ms = jnp.mean(x * x, axis=-1, keepdims=True)
    hn = (x * lax.rsqrt(ms + RMS_EPS) * g_ref[...]).astype(BF16)

    f = jnp.dot(hn, wf_ref[...], preferred_element_type=F32) + bf_ref[...]
    z = f.T[0:N_HEADS, :]
    logf = (jnp.minimum(z, 0.0) - jnp.log1p(jnp.exp(-jnp.abs(z)))) * LOG2_E
    parts = jnp.concatenate(_split3(logf), axis=0).astype(BF16)
    r_i = lax.broadcasted_iota(jnp.int32, (tm, tm), 0)
    c_i = lax.broadcasted_iota(jnp.int32, (tm, tm), 1)
    tri = jnp.where(r_i <= c_i, 1.0, 0.0).astype(BF16)
    cs3 = jnp.dot(parts, tri, preferred_element_type=F32)
    cs = cs3[0:8] + cs3[8:16] + cs3[16:24]
    fc = pltpu.repeat(fcarry[...], tm // LANES, axis=1) + cs
    last = jnp.broadcast_to(fc[:, n_valid - 1:n_valid], (N_HEADS, LANES))
    fcarry[...] = last
    ftail_ref[...] = last

    pieces = jnp.concatenate(
        _split3(fc) + (jnp.ones((8, tm), F32), jnp.zeros((LANES - 32, tm), F32)),
        axis=0).astype(BF16)
    fq = jnp.dot(selq_ref[...], pieces, preferred_element_type=F32)
    fk = jnp.dot(selk_ref[...], pieces, preferred_element_type=F32)

    qkv = jnp.dot(hn, wqkv_ref[...], preferred_element_type=F32)
    qt = qkv[:, 0:ATTN_WIDTH].T
    kt = qkv[:, ATTN_WIDTH:2 * ATTN_WIDTH].T
    vt_ref[...] = qkv[:, 2 * ATTN_WIDTH:3 * ATTN_WIDTH].T.astype(BF16)

    def head_norm(src, gain_ref, h):
        xh = src[h * HEAD_DIM:(h + 1) * HEAD_DIM, :]
        ssq = jnp.mean(xh * xh, axis=0, keepdims=True)
        return xh * lax.rsqrt(ssq + RMS_EPS) * gain_ref[...]

    k_pad = jnp.zeros((LANES - HEAD_ROWS, tm), F32)
    for h in range(N_HEADS):
        aug = slice(h * AUG_ROWS, (h + 1) * AUG_ROWS)
        r0 = h * HEAD_ROWS
        qta_ref[r0:r0 + HEAD_DIM, :] = head_norm(qt, gq_ref, h).astype(BF16)
        qta_ref[r0 + HEAD_DIM:r0 + HEAD_ROWS, :] = fq[aug, :].astype(BF16)
        k_blk = jnp.concatenate([head_norm(kt, gk_ref, h), fk[aug, :], k_pad], axis=0)
        ka_ref[:, h * LANES:(h + 1) * LANES] = k_blk.T.astype(BF16)

    u = jnp.dot(hn, wp_ref[...], preferred_element_type=F32)
    ubuf[MAX_WINDOW:MAX_WINDOW + tm, :] = u
    mixed = []
    for g, w in enumerate(POOL_WINDOWS):
        c0 = g * POOL_GROUP_DIM
        acc = u[:, c0:c0 + POOL_GROUP_DIM]
        for s in range(1, w):
            acc = acc + ubuf[MAX_WINDOW - s:MAX_WINDOW - s + tm, c0:c0 + POOL_GROUP_DIM]
        if first_pos + 1 >= w:
            mean = acc * (1.0 / w)
        else:
            pos = first_pos + j * tm + lax.broadcasted_iota(jnp.int32, (tm, POOL_GROUP_DIM), 0)
            mean = acc / jnp.minimum(pos + 1, w).astype(F32)
        mixed.append(mean - u[:, c0:c0 + POOL_GROUP_DIM])
    mixed = jnp.concatenate(mixed, axis=1).astype(BF16)
    y = jnp.dot(mixed, wpool_ref[...], preferred_element_type=F32) * pscale_ref[...]
    pooled_ref[...] = y.astype(BF16)
    tail = u[n_valid - MAX_WINDOW:n_valid, :]
    ubuf[0:MAX_WINDOW, :] = tail
    utail_ref[...] = tail

    gl = jnp.dot(hn, wg_ref[...], preferred_element_type=F32)
    gates_ref[...] = jax.nn.sigmoid(gl).astype(BF16)


def _inproj(x3, p, uprev, fprev, *, tm, n_valid, first_pos):
    nb, seq, _ = x3.shape
    nt = seq // tm
    gq = jnp.broadcast_to((p["gq"] * (LOG2_E / math.sqrt(HEAD_DIM)))[:, None], (HEAD_DIM, tm))
    gk = jnp.broadcast_to(p["gk"][:, None], (HEAD_DIM, tm))
    tok = lambda width: pl.BlockSpec((None, tm, width), lambda b, j: (b, j, 0))
    chan = lambda rows: pl.BlockSpec((None, rows, tm), lambda b, j: (b, 0, j))
    kern = functools.partial(_inproj_kernel, tm=tm, n_valid=n_valid, first_pos=first_pos)
    return pl.pallas_call(
        kern,
        grid=(nb, nt),
        in_specs=[
            tok(D_MODEL),
            _const_spec((1, D_MODEL)),
            _const_spec((D_MODEL, 3 * ATTN_WIDTH)),
            _const_spec((D_MODEL, LANES)),
            _const_spec((1, LANES)),
            _const_spec((D_MODEL, POOL_WIDTH)),
            _const_spec((D_MODEL, 2 * D_MODEL)),
            _const_spec((HEAD_DIM, tm)),
            _const_spec((HEAD_DIM, tm)),
            _const_spec((LANES, LANES)),
            _const_spec((LANES, LANES)),
            _const_spec((POOL_WIDTH, POOL_WIDTH)),
            _const_spec((1, POOL_WIDTH)),
            _const_spec((MAX_WINDOW, POOL_WIDTH)),
            _const_spec((N_HEADS, LANES)),
        ],
        out_specs=[
            chan(QK_ROWS), tok(N_HEADS * LANES), chan(ATTN_WIDTH),
            tok(POOL_WIDTH), tok(2 * D_MODEL),
            pl.BlockSpec((MAX_WINDOW, POOL_WIDTH), lambda b, j: (0, 0)),
            pl.BlockSpec((N_HEADS, LANES), lambda b, j: (0, 0)),
        ],
        out_shape=[
            jax.ShapeDtypeStruct((nb, QK_ROWS, seq), BF16),
            jax.ShapeDtypeStruct((nb, seq, N_HEADS * LANES), BF16),
            jax.ShapeDtypeStruct((nb, ATTN_WIDTH, seq), BF16),
            jax.ShapeDtypeStruct((nb, seq, POOL_WIDTH), BF16),
            jax.ShapeDtypeStruct((nb, seq, 2 * D_MODEL), BF16),
            jax.ShapeDtypeStruct((MAX_WINDOW, POOL_WIDTH), F32),
            jax.ShapeDtypeStruct((N_HEADS, LANES), F32),
        ],
        scratch_shapes=[
            pltpu.VMEM((MAX_WINDOW + tm, POOL_WIDTH), F32),
            pltpu.VMEM((N_HEADS, LANES), F32),
        ],
        compiler_params=pltpu.CompilerParams(
            dimension_semantics=("arbitrary", "arbitrary"), vmem_limit_bytes=VMEM_LIMIT),
        name="inproj",
    )(x3, p["g_mix"], p["wqkv"], p["wf"], p["bf"], p["wp"], p["wg"], gq, gk,
      p["selq"], p["selk"], p["wpool"], p["pscale"], uprev, fprev)


def _attn_kernel(qta_ref, ka_ref, vt_ref, kam_ref, vtm_ref, bias_ref, ot_ref, m_sc, acc_sc, *, tq):
    qi = pl.program_id(1)
    m_sc[...] = jnp.full(m_sc.shape, -jnp.inf, F32)
    acc_sc[...] = jnp.zeros(acc_sc.shape, F32)

    def rows(h):
        return slice(h * HEAD_ROWS, (h + 1) * HEAD_ROWS)

    def vrows(h):
        return slice(h * HEAD_DIM, (h + 1) * HEAD_DIM)

    def klanes(h):
        return slice(h * LANES, (h + 1) * LANES)

    q_pad = jnp.zeros((LANES - HEAD_ROWS, tq), BF16)

    def scores(h, ka):
        qa = jnp.concatenate([qta_ref[rows(h), :], q_pad], axis=0)
        return jnp.dot(ka, qa, preferred_element_type=F32)

    def absorb(h, s, v):
        m_old = m_sc[h]
        m_new = jnp.maximum(m_old, jnp.max(s, axis=0, keepdims=True))
        p = jnp.exp2(s - m_new).astype(BF16)
        va = jnp.concatenate([v, jnp.ones((AUG_ROWS, v.shape[1]), BF16)], axis=0)
        acc_sc[h] = (jnp.exp2(m_old - m_new) * acc_sc[h]
                     + jnp.dot(va, p, preferred_element_type=F32))
        m_sc[h] = m_new

    def sweep_heads(scores_of, values_of):
        s_next = scores_of(0)
        for h in range(N_HEADS):
            s_cur = s_next
            if h + 1 < N_HEADS:
                s_next = scores_of(h + 1)
            absorb(h, s_cur, values_of(h))

    def kv_body(i, carry):
        s0 = pl.multiple_of(i * tq, tq)
        sweep_heads(lambda h: scores(h, ka_ref[pl.ds(s0, tq), klanes(h)]),
                    lambda h: vt_ref[vrows(h), pl.ds(s0, tq)])
        return carry

    lax.fori_loop(0, qi, kv_body, 0)

    s0 = pl.multiple_of(qi * tq, tq)
    sweep_heads(
        lambda h: scores(h, jnp.concatenate(
            [ka_ref[pl.ds(s0, tq), klanes(h)], kam_ref[:, klanes(h)]], axis=0)) + bias_ref[...],
        lambda h: jnp.concatenate([vt_ref[vrows(h), pl.ds(s0, tq)], vtm_ref[vrows(h), :]], axis=1))
    for h in range(N_HEADS):
        acc = acc_sc[h]
        ot_ref[vrows(h), :] = (acc[0:HEAD_DIM, :] / acc[HEAD_DIM:HEAD_DIM + 1, :]).astype(BF16)


def _attention(qta, ka, vt, ka_m, vt_m, *, tq):
    nb, _, seq = qta.shape
    kw = N_HEADS * LANES
    key = lax.broadcasted_iota(jnp.int32, (tq + LANES, tq), 0)
    qry = lax.broadcasted_iota(jnp.int32, (tq + LANES, tq), 1)
    visible = jnp.where(key < tq, key <= qry, key - tq < N_META)
    bias = jnp.where(visible, 0.0, -jnp.inf).astype(F32)
    return pl.pallas_call(
        functools.partial(_attn_kernel, tq=tq),
        grid=(nb, seq // tq),
        in_specs=[
            pl.BlockSpec((None, QK_ROWS, tq), lambda b, i: (b, 0, i)),
            pl.BlockSpec((None, seq, kw), lambda b, i: (b, 0, 0)),
            pl.BlockSpec((None, ATTN_WIDTH, seq), lambda b, i: (b, 0, 0)),
            pl.BlockSpec((None, LANES, kw), lambda b, i: (0, 0, 0)),
            pl.BlockSpec((None, ATTN_WIDTH, LANES), lambda b, i: (0, 0, 0)),
            _const_spec((tq + LANES, tq)),
        ],
        out_specs=pl.BlockSpec((None, ATTN_WIDTH, tq), lambda b, i: (b, 0, i)),
        out_shape=jax.ShapeDtypeStruct((nb, ATTN_WIDTH, seq), BF16),
        scratch_shapes=[
            pltpu.VMEM((N_HEADS, 1, tq), F32),
            pltpu.VMEM((N_HEADS, HEAD_ROWS, tq), F32),
        ],
        compiler_params=pltpu.CompilerParams(
            dimension_semantics=("arbitrary", "arbitrary"), vmem_limit_bytes=VMEM_LIMIT),
        name="fox_attention",
    )(qta, ka, vt, ka_m, vt_m, bias)


def _post_kernel(ot_ref, pooled_ref, gates_ref, x_ref, wua_ref, wup_ref, wout_ref, g_ref,
                 wr_ref, br_ref, h2_ref, hn2_ref, comb_ref):
    y_attn = lax.dot_general(ot_ref[...], wua_ref[...], _TN, preferred_element_type=F32)
    y_pool = jnp.dot(pooled_ref[...], wup_ref[...], preferred_element_type=F32)
    merged = (gates_ref[:, 0:D_MODEL].astype(F32) * y_attn
              + gates_ref[:, D_MODEL:2 * D_MODEL].astype(F32) * y_pool)
    h2 = x_ref[...] + jnp.dot(merged.astype(BF16), wout_ref[...], preferred_element_type=F32)
    h2_ref[...] = h2
    ms = jnp.mean(h2 * h2, axis=-1, keepdims=True)
    hn2 = (h2 * lax.rsqrt(ms + RMS_EPS) * g_ref[...]).astype(BF16)
    hn2_ref[...] = hn2

    logits = jnp.dot(hn2, wr_ref[...], preferred_element_type=F32) + br_ref[...]
    lane = lax.broadcasted_iota(jnp.int32, logits.shape, 1)
    neg_inf = jnp.float32(-jnp.inf)

    def softmax_over(mask):
        z = jnp.where(mask, logits, neg_inf)
        e = jnp.exp(z - jnp.max(z, axis=1, keepdims=True))
        return jnp.where(mask, e / jnp.sum(e, axis=1, keepdims=True), -1.0)

    def top1(prob):
        top = jnp.max(prob, axis=1, keepdims=True)
        idx = jnp.min(jnp.where(prob == top, lane, LANES), axis=1, keepdims=True)
        return top, idx

    g_top, g_idx = top1(softmax_over(lane < N_GROUPS))
    e_lo = ROUTER_LANE0 + N_PER_GROUP * g_idx
    e_prob = softmax_over((lane >= e_lo) & (lane < e_lo + N_PER_GROUP))
    p1, i1 = top1(e_prob)
    p2, i2 = top1(jnp.where(lane == i1, -1.0, e_prob))
    denom = p1 + p2
    w1 = p1 / denom * g_top
    w2 = p2 / denom * g_top
    comb_ref[...] = (jnp.where(lane == i1 - ROUTER_LANE0, w1, 0.0)
                     + jnp.where(lane == i2 - ROUTER_LANE0, w2, 0.0))


def _post(ot, pooled, gates, x, p, *, tm):
    nb, seq, _ = x.shape
    tok = lambda width: pl.BlockSpec((None, tm, width), lambda b, j: (b, j, 0))
    return pl.pallas_call(
        _post_kernel,
        grid=(nb, seq // tm),
        in_specs=[
            pl.BlockSpec((None, ATTN_WIDTH, tm), lambda b, j: (b, 0, j)),
            tok(POOL_WIDTH), tok(2 * D_MODEL), tok(D_MODEL),
            _const_spec((ATTN_WIDTH, D_MODEL)),
            _const_spec((POOL_WIDTH, D_MODEL)),
            _const_spec((D_MODEL, D_MODEL)),
            _const_spec((1, D_MODEL)),
            _const_spec((D_MODEL, LANES)),
            _const_spec((1, LANES)),
        ],
        out_specs=[tok(D_MODEL), tok(D_MODEL), tok(LANES)],
        out_shape=[
            jax.ShapeDtypeStruct((nb, seq, D_MODEL), F32),
            jax.ShapeDtypeStruct((nb, seq, D_MODEL), BF16),
            jax.ShapeDtypeStruct((nb, seq, LANES), F32),
        ],
        compiler_params=pltpu.CompilerParams(
            dimension_semantics=("arbitrary", "arbitrary"), vmem_limit_bytes=VMEM_LIMIT),
        name="post_router",
    )(ot, pooled, gates, x, p["wua"], p["wup"], p["wout"], p["g_ffn"], p["wr"], p["br"])


def _moe_kernel(x_ref, h2_ref, comb_ref, w13_ref, w2_ref, out_ref):
    e = pl.program_id(1)

    @pl.when(e == 0)
    def _():
        out_ref[...] = h2_ref[...]

    h = jnp.dot(x_ref[...], w13_ref[...], preferred_element_type=F32)
    h1 = h[:, 0:EXPERT_HIDDEN]
    h3 = h[:, EXPERT_HIDDEN:2 * EXPERT_HIDDEN]
    comb = comb_ref[...]
    lane = lax.broadcasted_iota(jnp.int32, comb.shape, 1)
    ce = jnp.sum(jnp.where(lane == e, comb, 0.0), axis=1, keepdims=True)
    hh = h1 * jax.nn.sigmoid(h1) * h3 * ce
    out_ref[...] += jnp.dot(hh.astype(BF16), w2_ref[...], preferred_element_type=F32)


def _moe(hn2, h2, comb, p, *, tm):
    t = hn2.shape[0]
    tok = lambda width: pl.BlockSpec((tm, width), lambda i, e: (i, 0))
    return pl.pallas_call(
        _moe_kernel,
        grid=(t // tm, N_EXPERTS),
        in_specs=[
            tok(D_MODEL), tok(D_MODEL), tok(LANES),
            pl.BlockSpec((None, D_MODEL, 2 * EXPERT_HIDDEN), lambda i, e: (e, 0, 0)),
            pl.BlockSpec((None, EXPERT_HIDDEN, D_MODEL), lambda i, e: (e, 0, 0)),
        ],
        out_specs=tok(D_MODEL),
        out_shape=jax.ShapeDtypeStruct((t, D_MODEL), F32),
        compiler_params=pltpu.CompilerParams(
            dimension_semantics=("arbitrary", "arbitrary"), vmem_limit_bytes=VMEM_LIMIT),
        name="moe",
    )(hn2, h2, comb, p["w13"], p["w2"])


def _bias_selectors():
    selq = [[0.0] * LANES for _ in range(LANES)]
    selk = [[0.0] * LANES for _ in range(LANES)]
    ones_row = 3 * N_HEADS
    for h in range(N_HEADS):
        for part in range(3):
            selq[h * AUG_ROWS + part][part * N_HEADS + h] = 1.0
            selq[h * AUG_ROWS + 3 + part][ones_row] = 1.0
            selk[h * AUG_ROWS + part][ones_row] = 1.0
            selk[h * AUG_ROWS + 3 + part][part * N_HEADS + h] = -1.0
    return jnp.array(selq, BF16), jnp.array(selk, BF16)


def _prepare_params(norm_mix_g, w_in, b_forget, q_norm_g, k_norm_g, w_up_attn, w_pool,
                    pool_scale, w_up_pool, w_out, norm_ffn_g, w_group, b_group, w_router,
                    b_router, w1, w3, w2):
    aw, pw = ATTN_WIDTH, POOL_WIDTH
    f_off = 3 * aw
    p_off = f_off + N_HEADS
    g_off = p_off + pw
    selq, selk = _bias_selectors()
    pad_lanes = lambda a: jnp.pad(a, ((0, 0), (0, LANES - a.shape[1])))
    wpool = jnp.zeros((pw, pw), F32)
    for g in range(len(POOL_WINDOWS)):
        sl = slice(g * POOL_GROUP_DIM, (g + 1) * POOL_GROUP_DIM)
        wpool = wpool.at[sl, sl].set(w_pool[g])
    d, f = D_MODEL, EXPERT_HIDDEN
    return {
        "g_mix": norm_mix_g.reshape(1, d),
        "wqkv": w_in[:, 0:f_off].astype(BF16),
        "wf": pad_lanes(w_in[:, f_off:p_off]).astype(BF16),
        "bf": pad_lanes(b_forget.reshape(1, N_HEADS)),
        "wp": w_in[:, p_off:g_off].astype(BF16),
        "wg": w_in[:, g_off:].astype(BF16),
        "gq": q_norm_g, "gk": k_norm_g,
        "selq": selq, "selk": selk,
        "wpool": wpool.astype(BF16),
        "pscale": pool_scale.reshape(1, pw),
        "wua": w_up_attn.astype(BF16),
        "wup": w_up_pool.astype(BF16),
        "wout": w_out.astype(BF16),
        "g_ffn": norm_ffn_g.reshape(1, d),
        "wr": pad_lanes(jnp.concatenate([w_group, w_router], axis=1)).astype(BF16),
        "br": pad_lanes(jnp.concatenate([b_group, b_router]).reshape(1, -1)),
        "w13": jnp.concatenate([w1, w3], axis=-1).reshape(N_EXPERTS, d, 2 * f).astype(BF16),
        "w2": w2.reshape(N_EXPERTS, f, d).astype(BF16),
    }


def kernel(x, meta_tokens, norm_mix_g, w_in, b_forget, q_norm_g, k_norm_g, w_up_attn, w_pool,
           pool_scale, w_up_pool, w_out, norm_ffn_g, w_group, b_group, w_router, b_router,
           w1, w3, w2):
    nb, seq, d = x.shape
    p = _prepare_params(norm_mix_g[0], w_in[0], b_forget[0], q_norm_g[0], k_norm_g[0],
                        w_up_attn[0], w_pool[0], pool_scale[0], w_up_pool[0], w_out[0],
                        norm_ffn_g[0], w_group[0], b_group[0], w_router[0], b_router[0],
                        w1[0], w3[0], w2[0])

    meta = jnp.pad(meta_tokens.astype(x.dtype), ((0, LANES - N_META), (0, 0)))[None]
    _, ka_m, vt_m, _, _, u_meta, f_meta = _inproj(
        meta, p, jnp.zeros((MAX_WINDOW, POOL_WIDTH), F32), jnp.zeros((N_HEADS, LANES), F32),
        tm=LANES, n_valid=N_META, first_pos=0)

    tm = 512
    qta, ka, vt, pooled, gates, _, _ = _inproj(
        x, p, u_meta, f_meta, tm=tm, n_valid=tm, first_pos=N_META)
    ot = _attention(qta, ka, vt, ka_m, vt_m, tq=512)
    h2, hn2, comb = _post(ot, pooled, gates, x, p, tm=512)
    t = nb * seq
    out = _moe(hn2.reshape(t, d), h2.reshape(t, d), comb.reshape(t, LANES), p, tm=1024)
    return out.reshape(nb, seq, d)
```

```python
import functools
import math

import jax
import jax.numpy as jnp
from jax import lax
from jax.experimental import pallas as pl
from jax.experimental.pallas import tpu as pltpu

F32 = jnp.float32
BF16 = jnp.bfloat16

D_MODEL = 1024
N_META = 16
N_HEADS = 8
HEAD_DIM = 64
ATTN_WIDTH = N_HEADS * HEAD_DIM
POOL_WINDOWS = (2, 4, 8, 16)
POOL_WIDTH = 512
POOL_GROUP_DIM = POOL_WIDTH // len(POOL_WINDOWS)
MAX_WINDOW = max(POOL_WINDOWS)
N_GROUPS = 4
N_PER_GROUP = 8
N_EXPERTS = N_GROUPS * N_PER_GROUP
EXPERT_HIDDEN = 256
RMS_EPS = 1e-6
LOG2_E = math.log2(math.e)

LANES = 128
AUG_ROWS = 16
HEAD_ROWS = HEAD_DIM + AUG_ROWS
QK_ROWS = N_HEADS * HEAD_ROWS
ROUTER_LANE0 = N_GROUPS
PAIRS_PER_GROUP = N_PER_GROUP * (N_PER_GROUP - 1) // 2
N_CLASSES = N_GROUPS * PAIRS_PER_GROUP
ROW_WIDTH = D_MODEL + LANES
INFO_CLASS, INFO_RANK, INFO_W_LO, INFO_W_HI = 0, 1, 2, 3
MOE_TILE = 128
VMEM_LIMIT = 56 * 1024 * 1024

_TN = (((0,), (0,)), ((), ()))


def _const_spec(shape):
    zeros = (0,) * len(shape)
    return pl.BlockSpec(shape, lambda *_: zeros, pipeline_mode=pl.Buffered(1))


def _split3(x):
    hi = x.astype(BF16).astype(F32)
    r = x - hi
    mid = r.astype(BF16).astype(F32)
    lo = (r - mid).astype(BF16).astype(F32)
    return hi, mid, lo


def _inproj_kernel(x_ref, g_ref, wqkv_ref, wf_ref, bf_ref, wp_ref, wg_ref, gq_ref, gk_ref,
                   selq_ref, selk_ref, wpool_ref, pscale_ref, uprev_ref, fprev_ref,
                   qta_ref, ka_ref, vt_ref, pooled_ref, gates_ref, utail_ref, ftail_ref,
                   ubuf, fcarry, *, tm, n_valid, first_pos):
    j = pl.program_id(1)

    @pl.when(j == 0)
    def _():
        ubuf[0:MAX_WINDOW, :] = uprev_ref[...]
        fcarry[...] = fprev_ref[...]

    x = x_ref[...]
    ms = jnp.mean(x * x, axis=-1, keepdims=True)
    hn = (x * lax.rsqrt(ms + RMS_EPS) * g_ref[...]).astype(BF16)

    f = jnp.dot(hn, wf_ref[...], preferred_element_type=F32) + bf_ref[...]
    z = f.T[0:N_HEADS, :]
    logf = (jnp.minimum(z, 0.0) - jnp.log1p(jnp.exp(-jnp.abs(z)))) * LOG2_E
    parts = jnp.concatenate(_split3(logf), axis=0).astype(BF16)
    r_i = lax.broadcasted_iota(jnp.int32, (tm, tm), 0)
    c_i = lax.broadcasted_iota(jnp.int32, (tm, tm), 1)
    tri = jnp.where(r_i <= c_i, 1.0, 0.0).astype(BF16)
    cs3 = jnp.dot(parts, tri, preferred_element_type=F32)
    cs = cs3[0:8] + cs3[8:16] + cs3[16:24]
    fc = jnp.concatenate([fcarry[...]] * (tm // LANES), axis=1) + cs
    last = jnp.broadcast_to(fc[:, n_valid - 1:n_valid], (N_HEADS, LANES))
    fcarry[...] = last
    ftail_ref[...] = last

    pieces = jnp.concatenate(
        _split3(fc) + (jnp.ones((8, tm), F32), jnp.zeros((LANES - 32, tm), F32)),
        axis=0).astype(BF16)
    fq = jnp.dot(selq_ref[...], pieces, preferred_element_type=F32)
    fk = jnp.dot(selk_ref[...], pieces, preferred_element_type=F32)

    qkv = jnp.dot(hn, wqkv_ref[...], preferred_element_type=F32)
    qt = qkv[:, 0:ATTN_WIDTH].T
    kt = qkv[:, ATTN_WIDTH:2 * ATTN_WIDTH].T
    vt_ref[...] = qkv[:, 2 * ATTN_WIDTH:3 * ATTN_WIDTH].T.astype(BF16)

    def head_norm(src, gain_ref, h):
        xh = src[h * HEAD_DIM:(h + 1) * HEAD_DIM, :]
        ssq = jnp.mean(xh * xh, axis=0, keepdims=True)
        return xh * lax.rsqrt(ssq + RMS_EPS) * gain_ref[...]

    k_pad = jnp.zeros((LANES - HEAD_ROWS, tm), F32)
    for h in range(N_HEADS):
        aug = slice(h * AUG_ROWS, (h + 1) * AUG_ROWS)
        r0 = h * HEAD_ROWS
        qta_ref[r0:r0 + HEAD_DIM, :] = head_norm(qt, gq_ref, h).astype(BF16)
        qta_ref[r0 + HEAD_DIM:r0 + HEAD_ROWS, :] = fq[aug, :].astype(BF16)
        k_blk = jnp.concatenate([head_norm(kt, gk_ref, h), fk[aug, :], k_pad], axis=0)
        ka_ref[:, h * LANES:(h + 1) * LANES] = k_blk.T.astype(BF16)

    u = jnp.dot(hn, wp_ref[...], preferred_element_type=F32)
    ubuf[MAX_WINDOW:MAX_WINDOW + tm, :] = u
    mixed = []
    for g, w in enumerate(POOL_WINDOWS):
        c0 = g * POOL_GROUP_DIM
        acc = u[:, c0:c0 + POOL_GROUP_DIM]
        for s in range(1, w):
            acc = acc + ubuf[MAX_WINDOW - s:MAX_WINDOW - s + tm, c0:c0 + POOL_GROUP_DIM]
        if first_pos + 1 >= w:
            mean = acc * (1.0 / w)
        else:
            pos = first_pos + j * tm + lax.broadcasted_iota(jnp.int32, (tm, POOL_GROUP_DIM), 0)
            mean = acc / jnp.minimum(pos + 1, w).astype(F32)
        mixed.append(mean - u[:, c0:c0 + POOL_GROUP_DIM])
    mixed = jnp.concatenate(mixed, axis=1).astype(BF16)
    y = jnp.dot(mixed, wpool_ref[...], preferred_element_type=F32) * pscale_ref[...]
    pooled_ref[...] = y.astype(BF16)
    tail = u[n_valid - MAX_WINDOW:n_valid, :]
    ubuf[0:MAX_WINDOW, :] = tail
    utail_ref[...] = tail

    gl = jnp.dot(hn, wg_ref[...], preferred_element_type=F32)
    gates_ref[...] = jax.nn.sigmoid(gl).astype(BF16)


def _inproj(x3, p, uprev, fprev, *, tm, n_valid, first_pos):
    nb, seq, _ = x3.shape
    nt = seq // tm
    gq = jnp.broadcast_to((p["gq"] * (LOG2_E / math.sqrt(HEAD_DIM)))[:, None], (HEAD_DIM, tm))
    gk = jnp.broadcast_to(p["gk"][:, None], (HEAD_DIM, tm))
    tok = lambda width: pl.BlockSpec((None, tm, width), lambda b, j: (b, j, 0))
    chan = lambda rows: pl.BlockSpec((None, rows, tm), lambda b, j: (b, 0, j))
    kern = functools.partial(_inproj_kernel, tm=tm, n_valid=n_valid, first_pos=first_pos)
    return pl.pallas_call(
        kern,
        grid=(nb, nt),
        in_specs=[
            tok(D_MODEL),
            _const_spec((1, D_MODEL)),
            _const_spec((D_MODEL, 3 * ATTN_WIDTH)),
            _const_spec((D_MODEL, LANES)),
            _const_spec((1, LANES)),
            _const_spec((D_MODEL, POOL_WIDTH)),
            _const_spec((D_MODEL, 2 * D_MODEL)),
            _const_spec((HEAD_DIM, tm)),
            _const_spec((HEAD_DIM, tm)),
            _const_spec((LANES, LANES)),
            _const_spec((LANES, LANES)),
            _const_spec((POOL_WIDTH, POOL_WIDTH)),
            _const_spec((1, POOL_WIDTH)),
            _const_spec((MAX_WINDOW, POOL_WIDTH)),
            _const_spec((N_HEADS, LANES)),
        ],
        out_specs=[
            chan(QK_ROWS), tok(N_HEADS * LANES), chan(ATTN_WIDTH),
            tok(POOL_WIDTH), tok(2 * D_MODEL),
            pl.BlockSpec((MAX_WINDOW, POOL_WIDTH), lambda b, j: (0, 0)),
            pl.BlockSpec((N_HEADS, LANES), lambda b, j: (0, 0)),
        ],
        out_shape=[
            jax.ShapeDtypeStruct((nb, QK_ROWS, seq), BF16),
            jax.ShapeDtypeStruct((nb, seq, N_HEADS * LANES), BF16),
            jax.ShapeDtypeStruct((nb, ATTN_WIDTH, seq), BF16),
            jax.ShapeDtypeStruct((nb, seq, POOL_WIDTH), BF16),
            jax.ShapeDtypeStruct((nb, seq, 2 * D_MODEL), BF16),
            jax.ShapeDtypeStruct((MAX_WINDOW, POOL_WIDTH), F32),
            jax.ShapeDtypeStruct((N_HEADS, LANES), F32),
        ],
        scratch_shapes=[
            pltpu.VMEM((MAX_WINDOW + tm, POOL_WIDTH), F32),
            pltpu.VMEM((N_HEADS, LANES), F32),
        ],
        compiler_params=pltpu.CompilerParams(
            dimension_semantics=("arbitrary", "arbitrary"), vmem_limit_bytes=VMEM_LIMIT),
        name="inproj",
    )(x3, p["g_mix"], p["wqkv"], p["wf"], p["bf"], p["wp"], p["wg"], gq, gk,
      p["selq"], p["selk"], p["wpool"], p["pscale"], uprev, fprev)


def _attn_kernel(qta_ref, ka_ref, vt_ref, kam_ref, vtm_ref, bias_ref, ot_ref, m_sc, acc_sc, *, tq):
    qi = pl.program_id(1)
    m_sc[...] = jnp.full(m_sc.shape, -jnp.inf, F32)
    acc_sc[...] = jnp.zeros(acc_sc.shape, F32)

    def rows(h):
        return slice(h * HEAD_ROWS, (h + 1) * HEAD_ROWS)

    def vrows(h):
        return slice(h * HEAD_DIM, (h + 1) * HEAD_DIM)

    def klanes(h):
        return slice(h * LANES, (h + 1) * LANES)

    q_pad = jnp.zeros((LANES - HEAD_ROWS, tq), BF16)

    def scores(h, ka):
        qa = jnp.concatenate([qta_ref[rows(h), :], q_pad], axis=0)
        return jnp.dot(ka, qa, preferred_element_type=F32)

    def absorb(h, s, v):
        m_old = m_sc[h]
        m_new = jnp.maximum(m_old, jnp.max(s, axis=0, keepdims=True))
        p = jnp.exp2(s - m_new).astype(BF16)
        va = jnp.concatenate([v, jnp.ones((AUG_ROWS, v.shape[1]), BF16)], axis=0)
        acc_sc[h] = (jnp.exp2(m_old - m_new) * acc_sc[h]
                     + jnp.dot(va, p, preferred_element_type=F32))
        m_sc[h] = m_new

    def sweep_heads(scores_of, values_of):
        s_next = scores_of(0)
        for h in range(N_HEADS):
            s_cur = s_next
            if h + 1 < N_HEADS:
                s_next = scores_of(h + 1)
            absorb(h, s_cur, values_of(h))

    def kv_body(i, carry):
        s0 = pl.multiple_of(i * tq, tq)
        sweep_heads(lambda h: scores(h, ka_ref[pl.ds(s0, tq), klanes(h)]),
                    lambda h: vt_ref[vrows(h), pl.ds(s0, tq)])
        return carry

    lax.fori_loop(0, qi, kv_body, 0)

    s0 = pl.multiple_of(qi * tq, tq)
    sweep_heads(
        lambda h: scores(h, jnp.concatenate(
            [ka_ref[pl.ds(s0, tq), klanes(h)], kam_ref[:, klanes(h)]], axis=0)) + bias_ref[...],
        lambda h: jnp.concatenate([vt_ref[vrows(h), pl.ds(s0, tq)], vtm_ref[vrows(h), :]], axis=1))
    for h in range(N_HEADS):
        acc = acc_sc[h]
        ot_ref[vrows(h), :] = (acc[0:HEAD_DIM, :] / acc[HEAD_DIM:HEAD_DIM + 1, :]).astype(BF16)


def _attention(qta, ka, vt, ka_m, vt_m, *, tq):
    nb, _, seq = qta.shape
    kw = N_HEADS * LANES
    key = lax.broadcasted_iota(jnp.int32, (tq + LANES, tq), 0)
    qry = lax.broadcasted_iota(jnp.int32, (tq + LANES, tq), 1)
    visible = jnp.where(key < tq, key <= qry, key - tq < N_META)
    bias = jnp.where(visible, 0.0, -jnp.inf).astype(F32)
    return pl.pallas_call(
        functools.partial(_attn_kernel, tq=tq),
        grid=(nb, seq // tq),
        in_specs=[
            pl.BlockSpec((None, QK_ROWS, tq), lambda b, i: (b, 0, i)),
            pl.BlockSpec((None, seq, kw), lambda b, i: (b, 0, 0)),
            pl.BlockSpec((None, ATTN_WIDTH, seq), lambda b, i: (b, 0, 0)),
            pl.BlockSpec((None, LANES, kw), lambda b, i: (0, 0, 0)),
            pl.BlockSpec((None, ATTN_WIDTH, LANES), lambda b, i: (0, 0, 0)),
            _const_spec((tq + LANES, tq)),
        ],
        out_specs=pl.BlockSpec((None, ATTN_WIDTH, tq), lambda b, i: (b, 0, i)),
        out_shape=jax.ShapeDtypeStruct((nb, ATTN_WIDTH, seq), BF16),
        scratch_shapes=[
            pltpu.VMEM((N_HEADS, 1, tq), F32),
            pltpu.VMEM((N_HEADS, HEAD_ROWS, tq), F32),
        ],
        compiler_params=pltpu.CompilerParams(
            dimension_semantics=("arbitrary", "arbitrary"), vmem_limit_bytes=VMEM_LIMIT),
        name="fox_attention",
    )(qta, ka, vt, ka_m, vt_m, bias)


def _post_kernel(ot_ref, pooled_ref, gates_ref, x_ref, wua_ref, wup_ref, wout_ref, g_ref,
                 wr_ref, br_ref, h2x_ref, cnt_ref, cnt_sc, *, tm):
    @pl.when((pl.program_id(0) == 0) & (pl.program_id(1) == 0))
    def _():
        cnt_sc[...] = jnp.zeros(cnt_sc.shape, F32)

    y_attn = lax.dot_general(ot_ref[...], wua_ref[...], _TN, preferred_element_type=F32)
    y_pool = jnp.dot(pooled_ref[...], wup_ref[...], preferred_element_type=F32)
    merged = (gates_ref[:, 0:D_MODEL].astype(F32) * y_attn
              + gates_ref[:, D_MODEL:2 * D_MODEL].astype(F32) * y_pool)
    h2 = x_ref[...] + jnp.dot(merged.astype(BF16), wout_ref[...], preferred_element_type=F32)
    h2x_ref[:, 0:D_MODEL] = h2
    ms = jnp.mean(h2 * h2, axis=-1, keepdims=True)
    hn2 = (h2 * lax.rsqrt(ms + RMS_EPS) * g_ref[...]).astype(BF16)

    logits = jnp.dot(hn2, wr_ref[...], preferred_element_type=F32) + br_ref[...]
    lane = lax.broadcasted_iota(jnp.int32, logits.shape, 1)
    neg_inf = jnp.float32(-jnp.inf)

    def softmax_over(mask):
        z = jnp.where(mask, logits, neg_inf)
        e = jnp.exp(z - jnp.max(z, axis=1, keepdims=True))
        return jnp.where(mask, e / jnp.sum(e, axis=1, keepdims=True), -1.0)

    def top1(prob):
        top = jnp.max(prob, axis=1, keepdims=True)
        idx = jnp.min(jnp.where(prob == top, lane, LANES), axis=1, keepdims=True)
        return top, idx

    g_top, g_idx = top1(softmax_over(lane < N_GROUPS))
    e_lo = ROUTER_LANE0 + N_PER_GROUP * g_idx
    e_prob = softmax_over((lane >= e_lo) & (lane < e_lo + N_PER_GROUP))
    p1, i1 = top1(e_prob)
    p2, i2 = top1(jnp.where(lane == i1, -1.0, e_prob))
    denom = p1 + p2
    w1 = p1 / denom * g_top
    w2 = p2 / denom * g_top

    first_is_lo = i1 < i2
    a = jnp.minimum(i1, i2) - e_lo
    b = jnp.maximum(i1, i2) - e_lo
    w_lo = jnp.where(first_is_lo, w1, w2)
    w_hi = jnp.where(first_is_lo, w2, w1)
    pair = ((a * (2 * N_PER_GROUP - 1 - a)) >> 1) + (b - a - 1)
    cls = g_idx * PAIRS_PER_GROUP + pair

    member = lane == cls
    onehot = jnp.where(member, 1.0, 0.0)
    r_i = lax.broadcasted_iota(jnp.int32, (tm, tm), 0)
    c_i = lax.broadcasted_iota(jnp.int32, (tm, tm), 1)
    earlier = jnp.where(c_i < r_i, 1.0, 0.0).astype(BF16)
    prefix = jnp.dot(earlier, onehot.astype(BF16), preferred_element_type=F32)
    base = cnt_sc[0:1, :]
    rank = jnp.sum(jnp.where(member, prefix + base, 0.0), axis=1, keepdims=True)
    total = jnp.broadcast_to(base + jnp.sum(onehot, axis=0, keepdims=True), cnt_sc.shape)
    cnt_sc[...] = total
    cnt_ref[...] = total

    info = jnp.where(lane == INFO_CLASS, cls.astype(F32),
                     jnp.where(lane == INFO_RANK, rank,
                               jnp.where(lane == INFO_W_LO, w_lo,
                                         jnp.where(lane == INFO_W_HI, w_hi, 0.0))))
    h2x_ref[:, D_MODEL:ROW_WIDTH] = info


def _post(ot, pooled, gates, x, p, *, tm):
    nb, seq, _ = x.shape
    tok = lambda width: pl.BlockSpec((None, tm, width), lambda b, j: (b, j, 0))
    return pl.pallas_call(
        functools.partial(_post_kernel, tm=tm),
        grid=(nb, seq // tm),
        in_specs=[
            pl.BlockSpec((None, ATTN_WIDTH, tm), lambda b, j: (b, 0, j)),
            tok(POOL_WIDTH), tok(2 * D_MODEL), tok(D_MODEL),
            _const_spec((ATTN_WIDTH, D_MODEL)),
            _const_spec((POOL_WIDTH, D_MODEL)),
            _const_spec((D_MODEL, D_MODEL)),
            _const_spec((1, D_MODEL)),
            _const_spec((D_MODEL, LANES)),
            _const_spec((1, LANES)),
        ],
        out_specs=[tok(ROW_WIDTH), pl.BlockSpec((8, LANES), lambda b, j: (0, 0))],
        out_shape=[
            jax.ShapeDtypeStruct((nb, seq, ROW_WIDTH), F32),
            jax.ShapeDtypeStruct((8, LANES), F32),
        ],
        scratch_shapes=[pltpu.VMEM((8, LANES), F32)],
        compiler_params=pltpu.CompilerParams(
            dimension_semantics=("arbitrary", "arbitrary"), vmem_limit_bytes=VMEM_LIMIT),
        name="post_router",
    )(ot, pooled, gates, x, p["wua"], p["wup"], p["wout"], p["g_ffn"], p["wr"], p["br"])


def _moe_kernel(ea_ref, eb_ref, nvalid_ref, nused_ref,
                tok_ref, tok_next_ref, g_ref, w13a_ref, w13b_ref, w2a_ref, w2b_ref, h2x_hbm,
                out_hbm, xbuf, ybuf, gsem, ssem):
    del ea_ref, eb_ref
    i = pl.program_id(0)
    n_used = nused_ref[0]
    slot = i % 2

    def gather_row(tok_ids, r, s):
        return pltpu.make_async_copy(h2x_hbm.at[pl.ds(tok_ids[0, r], 1), :],
                                     xbuf.at[s, pl.ds(r, 1), :], gsem.at[s])

    def scatter_row(r, s, dst_row):
        return pltpu.make_async_copy(ybuf.at[s, pl.ds(r, 1), :],
                                     out_hbm.at[pl.ds(dst_row, 1), :], ssem.at[s])

    @pl.when(i == 0)
    def _():
        for r in range(MOE_TILE):
            gather_row(tok_ref, r, 0).start()

    @pl.when(i + 1 < n_used)
    def _():
        for r in range(MOE_TILE):
            gather_row(tok_next_ref, r, 1 - slot).start()

    @pl.when((i >= 2) & (i - 2 < n_used))
    def _():
        def body(r, c):
            scatter_row(0, slot, 0).wait()
            return c
        lax.fori_loop(0, nvalid_ref[jnp.maximum(i - 2, 0)], body, 0)

    @pl.when(i < n_used)
    def _():
        for r in range(MOE_TILE):
            gather_row(tok_ref, r, slot).wait()
        row = xbuf[slot]
        h2 = row[:, 0:D_MODEL]
        w_lo = row[:, D_MODEL + INFO_W_LO:D_MODEL + INFO_W_LO + 1]
        w_hi = row[:, D_MODEL + INFO_W_HI:D_MODEL + INFO_W_HI + 1]
        ms = jnp.mean(h2 * h2, axis=-1, keepdims=True)
        xn = (h2 * lax.rsqrt(ms + RMS_EPS) * g_ref[...]).astype(BF16)

        def expert(w13_ref, w2_ref, weight):
            h = jnp.dot(xn, w13_ref[...], preferred_element_type=F32)
            h1 = h[:, 0:EXPERT_HIDDEN]
            h3 = h[:, EXPERT_HIDDEN:2 * EXPERT_HIDDEN]
            hh = h1 * jax.nn.sigmoid(h1) * h3 * weight
            return jnp.dot(hh.astype(BF16), w2_ref[...], preferred_element_type=F32)

        ybuf[slot] = h2 + (expert(w13a_ref, w2a_ref, w_lo) + expert(w13b_ref, w2b_ref, w_hi))

        def body(r, c):
            scatter_row(r, slot, tok_ref[0, r]).start()
            return c
        lax.fori_loop(0, nvalid_ref[i], body, 0)


def _class_experts():
    ea, eb = [], []
    for g in range(N_GROUPS):
        for a in range(N_PER_GROUP):
            for b in range(a + 1, N_PER_GROUP):
                ea.append(g * N_PER_GROUP + a)
                eb.append(g * N_PER_GROUP + b)
    fill = LANES - len(ea)
    return (jnp.array(ea + [ea[-1]] * fill, jnp.int32),
            jnp.array(eb + [eb[-1]] * fill, jnp.int32))


def _moe(h2x, counts, p):
    t = h2x.shape[0]
    n_tiles = t // MOE_TILE + N_CLASSES + 2
    ntile_c = (counts + MOE_TILE - 1) // MOE_TILE
    tile_end = jnp.cumsum(ntile_c)
    tile_start = tile_end - ntile_c
    n_used = tile_end[-1]
    cls = h2x[:, D_MODEL + INFO_CLASS].astype(jnp.int32)
    rank = h2x[:, D_MODEL + INFO_RANK].astype(jnp.int32)
    pos = tile_start[cls] * MOE_TILE + rank
    tok = jnp.zeros((n_tiles * MOE_TILE,), jnp.int32).at[pos].set(jnp.arange(t, dtype=jnp.int32))
    tok = tok.reshape(n_tiles, 1, MOE_TILE)
    tile_ids = jnp.arange(n_tiles, dtype=jnp.int32)
    last = jnp.maximum(n_used - 1, 0)
    tcls = jnp.sum(tile_end[None, :] <= jnp.minimum(tile_ids, last)[:, None], axis=1,
                   dtype=jnp.int32)
    nvalid = jnp.clip(counts[tcls] - (tile_ids - tile_start[tcls]) * MOE_TILE, 0, MOE_TILE)
    nvalid = jnp.where(tile_ids < n_used, nvalid, 0).astype(jnp.int32)
    ea_tab, eb_tab = _class_experts()
    ea, eb = ea_tab[tcls], eb_tab[tcls]

    smem_tile = lambda fn: pl.BlockSpec((None, 1, MOE_TILE), fn, memory_space=pltpu.SMEM)
    first = lambda ea, eb: ea
    second = lambda ea, eb: eb
    w13 = lambda sel: pl.BlockSpec((None, D_MODEL, 2 * EXPERT_HIDDEN),
                                   lambda i, ea, eb, nv, nu: (sel(ea, eb)[i], 0, 0))
    w2 = lambda sel: pl.BlockSpec((None, EXPERT_HIDDEN, D_MODEL),
                                  lambda i, ea, eb, nv, nu: (sel(ea, eb)[i], 0, 0))
    return pl.pallas_call(
        _moe_kernel,
        grid_spec=pltpu.PrefetchScalarGridSpec(
            num_scalar_prefetch=4,
            grid=(n_tiles,),
            in_specs=[
                smem_tile(lambda i, *_: (i, 0, 0)),
                smem_tile(lambda i, *_: (jnp.minimum(i + 1, n_tiles - 1), 0, 0)),
                pl.BlockSpec((1, D_MODEL), lambda i, *_: (0, 0)),
                w13(first), w13(second), w2(first), w2(second),
                pl.BlockSpec(memory_space=pl.ANY),
            ],
            out_specs=pl.BlockSpec(memory_space=pl.ANY),
            scratch_shapes=[
                pltpu.VMEM((2, MOE_TILE, ROW_WIDTH), F32),
                pltpu.VMEM((2, MOE_TILE, D_MODEL), F32),
                pltpu.SemaphoreType.DMA((2,)),
                pltpu.SemaphoreType.DMA((2,)),
            ],
        ),
        out_shape=jax.ShapeDtypeStruct((t, D_MODEL), F32),
        compiler_params=pltpu.CompilerParams(
            dimension_semantics=("arbitrary",), vmem_limit_bytes=VMEM_LIMIT),
        name="moe",
    )(ea, eb, nvalid, n_used.reshape(1).astype(jnp.int32), tok, tok, p["g_ffn"],
      p["w13"], p["w13"], p["w2"], p["w2"], h2x)


def _bias_selectors():
    selq = [[0.0] * LANES for _ in range(LANES)]
    selk = [[0.0] * LANES for _ in range(LANES)]
    ones_row = 3 * N_HEADS
    for h in range(N_HEADS):
        for part in range(3):
            selq[h * AUG_ROWS + part][part * N_HEADS + h] = 1.0
            selq[h * AUG_ROWS + 3 + part][ones_row] = 1.0
            selk[h * AUG_ROWS + part][ones_row] = 1.0
            selk[h * AUG_ROWS + 3 + part][part * N_HEADS + h] = -1.0
    return jnp.array(selq, BF16), jnp.array(selk, BF16)


def _prepare_params(norm_mix_g, w_in, b_forget, q_norm_g, k_norm_g, w_up_attn, w_pool,
                    pool_scale, w_up_pool, w_out, norm_ffn_g, w_group, b_group, w_router,
                    b_router, w1, w3, w2):
    aw, pw = ATTN_WIDTH, POOL_WIDTH
    f_off = 3 * aw
    p_off = f_off + N_HEADS
    g_off = p_off + pw
    selq, selk = _bias_selectors()
    pad_lanes = lambda a: jnp.pad(a, ((0, 0), (0, LANES - a.shape[1])))
    wpool = jnp.zeros((pw, pw), F32)
    for g in range(len(POOL_WINDOWS)):
        sl = slice(g * POOL_GROUP_DIM, (g + 1) * POOL_GROUP_DIM)
        wpool = wpool.at[sl, sl].set(w_pool[g])
    d, f = D_MODEL, EXPERT_HIDDEN
    return {
        "g_mix": norm_mix_g.reshape(1, d),
        "wqkv": w_in[:, 0:f_off].astype(BF16),
        "wf": pad_lanes(w_in[:, f_off:p_off]).astype(BF16),
        "bf": pad_lanes(b_forget.reshape(1, N_HEADS)),
        "wp": w_in[:, p_off:g_off].astype(BF16),
        "wg": w_in[:, g_off:].astype(BF16),
        "gq": q_norm_g, "gk": k_norm_g,
        "selq": selq, "selk": selk,
        "wpool": wpool.astype(BF16),
        "pscale": pool_scale.reshape(1, pw),
        "wua": w_up_attn.astype(BF16),
        "wup": w_up_pool.astype(BF16),
        "wout": w_out.astype(BF16),
        "g_ffn": norm_ffn_g.reshape(1, d),
        "wr": pad_lanes(jnp.concatenate([w_group, w_router], axis=1)).astype(BF16),
        "br": pad_lanes(jnp.concatenate([b_group, b_router]).reshape(1, -1)),
        "w13": jnp.concatenate([w1, w3], axis=-1).reshape(N_EXPERTS, d, 2 * f).astype(BF16),
        "w2": w2.reshape(N_EXPERTS, f, d).astype(BF16),
    }


def kernel(x, meta_tokens, norm_mix_g, w_in, b_forget, q_norm_g, k_norm_g, w_up_attn, w_pool,
           pool_scale, w_up_pool, w_out, norm_ffn_g, w_group, b_group, w_router, b_router,
           w1, w3, w2):
    nb, seq, d = x.shape
    p = _prepare_params(norm_mix_g[0], w_in[0], b_forget[0], q_norm_g[0], k_norm_g[0],
                        w_up_attn[0], w_pool[0], pool_scale[0], w_up_pool[0], w_out[0],
                        norm_ffn_g[0], w_group[0], b_group[0], w_router[0], b_router[0],
                        w1[0], w3[0], w2[0])

    meta = jnp.pad(meta_tokens.astype(x.dtype), ((0, LANES - N_META), (0, 0)))[None]
    _, ka_m, vt_m, _, _, u_meta, f_meta = _inproj(
        meta, p, jnp.zeros((MAX_WINDOW, POOL_WIDTH), F32), jnp.zeros((N_HEADS, LANES), F32),
        tm=LANES, n_valid=N_META, first_pos=0)

    tm = 512
    qta, ka, vt, pooled, gates, _, _ = _inproj(
        x, p, u_meta, f_meta, tm=tm, n_valid=tm, first_pos=N_META)
    ot = _attention(qta, ka, vt, ka_m, vt_m, tq=512)
    h2x, cnt = _post(ot, pooled, gates, x, p, tm=512)
    out = _moe(h2x.reshape(nb * seq, ROW_WIDTH), cnt[0].astype(jnp.int32), p)
    return out.reshape(nb, seq, d)
```

```python
import functools
import math

import jax
import jax.numpy as jnp
from jax import lax
from jax.experimental import pallas as pl
from jax.experimental.pallas import tpu as pltpu

F32 = jnp.float32
BF16 = jnp.bfloat16

D_MODEL = 1024
N_META = 16
N_HEADS = 8
HEAD_DIM = 64
ATTN_WIDTH = N_HEADS * HEAD_DIM
POOL_WINDOWS = (2, 4, 8, 16)
POOL_WIDTH = 512
POOL_GROUP_DIM = POOL_WIDTH // len(POOL_WINDOWS)
MAX_WINDOW = max(POOL_WINDOWS)
N_GROUPS = 4
N_PER_GROUP = 8
N_EXPERTS = N_GROUPS * N_PER_GROUP
EXPERT_HIDDEN = 256
RMS_EPS = 1e-6
LOG2_E = math.log2(math.e)

LANES = 128
AUG_ROWS = 16
HEAD_ROWS = HEAD_DIM + AUG_ROWS
QK_ROWS = N_HEADS * HEAD_ROWS
ROUTER_LANE0 = N_GROUPS
PAIRS_PER_GROUP = N_PER_GROUP * (N_PER_GROUP - 1) // 2
N_CLASSES = N_GROUPS * PAIRS_PER_GROUP
INFO_CLASS, INFO_RANK, INFO_W_LO, INFO_W_HI = 0, 1, 2, 3
MOE_TILE = 128
SLOT_ROWS = D_MODEL // LANES
IN_SLOT_ROWS = 2 * SLOT_ROWS
DMA_UNROLL = 8
VMEM_LIMIT = 56 * 1024 * 1024

_TN = (((0,), (0,)), ((), ()))


def _const_spec(shape):
    zeros = (0,) * len(shape)
    return pl.BlockSpec(shape, lambda *_: zeros, pipeline_mode=pl.Buffered(1))


def _split3(x):
    hi = x.astype(BF16).astype(F32)
    r = x - hi
    mid = r.astype(BF16).astype(F32)
    lo = (r - mid).astype(BF16).astype(F32)
    return hi, mid, lo


def _inproj_kernel(x_ref, g_ref, wqkv_ref, wf_ref, bf_ref, wp_ref, wg_ref, gq_ref, gk_ref,
                   selq_ref, selk_ref, wpool_ref, pscale_ref, uprev_ref, fprev_ref,
                   qta_ref, ka_ref, vt_ref, pooled_ref, gates_ref, utail_ref, ftail_ref,
                   ubuf, fcarry, *, tm, n_valid, first_pos):
    j = pl.program_id(1)

    @pl.when(j == 0)
    def _():
        ubuf[0:MAX_WINDOW, :] = uprev_ref[...]
        fcarry[...] = fprev_ref[...]

    x = x_ref[...]
    ms = jnp.mean(x * x, axis=-1, keepdims=True)
    hn = (x * lax.rsqrt(ms + RMS_EPS) * g_ref[...]).astype(BF16)

    f = jnp.dot(hn, wf_ref[...], preferred_element_type=F32) + bf_ref[...]
    z = f.T[0:N_HEADS, :]
    logf = (jnp.minimum(z, 0.0) - jnp.log1p(jnp.exp(-jnp.abs(z)))) * LOG2_E
    parts = jnp.concatenate(_split3(logf), axis=0).astype(BF16)
    r_i = lax.broadcasted_iota(jnp.int32, (tm, tm), 0)
    c_i = lax.broadcasted_iota(jnp.int32, (tm, tm), 1)
    tri = jnp.where(r_i <= c_i, 1.0, 0.0).astype(BF16)
    cs3 = jnp.dot(parts, tri, preferred_element_type=F32)
    cs = cs3[0:8] + cs3[8:16] + cs3[16:24]
    fc = jnp.concatenate([fcarry[...]] * (tm // LANES), axis=1) + cs
    last = jnp.broadcast_to(fc[:, n_valid - 1:n_valid], (N_HEADS, LANES))
    fcarry[...] = last
    ftail_ref[...] = last

    pieces = jnp.concatenate(
        _split3(fc) + (jnp.ones((8, tm), F32), jnp.zeros((LANES - 32, tm), F32)),
        axis=0).astype(BF16)
    fq = jnp.dot(selq_ref[...], pieces, preferred_element_type=F32)
    fk = jnp.dot(selk_ref[...], pieces, preferred_element_type=F32)

    qkv = jnp.dot(hn, wqkv_ref[...], preferred_element_type=F32)
    qt = qkv[:, 0:ATTN_WIDTH].T
    kt = qkv[:, ATTN_WIDTH:2 * ATTN_WIDTH].T
    vt_ref[...] = qkv[:, 2 * ATTN_WIDTH:3 * ATTN_WIDTH].T.astype(BF16)

    def head_norm(src, gain_ref, h):
        xh = src[h * HEAD_DIM:(h + 1) * HEAD_DIM, :]
        ssq = jnp.mean(xh * xh, axis=0, keepdims=True)
        return xh * lax.rsqrt(ssq + RMS_EPS) * gain_ref[...]

    k_pad = jnp.zeros((LANES - HEAD_ROWS, tm), F32)
    for h in range(N_HEADS):
        aug = slice(h * AUG_ROWS, (h + 1) * AUG_ROWS)
        r0 = h * HEAD_ROWS
        qta_ref[r0:r0 + HEAD_DIM, :] = head_norm(qt, gq_ref, h).astype(BF16)
        qta_ref[r0 + HEAD_DIM:r0 + HEAD_ROWS, :] = fq[aug, :].astype(BF16)
        k_blk = jnp.concatenate([head_norm(kt, gk_ref, h), fk[aug, :], k_pad], axis=0)
        ka_ref[:, h * LANES:(h + 1) * LANES] = k_blk.T.astype(BF16)

    u = jnp.dot(hn, wp_ref[...], preferred_element_type=F32)
    ubuf[MAX_WINDOW:MAX_WINDOW + tm, :] = u
    mixed = []
    for g, w in enumerate(POOL_WINDOWS):
        c0 = g * POOL_GROUP_DIM
        acc = u[:, c0:c0 + POOL_GROUP_DIM]
        for s in range(1, w):
            acc = acc + ubuf[MAX_WINDOW - s:MAX_WINDOW - s + tm, c0:c0 + POOL_GROUP_DIM]
        if first_pos + 1 >= w:
            mean = acc * (1.0 / w)
        else:
            pos = first_pos + j * tm + lax.broadcasted_iota(jnp.int32, (tm, POOL_GROUP_DIM), 0)
            mean = acc / jnp.minimum(pos + 1, w).astype(F32)
        mixed.append(mean - u[:, c0:c0 + POOL_GROUP_DIM])
    mixed = jnp.concatenate(mixed, axis=1).astype(BF16)
    y = jnp.dot(mixed, wpool_ref[...], preferred_element_type=F32) * pscale_ref[...]
    pooled_ref[...] = y.astype(BF16)
    tail = u[n_valid - MAX_WINDOW:n_valid, :]
    ubuf[0:MAX_WINDOW, :] = tail
    utail_ref[...] = tail

    gl = jnp.dot(hn, wg_ref[...], preferred_element_type=F32)
    gates_ref[...] = jax.nn.sigmoid(gl).astype(BF16)


def _inproj(x3, p, uprev, fprev, *, tm, n_valid, first_pos):
    nb, seq, _ = x3.shape
    nt = seq // tm
    gq = jnp.broadcast_to((p["gq"] * (LOG2_E / math.sqrt(HEAD_DIM)))[:, None], (HEAD_DIM, tm))
    gk = jnp.broadcast_to(p["gk"][:, None], (HEAD_DIM, tm))
    tok = lambda width: pl.BlockSpec((None, tm, width), lambda b, j: (b, j, 0))
    chan = lambda rows: pl.BlockSpec((None, rows, tm), lambda b, j: (b, 0, j))
    kern = functools.partial(_inproj_kernel, tm=tm, n_valid=n_valid, first_pos=first_pos)
    return pl.pallas_call(
        kern,
        grid=(nb, nt),
        in_specs=[
            tok(D_MODEL),
            _const_spec((1, D_MODEL)),
            _const_spec((D_MODEL, 3 * ATTN_WIDTH)),
            _const_spec((D_MODEL, LANES)),
            _const_spec((1, LANES)),
            _const_spec((D_MODEL, POOL_WIDTH)),
            _const_spec((D_MODEL, 2 * D_MODEL)),
            _const_spec((HEAD_DIM, tm)),
            _const_spec((HEAD_DIM, tm)),
            _const_spec((LANES, LANES)),
            _const_spec((LANES, LANES)),
            _const_spec((POOL_WIDTH, POOL_WIDTH)),
            _const_spec((1, POOL_WIDTH)),
            _const_spec((MAX_WINDOW, POOL_WIDTH)),
            _const_spec((N_HEADS, LANES)),
        ],
        out_specs=[
            chan(QK_ROWS), tok(N_HEADS * LANES), chan(ATTN_WIDTH),
            tok(POOL_WIDTH), tok(2 * D_MODEL),
            pl.BlockSpec((MAX_WINDOW, POOL_WIDTH), lambda b, j: (0, 0)),
            pl.BlockSpec((N_HEADS, LANES), lambda b, j: (0, 0)),
        ],
        out_shape=[
            jax.ShapeDtypeStruct((nb, QK_ROWS, seq), BF16),
            jax.ShapeDtypeStruct((nb, seq, N_HEADS * LANES), BF16),
            jax.ShapeDtypeStruct((nb, ATTN_WIDTH, seq), BF16),
            jax.ShapeDtypeStruct((nb, seq, POOL_WIDTH), BF16),
            jax.ShapeDtypeStruct((nb, seq, 2 * D_MODEL), BF16),
            jax.ShapeDtypeStruct((MAX_WINDOW, POOL_WIDTH), F32),
            jax.ShapeDtypeStruct((N_HEADS, LANES), F32),
        ],
        scratch_shapes=[
            pltpu.VMEM((MAX_WINDOW + tm, POOL_WIDTH), F32),
            pltpu.VMEM((N_HEADS, LANES), F32),
        ],
        compiler_params=pltpu.CompilerParams(
            dimension_semantics=("arbitrary", "arbitrary"), vmem_limit_bytes=VMEM_LIMIT),
        name="inproj",
    )(x3, p["g_mix"], p["wqkv"], p["wf"], p["bf"], p["wp"], p["wg"], gq, gk,
      p["selq"], p["selk"], p["wpool"], p["pscale"], uprev, fprev)


def _attn_kernel(qta_ref, ka_ref, vt_ref, kam_ref, vtm_ref, bias_ref, ot_ref, m_sc, acc_sc, *, tq):
    qi = pl.program_id(1)
    m_sc[...] = jnp.full(m_sc.shape, -jnp.inf, F32)
    acc_sc[...] = jnp.zeros(acc_sc.shape, F32)

    def rows(h):
        return slice(h * HEAD_ROWS, (h + 1) * HEAD_ROWS)

    def vrows(h):
        return slice(h * HEAD_DIM, (h + 1) * HEAD_DIM)

    def klanes(h):
        return slice(h * LANES, (h + 1) * LANES)

    q_pad = jnp.zeros((LANES - HEAD_ROWS, tq), BF16)

    def scores(h, ka):
        qa = jnp.concatenate([qta_ref[rows(h), :], q_pad], axis=0)
        return jnp.dot(ka, qa, preferred_element_type=F32)

    def absorb(h, s, v):
        m_old = m_sc[h]
        m_new = jnp.maximum(m_old, jnp.max(s, axis=0, keepdims=True))
        p = jnp.exp2(s - m_new).astype(BF16)
        va = jnp.concatenate([v, jnp.ones((AUG_ROWS, v.shape[1]), BF16)], axis=0)
        acc_sc[h] = (jnp.exp2(m_old - m_new) * acc_sc[h]
                     + jnp.dot(va, p, preferred_element_type=F32))
        m_sc[h] = m_new

    def sweep_heads(scores_of, values_of):
        s_next = scores_of(0)
        for h in range(N_HEADS):
            s_cur = s_next
            if h + 1 < N_HEADS:
                s_next = scores_of(h + 1)
            absorb(h, s_cur, values_of(h))

    def kv_body(i, carry):
        s0 = pl.multiple_of(i * tq, tq)
        sweep_heads(lambda h: scores(h, ka_ref[pl.ds(s0, tq), klanes(h)]),
                    lambda h: vt_ref[vrows(h), pl.ds(s0, tq)])
        return carry

    lax.fori_loop(0, qi, kv_body, 0)

    s0 = pl.multiple_of(qi * tq, tq)
    sweep_heads(
        lambda h: scores(h, jnp.concatenate(
            [ka_ref[pl.ds(s0, tq), klanes(h)], kam_ref[:, klanes(h)]], axis=0)) + bias_ref[...],
        lambda h: jnp.concatenate([vt_ref[vrows(h), pl.ds(s0, tq)], vtm_ref[vrows(h), :]], axis=1))
    for h in range(N_HEADS):
        acc = acc_sc[h]
        ot_ref[vrows(h), :] = (acc[0:HEAD_DIM, :] / acc[HEAD_DIM:HEAD_DIM + 1, :]).astype(BF16)


def _attention(qta, ka, vt, ka_m, vt_m, *, tq):
    nb, _, seq = qta.shape
    kw = N_HEADS * LANES
    key = lax.broadcasted_iota(jnp.int32, (tq + LANES, tq), 0)
    qry = lax.broadcasted_iota(jnp.int32, (tq + LANES, tq), 1)
    visible = jnp.where(key < tq, key <= qry, key - tq < N_META)
    bias = jnp.where(visible, 0.0, -jnp.inf).astype(F32)
    return pl.pallas_call(
        functools.partial(_attn_kernel, tq=tq),
        grid=(nb, seq // tq),
        in_specs=[
            pl.BlockSpec((None, QK_ROWS, tq), lambda b, i: (b, 0, i)),
            pl.BlockSpec((None, seq, kw), lambda b, i: (b, 0, 0)),
            pl.BlockSpec((None, ATTN_WIDTH, seq), lambda b, i: (b, 0, 0)),
            pl.BlockSpec((None, LANES, kw), lambda b, i: (0, 0, 0)),
            pl.BlockSpec((None, ATTN_WIDTH, LANES), lambda b, i: (0, 0, 0)),
            _const_spec((tq + LANES, tq)),
        ],
        out_specs=pl.BlockSpec((None, ATTN_WIDTH, tq), lambda b, i: (b, 0, i)),
        out_shape=jax.ShapeDtypeStruct((nb, ATTN_WIDTH, seq), BF16),
        scratch_shapes=[
            pltpu.VMEM((N_HEADS, 1, tq), F32),
            pltpu.VMEM((N_HEADS, HEAD_ROWS, tq), F32),
        ],
        compiler_params=pltpu.CompilerParams(
            dimension_semantics=("arbitrary", "arbitrary"), vmem_limit_bytes=VMEM_LIMIT),
        name="fox_attention",
    )(qta, ka, vt, ka_m, vt_m, bias)


def _post_kernel(ot_ref, pooled_ref, gates_ref, x_ref, wua_ref, wup_ref, wout_ref, g_ref,
                 wr_ref, br_ref, h2_ref, info_ref, cnt_ref, cnt_sc, *, tm):
    @pl.when((pl.program_id(0) == 0) & (pl.program_id(1) == 0))
    def _():
        cnt_sc[...] = jnp.zeros(cnt_sc.shape, F32)

    y_attn = lax.dot_general(ot_ref[...], wua_ref[...], _TN, preferred_element_type=F32)
    y_pool = jnp.dot(pooled_ref[...], wup_ref[...], preferred_element_type=F32)
    merged = (gates_ref[:, 0:D_MODEL].astype(F32) * y_attn
              + gates_ref[:, D_MODEL:2 * D_MODEL].astype(F32) * y_pool)
    h2 = x_ref[...] + jnp.dot(merged.astype(BF16), wout_ref[...], preferred_element_type=F32)
    h2_ref[...] = h2
    ms = jnp.mean(h2 * h2, axis=-1, keepdims=True)
    hn2 = (h2 * lax.rsqrt(ms + RMS_EPS) * g_ref[...]).astype(BF16)

    logits = jnp.dot(hn2, wr_ref[...], preferred_element_type=F32) + br_ref[...]
    lane = lax.broadcasted_iota(jnp.int32, logits.shape, 1)
    neg_inf = jnp.float32(-jnp.inf)

    def softmax_over(mask):
        z = jnp.where(mask, logits, neg_inf)
        e = jnp.exp(z - jnp.max(z, axis=1, keepdims=True))
        return jnp.where(mask, e / jnp.sum(e, axis=1, keepdims=True), -1.0)

    def top1(prob):
        top = jnp.max(prob, axis=1, keepdims=True)
        idx = jnp.min(jnp.where(prob == top, lane, LANES), axis=1, keepdims=True)
        return top, idx

    g_top, g_idx = top1(softmax_over(lane < N_GROUPS))
    e_lo = ROUTER_LANE0 + N_PER_GROUP * g_idx
    e_prob = softmax_over((lane >= e_lo) & (lane < e_lo + N_PER_GROUP))
    p1, i1 = top1(e_prob)
    p2, i2 = top1(jnp.where(lane == i1, -1.0, e_prob))
    denom = p1 + p2
    w1 = p1 / denom * g_top
    w2 = p2 / denom * g_top

    first_is_lo = i1 < i2
    a = jnp.minimum(i1, i2) - e_lo
    b = jnp.maximum(i1, i2) - e_lo
    w_lo = jnp.where(first_is_lo, w1, w2)
    w_hi = jnp.where(first_is_lo, w2, w1)
    pair = ((a * (2 * N_PER_GROUP - 1 - a)) >> 1) + (b - a - 1)
    cls = g_idx * PAIRS_PER_GROUP + pair

    member = lane == cls
    onehot = jnp.where(member, 1.0, 0.0)
    r_i = lax.broadcasted_iota(jnp.int32, (tm, tm), 0)
    c_i = lax.broadcasted_iota(jnp.int32, (tm, tm), 1)
    earlier = jnp.where(c_i < r_i, 1.0, 0.0).astype(BF16)
    prefix = jnp.dot(earlier, onehot.astype(BF16), preferred_element_type=F32)
    base = cnt_sc[0:1, :]
    rank = jnp.sum(jnp.where(member, prefix + base, 0.0), axis=1, keepdims=True)
    total = jnp.broadcast_to(base + jnp.sum(onehot, axis=0, keepdims=True), cnt_sc.shape)
    cnt_sc[...] = total
    cnt_ref[...] = total

    info_ref[...] = jnp.where(lane == INFO_CLASS, cls.astype(F32),
                              jnp.where(lane == INFO_RANK, rank,
                                        jnp.where(lane == INFO_W_LO, w_lo,
                                                  jnp.where(lane == INFO_W_HI, w_hi, 0.0))))


def _post(ot, pooled, gates, x, p, *, tm):
    nb, seq, _ = x.shape
    tok = lambda width: pl.BlockSpec((None, tm, width), lambda b, j: (b, j, 0))
    return pl.pallas_call(
        functools.partial(_post_kernel, tm=tm),
        grid=(nb, seq // tm),
        in_specs=[
            pl.BlockSpec((None, ATTN_WIDTH, tm), lambda b, j: (b, 0, j)),
            tok(POOL_WIDTH), tok(2 * D_MODEL), tok(D_MODEL),
            _const_spec((ATTN_WIDTH, D_MODEL)),
            _const_spec((POOL_WIDTH, D_MODEL)),
            _const_spec((D_MODEL, D_MODEL)),
            _const_spec((1, D_MODEL)),
            _const_spec((D_MODEL, LANES)),
            _const_spec((1, LANES)),
        ],
        out_specs=[tok(D_MODEL), tok(LANES), pl.BlockSpec((8, LANES), lambda b, j: (0, 0))],
        out_shape=[
            jax.ShapeDtypeStruct((nb, seq, D_MODEL), F32),
            jax.ShapeDtypeStruct((nb, seq, LANES), F32),
            jax.ShapeDtypeStruct((8, LANES), F32),
        ],
        scratch_shapes=[pltpu.VMEM((8, LANES), F32)],
        compiler_params=pltpu.CompilerParams(
            dimension_semantics=("arbitrary", "arbitrary"), vmem_limit_bytes=VMEM_LIMIT),
        name="post_router",
    )(ot, pooled, gates, x, p["wua"], p["wup"], p["wout"], p["g_ffn"], p["wr"], p["br"])


def _to_slots(dst, slot, values, first_row, stride):
    n = values.shape[0]
    for c in range(values.shape[1] // LANES):
        dst[slot, pl.ds(first_row + c, n, stride=stride), :] = values[:, c * LANES:(c + 1) * LANES]


def _from_slots(src, n, stride):
    return jnp.concatenate(
        [src[pl.ds(c, n, stride=stride), :] for c in range(D_MODEL // LANES)], axis=1)


def _dispatch_kernel(pos_ref, h2_ref, info_ref, xs_hbm, buf, sem, *, tm):
    j = pl.program_id(0)
    slot = j % 2

    def drain(s):
        pltpu.make_async_copy(buf.at[s], xs_hbm.at[pl.ds(0, tm * IN_SLOT_ROWS), :],
                              sem.at[s]).wait()

    @pl.when(j < 2)
    def _():
        buf[slot] = jnp.zeros(buf.shape[1:], F32)

    @pl.when(j >= 2)
    def _():
        drain(slot)

    _to_slots(buf, slot, h2_ref[...], 0, IN_SLOT_ROWS)
    _to_slots(buf, slot, info_ref[...], SLOT_ROWS, IN_SLOT_ROWS)

    def body(g, c):
        for u in range(DMA_UNROLL):
            r = g * DMA_UNROLL + u
            src = pl.multiple_of(r * IN_SLOT_ROWS, IN_SLOT_ROWS)
            dst = pl.multiple_of(pos_ref[0, r] * IN_SLOT_ROWS, IN_SLOT_ROWS)
            pltpu.make_async_copy(buf.at[slot, pl.ds(src, IN_SLOT_ROWS), :],
                                  xs_hbm.at[pl.ds(dst, IN_SLOT_ROWS), :], sem.at[slot]).start()
        return c
    lax.fori_loop(0, tm // DMA_UNROLL, body, 0)

    @pl.when(j == pl.num_programs(0) - 1)
    def _():
        drain(slot)
        drain(1 - slot)


def _dispatch(h2, info, pos, *, tm):
    t = h2.shape[0]
    nt = t // tm
    return pl.pallas_call(
        functools.partial(_dispatch_kernel, tm=tm),
        grid=(nt,),
        in_specs=[
            pl.BlockSpec((None, 1, tm), lambda j: (j, 0, 0), memory_space=pltpu.SMEM),
            pl.BlockSpec((tm, D_MODEL), lambda j: (j, 0)),
            pl.BlockSpec((tm, LANES), lambda j: (j, 0)),
        ],
        out_specs=pl.BlockSpec(memory_space=pl.ANY),
        out_shape=jax.ShapeDtypeStruct((t * IN_SLOT_ROWS, LANES), F32),
        scratch_shapes=[
            pltpu.VMEM((2, tm * IN_SLOT_ROWS, LANES), F32),
            pltpu.SemaphoreType.DMA((2,)),
        ],
        compiler_params=pltpu.CompilerParams(
            dimension_semantics=("arbitrary",), vmem_limit_bytes=VMEM_LIMIT),
        name="moe_dispatch",
    )(pos.reshape(nt, 1, tm), h2, info)


def _moe_kernel(tile_ref, ea_ref, eb_ref, lo_ref, hi_ref, nitems_ref,
                xs_ref, g_ref, w13a_ref, w13b_ref, w2a_ref, w2b_ref, ys_ref, acc):
    del ea_ref, eb_ref
    k = pl.program_id(0)

    @pl.when(k < nitems_ref[0])
    def _():
        h2 = _from_slots(xs_ref, MOE_TILE, IN_SLOT_ROWS)
        info = xs_ref[pl.ds(SLOT_ROWS, MOE_TILE, stride=IN_SLOT_ROWS), :]
        row = lax.broadcasted_iota(jnp.int32, (MOE_TILE, 1), 0)
        in_segment = (row >= lo_ref[k]) & (row < hi_ref[k])
        w_lo = jnp.where(in_segment, info[:, INFO_W_LO:INFO_W_LO + 1], 0.0)
        w_hi = jnp.where(in_segment, info[:, INFO_W_HI:INFO_W_HI + 1], 0.0)
        ms = jnp.mean(h2 * h2, axis=-1, keepdims=True)
        xn = (h2 * lax.rsqrt(ms + RMS_EPS) * g_ref[...]).astype(BF16)

        def expert(w13_ref, w2_ref, weight):
            h = jnp.dot(xn, w13_ref[...], preferred_element_type=F32)
            h1 = h[:, 0:EXPERT_HIDDEN]
            h3 = h[:, EXPERT_HIDDEN:2 * EXPERT_HIDDEN]
            hh = h1 * jax.nn.sigmoid(h1) * h3 * weight
            return jnp.dot(hh.astype(BF16), w2_ref[...], preferred_element_type=F32)

        y = expert(w13a_ref, w2a_ref, w_lo) + expert(w13b_ref, w2b_ref, w_hi)
        new_tile = (k == 0) | (tile_ref[k] != tile_ref[jnp.maximum(k - 1, 0)])

        @pl.when(new_tile)
        def _():
            acc[...] = h2 + y

        @pl.when(jnp.logical_not(new_tile))
        def _():
            acc[...] += y

        total = acc[...]
        for c in range(D_MODEL // LANES):
            ys_ref[pl.ds(c, MOE_TILE, stride=SLOT_ROWS), :] = total[:, c * LANES:(c + 1) * LANES]


def _combine_kernel(pos_ref, pos_next_ref, ys_hbm, out_ref, buf, sem, *, tm):
    j = pl.program_id(0)
    slot = j % 2

    def start_tile(pos, s):
        def body(g, c):
            for u in range(DMA_UNROLL):
                r = g * DMA_UNROLL + u
                src = pl.multiple_of(pos[0, r] * SLOT_ROWS, SLOT_ROWS)
                dst = pl.multiple_of(r * SLOT_ROWS, SLOT_ROWS)
                pltpu.make_async_copy(ys_hbm.at[pl.ds(src, SLOT_ROWS), :],
                                      buf.at[s, pl.ds(dst, SLOT_ROWS), :], sem.at[s]).start()
            return c
        lax.fori_loop(0, tm // DMA_UNROLL, body, 0)

    @pl.when(j == 0)
    def _():
        start_tile(pos_ref, 0)

    @pl.when(j + 1 < pl.num_programs(0))
    def _():
        start_tile(pos_next_ref, 1 - slot)

    pltpu.make_async_copy(ys_hbm.at[pl.ds(0, tm * SLOT_ROWS), :], buf.at[slot], sem.at[slot]).wait()
    out_ref[...] = _from_slots(buf.at[slot], tm, SLOT_ROWS)


def _combine(ys, pos, *, tm):
    t = pos.shape[0]
    nt = t // tm
    pos3 = pos.reshape(nt, 1, tm)
    smem_tile = lambda fn: pl.BlockSpec((None, 1, tm), fn, memory_space=pltpu.SMEM)
    return pl.pallas_call(
        functools.partial(_combine_kernel, tm=tm),
        grid=(nt,),
        in_specs=[
            smem_tile(lambda j: (j, 0, 0)),
            smem_tile(lambda j: (jnp.minimum(j + 1, nt - 1), 0, 0)),
            pl.BlockSpec(memory_space=pl.ANY),
        ],
        out_specs=pl.BlockSpec((tm, D_MODEL), lambda j: (j, 0)),
        out_shape=jax.ShapeDtypeStruct((t, D_MODEL), F32),
        scratch_shapes=[
            pltpu.VMEM((2, tm * SLOT_ROWS, LANES), F32),
            pltpu.SemaphoreType.DMA((2,)),
        ],
        compiler_params=pltpu.CompilerParams(
            dimension_semantics=("arbitrary",), vmem_limit_bytes=VMEM_LIMIT),
        name="moe_combine",
    )(pos3, pos3, ys)


def _class_experts():
    ea, eb = [], []
    for g in range(N_GROUPS):
        for a in range(N_PER_GROUP):
            for b in range(a + 1, N_PER_GROUP):
                ea.append(g * N_PER_GROUP + a)
                eb.append(g * N_PER_GROUP + b)
    fill = LANES - len(ea)
    return (jnp.array(ea + [ea[-1]] * fill, jnp.int32),
            jnp.array(eb + [eb[-1]] * fill, jnp.int32))


def _moe(h2, info, counts, p, *, tm):
    t = h2.shape[0]
    assert t % tm == 0 and t // tm >= 2 and t % MOE_TILE == 0
    n_tiles = t // MOE_TILE
    max_items = n_tiles + N_CLASSES
    class_ids = jnp.arange(LANES, dtype=jnp.int32)
    c_end = jnp.cumsum(counts)
    c_start = c_end - counts
    cls = info[:, INFO_CLASS].astype(jnp.int32)
    rank = info[:, INFO_RANK].astype(jnp.int32)
    pos = jnp.sum(jnp.where(cls[:, None] == class_ids[None, :], c_start[None, :], 0),
                  axis=1) + rank
    first_tile = c_start // MOE_TILE
    n_items_c = jnp.where(counts > 0, (c_end - 1) // MOE_TILE - first_tile + 1, 0)
    item_end = jnp.cumsum(n_items_c)
    item_start = item_end - n_items_c
    n_items = item_end[-1]
    item_ids = jnp.minimum(jnp.arange(max_items, dtype=jnp.int32), n_items - 1)
    icls = jnp.sum(item_end[None, :] <= item_ids[:, None], axis=1, dtype=jnp.int32)
    pick = lambda table: jnp.sum(
        jnp.where(icls[:, None] == class_ids[None, :], table[None, :], 0), axis=1)
    tile = pick(first_tile) + item_ids - pick(item_start)
    lo = jnp.maximum(pick(c_start) - tile * MOE_TILE, 0)
    hi = jnp.minimum(pick(c_end) - tile * MOE_TILE, MOE_TILE)
    ea_tab, eb_tab = _class_experts()
    ea, eb = pick(ea_tab), pick(eb_tab)

    xs = _dispatch(h2, info, pos, tm=tm)

    first = lambda ea, eb: ea
    second = lambda ea, eb: eb
    w13 = lambda sel: pl.BlockSpec((None, D_MODEL, 2 * EXPERT_HIDDEN),
                                   lambda k, tl, ea, eb, lo, hi, n: (sel(ea, eb)[k], 0, 0))
    w2 = lambda sel: pl.BlockSpec((None, EXPERT_HIDDEN, D_MODEL),
                                  lambda k, tl, ea, eb, lo, hi, n: (sel(ea, eb)[k], 0, 0))
    ys = pl.pallas_call(
        _moe_kernel,
        grid_spec=pltpu.PrefetchScalarGridSpec(
            num_scalar_prefetch=6,
            grid=(max_items,),
            in_specs=[
                pl.BlockSpec((MOE_TILE * IN_SLOT_ROWS, LANES), lambda k, tl, *_: (tl[k], 0)),
                pl.BlockSpec((1, D_MODEL), lambda k, *_: (0, 0)),
                w13(first), w13(second), w2(first), w2(second),
            ],
            out_specs=pl.BlockSpec((MOE_TILE * SLOT_ROWS, LANES), lambda k, tl, *_: (tl[k], 0)),
            scratch_shapes=[pltpu.VMEM((MOE_TILE, D_MODEL), F32)],
        ),
        out_shape=jax.ShapeDtypeStruct((t * SLOT_ROWS, LANES), F32),
        compiler_params=pltpu.CompilerParams(
            dimension_semantics=("arbitrary",), vmem_limit_bytes=VMEM_LIMIT),
        name="moe",
    )(tile, ea, eb, lo, hi, n_items.reshape(1), xs, p["g_ffn"],
      p["w13"], p["w13"], p["w2"], p["w2"])
    return _combine(ys, pos, tm=tm)


def _bias_selectors():
    selq = [[0.0] * LANES for _ in range(LANES)]
    selk = [[0.0] * LANES for _ in range(LANES)]
    ones_row = 3 * N_HEADS
    for h in range(N_HEADS):
        for part in range(3):
            selq[h * AUG_ROWS + part][part * N_HEADS + h] = 1.0
            selq[h * AUG_ROWS + 3 + part][ones_row] = 1.0
            selk[h * AUG_ROWS + part][ones_row] = 1.0
            selk[h * AUG_ROWS + 3 + part][part * N_HEADS + h] = -1.0
    return jnp.array(selq, BF16), jnp.array(selk, BF16)


def _prepare_params(norm_mix_g, w_in, b_forget, q_norm_g, k_norm_g, w_up_attn, w_pool,
                    pool_scale, w_up_pool, w_out, norm_ffn_g, w_group, b_group, w_router,
                    b_router, w1, w3, w2):
    aw, pw = ATTN_WIDTH, POOL_WIDTH
    f_off = 3 * aw
    p_off = f_off + N_HEADS
    g_off = p_off + pw
    selq, selk = _bias_selectors()
    pad_lanes = lambda a: jnp.pad(a, ((0, 0), (0, LANES - a.shape[1])))
    wpool = jnp.zeros((pw, pw), F32)
    for g in range(len(POOL_WINDOWS)):
        sl = slice(g * POOL_GROUP_DIM, (g + 1) * POOL_GROUP_DIM)
        wpool = wpool.at[sl, sl].set(w_pool[g])
    d, f = D_MODEL, EXPERT_HIDDEN
    return {
        "g_mix": norm_mix_g.reshape(1, d),
        "wqkv": w_in[:, 0:f_off].astype(BF16),
        "wf": pad_lanes(w_in[:, f_off:p_off]).astype(BF16),
        "bf": pad_lanes(b_forget.reshape(1, N_HEADS)),
        "wp": w_in[:, p_off:g_off].astype(BF16),
        "wg": w_in[:, g_off:].astype(BF16),
        "gq": q_norm_g, "gk": k_norm_g,
        "selq": selq, "selk": selk,
        "wpool": wpool.astype(BF16),
        "pscale": pool_scale.reshape(1, pw),
        "wua": w_up_attn.astype(BF16),
        "wup": w_up_pool.astype(BF16),
        "wout": w_out.astype(BF16),
        "g_ffn": norm_ffn_g.reshape(1, d),
        "wr": pad_lanes(jnp.concatenate([w_group, w_router], axis=1)).astype(BF16),
        "br": pad_lanes(jnp.concatenate([b_group, b_router]).reshape(1, -1)),
        "w13": jnp.concatenate([w1, w3], axis=-1).reshape(N_EXPERTS, d, 2 * f).astype(BF16),
        "w2": w2.reshape(N_EXPERTS, f, d).astype(BF16),
    }


def kernel(x, meta_tokens, norm_mix_g, w_in, b_forget, q_norm_g, k_norm_g, w_up_attn, w_pool,
           pool_scale, w_up_pool, w_out, norm_ffn_g, w_group, b_group, w_router, b_router,
           w1, w3, w2):
    nb, seq, d = x.shape
    p = _prepare_params(norm_mix_g[0], w_in[0], b_forget[0], q_norm_g[0], k_norm_g[0],
                        w_up_attn[0], w_pool[0], pool_scale[0], w_up_pool[0], w_out[0],
                        norm_ffn_g[0], w_group[0], b_group[0], w_router[0], b_router[0],
                        w1[0], w3[0], w2[0])

    meta = jnp.pad(meta_tokens.astype(x.dtype), ((0, LANES - N_META), (0, 0)))[None]
    _, ka_m, vt_m, _, _, u_meta, f_meta = _inproj(
        meta, p, jnp.zeros((MAX_WINDOW, POOL_WIDTH), F32), jnp.zeros((N_HEADS, LANES), F32),
        tm=LANES, n_valid=N_META, first_pos=0)

    tm = 512
    qta, ka, vt, pooled, gates, _, _ = _inproj(
        x, p, u_meta, f_meta, tm=tm, n_valid=tm, first_pos=N_META)
    ot = _attention(qta, ka, vt, ka_m, vt_m, tq=512)
    h2, info, cnt = _post(ot, pooled, gates, x, p, tm=512)
    t = nb * seq
    out = _moe(h2.reshape(t, d), info.reshape(t, LANES), cnt[0].astype(jnp.int32), p, tm=512)
    return out.reshape(nb, seq, d)
```

```python
import functools
import math

import jax
import jax.numpy as jnp
import numpy as np
from jax import lax
from jax.experimental import pallas as pl
from jax.experimental.pallas import tpu as pltpu

F32 = jnp.float32
BF16 = jnp.bfloat16

D_MODEL = 1024
N_META = 16
N_HEADS = 8
HEAD_DIM = 64
ATTN_WIDTH = N_HEADS * HEAD_DIM
POOL_WINDOWS = (2, 4, 8, 16)
POOL_WIDTH = 512
POOL_GROUP_DIM = POOL_WIDTH // len(POOL_WINDOWS)
MAX_WINDOW = max(POOL_WINDOWS)
N_GROUPS = 4
N_PER_GROUP = 8
N_EXPERTS = N_GROUPS * N_PER_GROUP
EXPERT_HIDDEN = 256
RMS_EPS = 1e-6
LOG2_E = math.log2(math.e)

LANES = 128
AUG_ROWS = 16
HEAD_ROWS = HEAD_DIM + AUG_ROWS
QK_ROWS = N_HEADS * HEAD_ROWS
ROUTER_LANE0 = N_GROUPS
PAIRS_PER_GROUP = N_PER_GROUP * (N_PER_GROUP - 1) // 2
N_CLASSES = N_GROUPS * PAIRS_PER_GROUP
INFO_CLASS, INFO_RANK, INFO_W_LO, INFO_W_HI = 0, 1, 2, 3
MOE_TILE = 256
SLOT_ROWS = D_MODEL // LANES
IN_SLOT_ROWS = 2 * SLOT_ROWS
DMA_UNROLL = 8
POST_CHAINS = 1
VMEM_LIMIT = 56 * 1024 * 1024

_TN = (((0,), (0,)), ((), ()))


def _const_spec(shape):
    zeros = (0,) * len(shape)
    return pl.BlockSpec(shape, lambda *_: zeros, pipeline_mode=pl.Buffered(1))


def _split3(x):
    hi = x.astype(BF16).astype(F32)
    r = x - hi
    mid = r.astype(BF16).astype(F32)
    lo = (r - mid).astype(BF16).astype(F32)
    return hi, mid, lo


def _inproj_kernel(x_ref, g_ref, wqkv_ref, wf_ref, bf_ref, wp_ref, wg_ref, gq_ref, gk_ref,
                   selq_ref, selk_ref, wpool_ref, pscale_ref, uprev_ref, fprev_ref,
                   qta_ref, ka_ref, vt_ref, pooled_ref, gates_ref, utail_ref, ftail_ref,
                   ubuf, fcarry, *, tm, n_valid, first_pos):
    j = pl.program_id(1)

    @pl.when(j == 0)
    def _():
        ubuf[0:MAX_WINDOW, :] = uprev_ref[...]
        fcarry[...] = fprev_ref[...]

    x = x_ref[...]
    ms = jnp.mean(x * x, axis=-1, keepdims=True)
    hn = (x * lax.rsqrt(ms + RMS_EPS) * g_ref[...]).astype(BF16)

    f = jnp.dot(hn, wf_ref[...], preferred_element_type=F32) + bf_ref[...]
    z = f.T[0:N_HEADS, :]
    logf = (jnp.minimum(z, 0.0) - jnp.log1p(jnp.exp(-jnp.abs(z)))) * LOG2_E
    parts = jnp.concatenate(_split3(logf), axis=0).astype(BF16)
    r_i = lax.broadcasted_iota(jnp.int32, (tm, tm), 0)
    c_i = lax.broadcasted_iota(jnp.int32, (tm, tm), 1)
    tri = jnp.where(r_i <= c_i, 1.0, 0.0).astype(BF16)
    cs3 = jnp.dot(parts, tri, preferred_element_type=F32)
    cs = cs3[0:8] + cs3[8:16] + cs3[16:24]
    fc = jnp.concatenate([fcarry[...]] * (tm // LANES), axis=1) + cs
    last = jnp.broadcast_to(fc[:, n_valid - 1:n_valid], (N_HEADS, LANES))
    fcarry[...] = last
    ftail_ref[...] = last

    pieces = jnp.concatenate(
        _split3(fc) + (jnp.ones((8, tm), F32), jnp.zeros((LANES - 32, tm), F32)),
        axis=0).astype(BF16)
    fq = jnp.dot(selq_ref[...], pieces, preferred_element_type=F32)
    fk = jnp.dot(selk_ref[...], pieces, preferred_element_type=F32)

    qkv = jnp.dot(hn, wqkv_ref[...], preferred_element_type=F32)
    qt = qkv[:, 0:ATTN_WIDTH].T
    kt = qkv[:, ATTN_WIDTH:2 * ATTN_WIDTH].T
    vt_ref[...] = qkv[:, 2 * ATTN_WIDTH:3 * ATTN_WIDTH].T.astype(BF16)

    def head_norm(src, gain_ref, h):
        xh = src[h * HEAD_DIM:(h + 1) * HEAD_DIM, :]
        ssq = jnp.mean(xh * xh, axis=0, keepdims=True)
        return xh * lax.rsqrt(ssq + RMS_EPS) * gain_ref[...]

    k_pad = jnp.zeros((LANES - HEAD_ROWS, tm), F32)
    for h in range(N_HEADS):
        aug = slice(h * AUG_ROWS, (h + 1) * AUG_ROWS)
        r0 = h * HEAD_ROWS
        qta_ref[r0:r0 + HEAD_DIM, :] = head_norm(qt, gq_ref, h).astype(BF16)
        qta_ref[r0 + HEAD_DIM:r0 + HEAD_ROWS, :] = fq[aug, :].astype(BF16)
        k_blk = jnp.concatenate([head_norm(kt, gk_ref, h), fk[aug, :], k_pad], axis=0)
        ka_ref[:, h * LANES:(h + 1) * LANES] = k_blk.T.astype(BF16)

    u = jnp.dot(hn, wp_ref[...], preferred_element_type=F32)
    ubuf[MAX_WINDOW:MAX_WINDOW + tm, :] = u
    mixed = []
    for g, w in enumerate(POOL_WINDOWS):
        c0 = g * POOL_GROUP_DIM
        acc = u[:, c0:c0 + POOL_GROUP_DIM]
        for s in range(1, w):
            acc = acc + ubuf[MAX_WINDOW - s:MAX_WINDOW - s + tm, c0:c0 + POOL_GROUP_DIM]
        if first_pos + 1 >= w:
            mean = acc * (1.0 / w)
        else:
            pos = first_pos + j * tm + lax.broadcasted_iota(jnp.int32, (tm, POOL_GROUP_DIM), 0)
            mean = acc / jnp.minimum(pos + 1, w).astype(F32)
        mixed.append(mean - u[:, c0:c0 + POOL_GROUP_DIM])
    mixed = jnp.concatenate(mixed, axis=1).astype(BF16)
    y = jnp.dot(mixed, wpool_ref[...], preferred_element_type=F32) * pscale_ref[...]
    pooled_ref[...] = y.astype(BF16)
    tail = u[n_valid - MAX_WINDOW:n_valid, :]
    ubuf[0:MAX_WINDOW, :] = tail
    utail_ref[...] = tail

    gl = jnp.dot(hn, wg_ref[...], preferred_element_type=F32)
    gates_ref[...] = jax.nn.sigmoid(gl).astype(BF16)


def _inproj(x3, p, uprev, fprev, *, tm, n_valid, first_pos):
    nb, seq, _ = x3.shape
    nt = seq // tm
    gq = jnp.broadcast_to((p["gq"] * (LOG2_E / math.sqrt(HEAD_DIM)))[:, None], (HEAD_DIM, tm))
    gk = jnp.broadcast_to(p["gk"][:, None], (HEAD_DIM, tm))
    tok = lambda width: pl.BlockSpec((None, tm, width), lambda b, j: (b, j, 0))
    chan = lambda rows: pl.BlockSpec((None, rows, tm), lambda b, j: (b, 0, j))
    kern = functools.partial(_inproj_kernel, tm=tm, n_valid=n_valid, first_pos=first_pos)
    return pl.pallas_call(
        kern,
        grid=(nb, nt),
        in_specs=[
            tok(D_MODEL),
            _const_spec((1, D_MODEL)),
            _const_spec((D_MODEL, 3 * ATTN_WIDTH)),
            _const_spec((D_MODEL, LANES)),
            _const_spec((1, LANES)),
            _const_spec((D_MODEL, POOL_WIDTH)),
            _const_spec((D_MODEL, 2 * D_MODEL)),
            _const_spec((HEAD_DIM, tm)),
            _const_spec((HEAD_DIM, tm)),
            _const_spec((LANES, LANES)),
            _const_spec((LANES, LANES)),
            _const_spec((POOL_WIDTH, POOL_WIDTH)),
            _const_spec((1, POOL_WIDTH)),
            _const_spec((MAX_WINDOW, POOL_WIDTH)),
            _const_spec((N_HEADS, LANES)),
        ],
        out_specs=[
            chan(QK_ROWS), tok(N_HEADS * LANES), chan(ATTN_WIDTH),
            tok(POOL_WIDTH), tok(2 * D_MODEL),
            pl.BlockSpec((MAX_WINDOW, POOL_WIDTH), lambda b, j: (0, 0)),
            pl.BlockSpec((N_HEADS, LANES), lambda b, j: (0, 0)),
        ],
        out_shape=[
            jax.ShapeDtypeStruct((nb, QK_ROWS, seq), BF16),
            jax.ShapeDtypeStruct((nb, seq, N_HEADS * LANES), BF16),
            jax.ShapeDtypeStruct((nb, ATTN_WIDTH, seq), BF16),
            jax.ShapeDtypeStruct((nb, seq, POOL_WIDTH), BF16),
            jax.ShapeDtypeStruct((nb, seq, 2 * D_MODEL), BF16),
            jax.ShapeDtypeStruct((MAX_WINDOW, POOL_WIDTH), F32),
            jax.ShapeDtypeStruct((N_HEADS, LANES), F32),
        ],
        scratch_shapes=[
            pltpu.VMEM((MAX_WINDOW + tm, POOL_WIDTH), F32),
            pltpu.VMEM((N_HEADS, LANES), F32),
        ],
        compiler_params=pltpu.CompilerParams(
            dimension_semantics=("arbitrary", "arbitrary"), vmem_limit_bytes=VMEM_LIMIT),
        name="inproj",
    )(x3, p["g_mix"], p["wqkv"], p["wf"], p["bf"], p["wp"], p["wg"], gq, gk,
      p["selq"], p["selk"], p["wpool"], p["pscale"], uprev, fprev)


def _attn_kernel(qta_ref, ka_ref, vt_ref, kam_ref, vtm_ref, bias_ref, ot_ref, m_sc, acc_sc, *, tq):
    qi = pl.program_id(1)
    m_sc[...] = jnp.full(m_sc.shape, -jnp.inf, F32)
    acc_sc[...] = jnp.zeros(acc_sc.shape, F32)

    def rows(h):
        return slice(h * HEAD_ROWS, (h + 1) * HEAD_ROWS)

    def vrows(h):
        return slice(h * HEAD_DIM, (h + 1) * HEAD_DIM)

    def klanes(h):
        return slice(h * LANES, (h + 1) * LANES)

    q_pad = jnp.zeros((LANES - HEAD_ROWS, tq), BF16)

    def scores(h, ka):
        qa = jnp.concatenate([qta_ref[rows(h), :], q_pad], axis=0)
        return jnp.dot(ka, qa, preferred_element_type=F32)

    def new_max(h, s):
        return jnp.maximum(m_sc[h], jnp.max(s, axis=0, keepdims=True))

    def absorb(h, s, m_new, v):
        p = jnp.exp2(s - m_new).astype(BF16)
        va = jnp.concatenate([v, jnp.ones((AUG_ROWS, v.shape[1]), BF16)], axis=0)
        acc_sc[h] = (jnp.exp2(m_sc[h] - m_new) * acc_sc[h]
                     + jnp.dot(va, p, preferred_element_type=F32))
        m_sc[h] = m_new

    def sweep_heads(scores_of, values_of):
        s = {0: scores_of(0), 1: scores_of(1)}
        m = {0: new_max(0, s[0])}
        for h in range(N_HEADS):
            if h + 2 < N_HEADS:
                s[h + 2] = scores_of(h + 2)
            if h + 1 < N_HEADS:
                m[h + 1] = new_max(h + 1, s[h + 1])
            absorb(h, s.pop(h), m.pop(h), values_of(h))

    def kv_body(i, carry):
        s0 = pl.multiple_of(i * tq, tq)
        sweep_heads(lambda h: scores(h, ka_ref[pl.ds(s0, tq), klanes(h)]),
                    lambda h: vt_ref[vrows(h), pl.ds(s0, tq)])
        return carry

    lax.fori_loop(0, qi, kv_body, 0)

    s0 = pl.multiple_of(qi * tq, tq)
    sweep_heads(
        lambda h: scores(h, jnp.concatenate(
            [ka_ref[pl.ds(s0, tq), klanes(h)], kam_ref[:, klanes(h)]], axis=0)) + bias_ref[...],
        lambda h: jnp.concatenate([vt_ref[vrows(h), pl.ds(s0, tq)], vtm_ref[vrows(h), :]], axis=1))
    for h in range(N_HEADS):
        acc = acc_sc[h]
        ot_ref[vrows(h), :] = (acc[0:HEAD_DIM, :] / acc[HEAD_DIM:HEAD_DIM + 1, :]).astype(BF16)


def _attention(qta, ka, vt, ka_m, vt_m, *, tq):
    nb, _, seq = qta.shape
    kw = N_HEADS * LANES
    key = np.arange(tq + LANES)[:, None]
    qry = np.arange(tq)[None, :]
    visible = np.where(key < tq, key <= qry, key - tq < N_META)
    bias = jnp.asarray(np.where(visible, 0.0, -np.inf), F32)
    return pl.pallas_call(
        functools.partial(_attn_kernel, tq=tq),
        grid=(nb, seq // tq),
        in_specs=[
            pl.BlockSpec((None, QK_ROWS, tq), lambda b, i: (b, 0, i)),
            pl.BlockSpec((None, seq, kw), lambda b, i: (b, 0, 0)),
            pl.BlockSpec((None, ATTN_WIDTH, seq), lambda b, i: (b, 0, 0)),
            pl.BlockSpec((None, LANES, kw), lambda b, i: (0, 0, 0)),
            pl.BlockSpec((None, ATTN_WIDTH, LANES), lambda b, i: (0, 0, 0)),
            _const_spec((tq + LANES, tq)),
        ],
        out_specs=pl.BlockSpec((None, ATTN_WIDTH, tq), lambda b, i: (b, 0, i)),
        out_shape=jax.ShapeDtypeStruct((nb, ATTN_WIDTH, seq), BF16),
        scratch_shapes=[
            pltpu.VMEM((N_HEADS, 1, tq), F32),
            pltpu.VMEM((N_HEADS, HEAD_ROWS, tq), F32),
        ],
        compiler_params=pltpu.CompilerParams(
            dimension_semantics=("arbitrary", "arbitrary"), vmem_limit_bytes=VMEM_LIMIT),
        name="fox_attention",
    )(qta, ka, vt, ka_m, vt_m, bias)


def _post_kernel(ot_ref, pooled_ref, gates_ref, x_ref, wua_ref, wup_ref, wout_ref, g_ref,
                 wr_ref, br_ref, h2_ref, info_ref, cnt_ref, cnt_sc, *, tm):
    @pl.when((pl.program_id(0) == 0) & (pl.program_id(1) == 0))
    def _():
        cnt_sc[...] = jnp.zeros(cnt_sc.shape, F32)

    sub = tm // POST_CHAINS
    lane = lax.broadcasted_iota(jnp.int32, (sub, LANES), 1)
    neg_inf = jnp.float32(-jnp.inf)
    r_i = lax.broadcasted_iota(jnp.int32, (sub, sub), 0)
    c_i = lax.broadcasted_iota(jnp.int32, (sub, sub), 1)
    earlier = jnp.where(c_i < r_i, 1.0, 0.0).astype(BF16)
    base = cnt_sc[0:1, :]

    for chain in range(POST_CHAINS):
        rows = slice(chain * sub, (chain + 1) * sub)
        y_attn = lax.dot_general(ot_ref[:, rows], wua_ref[...], _TN, preferred_element_type=F32)
        y_pool = jnp.dot(pooled_ref[rows, :], wup_ref[...], preferred_element_type=F32)
        merged = (gates_ref[rows, 0:D_MODEL].astype(F32) * y_attn
                  + gates_ref[rows, D_MODEL:2 * D_MODEL].astype(F32) * y_pool)
        h2 = x_ref[rows, :] + jnp.dot(merged.astype(BF16), wout_ref[...],
                                      preferred_element_type=F32)
        h2_ref[rows, :] = h2
        ms = jnp.mean(h2 * h2, axis=-1, keepdims=True)
        hn2 = (h2 * lax.rsqrt(ms + RMS_EPS) * g_ref[...]).astype(BF16)

        logits = jnp.dot(hn2, wr_ref[...], preferred_element_type=F32) + br_ref[...]

        def softmax_over(mask):
            z = jnp.where(mask, logits, neg_inf)
            e = jnp.exp(z - jnp.max(z, axis=1, keepdims=True))
            return jnp.where(mask, e / jnp.sum(e, axis=1, keepdims=True), -1.0)

        def top1(prob):
            top = jnp.max(prob, axis=1, keepdims=True)
            idx = jnp.min(jnp.where(prob == top, lane, LANES), axis=1, keepdims=True)
            return top, idx

        g_top, g_idx = top1(softmax_over(lane < N_GROUPS))
        e_lo = ROUTER_LANE0 + N_PER_GROUP * g_idx
        e_prob = softmax_over((lane >= e_lo) & (lane < e_lo + N_PER_GROUP))
        p1, i1 = top1(e_prob)
        p2, i2 = top1(jnp.where(lane == i1, -1.0, e_prob))
        denom = p1 + p2
        w1 = p1 / denom * g_top
        w2 = p2 / denom * g_top

        first_is_lo = i1 < i2
        a = jnp.minimum(i1, i2) - e_lo
        b = jnp.maximum(i1, i2) - e_lo
        w_lo = jnp.where(first_is_lo, w1, w2)
        w_hi = jnp.where(first_is_lo, w2, w1)
        pair = ((a * (2 * N_PER_GROUP - 1 - a)) >> 1) + (b - a - 1)
        cls = g_idx * PAIRS_PER_GROUP + pair

        member = lane == cls
        onehot = jnp.where(member, 1.0, 0.0)
        prefix = jnp.dot(earlier, onehot.astype(BF16), preferred_element_type=F32)
        rank = jnp.sum(jnp.where(member, prefix + base, 0.0), axis=1, keepdims=True)
        base = base + jnp.sum(onehot, axis=0, keepdims=True)

        info_ref[rows, :] = jnp.where(
            lane == INFO_CLASS, cls.astype(F32),
            jnp.where(lane == INFO_RANK, rank,
                      jnp.where(lane == INFO_W_LO, w_lo,
                                jnp.where(lane == INFO_W_HI, w_hi, 0.0))))

    total = jnp.broadcast_to(base, cnt_sc.shape)
    cnt_sc[...] = total
    cnt_ref[...] = total


def _post(ot, pooled, gates, x, p, *, tm):
    nb, seq, _ = x.shape
    tok = lambda width: pl.BlockSpec((None, tm, width), lambda b, j: (b, j, 0))
    return pl.pallas_call(
        functools.partial(_post_kernel, tm=tm),
        grid=(nb, seq // tm),
        in_specs=[
            pl.BlockSpec((None, ATTN_WIDTH, tm), lambda b, j: (b, 0, j)),
            tok(POOL_WIDTH), tok(2 * D_MODEL), tok(D_MODEL),
            _const_spec((ATTN_WIDTH, D_MODEL)),
            _const_spec((POOL_WIDTH, D_MODEL)),
            _const_spec((D_MODEL, D_MODEL)),
            _const_spec((1, D_MODEL)),
            _const_spec((D_MODEL, LANES)),
            _const_spec((1, LANES)),
        ],
        out_specs=[tok(D_MODEL), tok(LANES), pl.BlockSpec((8, LANES), lambda b, j: (0, 0))],
        out_shape=[
            jax.ShapeDtypeStruct((nb, seq, D_MODEL), F32),
            jax.ShapeDtypeStruct((nb, seq, LANES), F32),
            jax.ShapeDtypeStruct((8, LANES), F32),
        ],
        scratch_shapes=[pltpu.VMEM((8, LANES), F32)],
        compiler_params=pltpu.CompilerParams(
            dimension_semantics=("arbitrary", "arbitrary"), vmem_limit_bytes=VMEM_LIMIT),
        name="post_router",
    )(ot, pooled, gates, x, p["wua"], p["wup"], p["wout"], p["g_ffn"], p["wr"], p["br"])


def _to_slots(dst, slot, values, first_row, stride):
    n = values.shape[0]
    for c in range(values.shape[1] // LANES):
        dst[slot, pl.ds(first_row + c, n, stride=stride), :] = values[:, c * LANES:(c + 1) * LANES]


def _from_slots(src, n, stride):
    return jnp.concatenate(
        [src[pl.ds(c, n, stride=stride), :] for c in range(D_MODEL // LANES)], axis=1)


def _dispatch_kernel(pos_ref, h2_ref, info_ref, xs_hbm, buf, sem, *, tm):
    j = pl.program_id(0)
    slot = j % 2

    def drain(s):
        pltpu.make_async_copy(buf.at[s], xs_hbm.at[pl.ds(0, tm * IN_SLOT_ROWS), :],
                              sem.at[s]).wait()

    @pl.when(j < 2)
    def _():
        buf[slot] = jnp.zeros(buf.shape[1:], F32)

    @pl.when(j >= 2)
    def _():
        drain(slot)

    _to_slots(buf, slot, h2_ref[...], 0, IN_SLOT_ROWS)
    _to_slots(buf, slot, info_ref[...], SLOT_ROWS, IN_SLOT_ROWS)

    def body(g, c):
        for u in range(DMA_UNROLL):
            r = g * DMA_UNROLL + u
            src = pl.multiple_of(r * IN_SLOT_ROWS, IN_SLOT_ROWS)
            dst = pl.multiple_of(pos_ref[0, r] * IN_SLOT_ROWS, IN_SLOT_ROWS)
            pltpu.make_async_copy(buf.at[slot, pl.ds(src, IN_SLOT_ROWS), :],
                                  xs_hbm.at[pl.ds(dst, IN_SLOT_ROWS), :], sem.at[slot]).start()
        return c
    lax.fori_loop(0, tm // DMA_UNROLL, body, 0)

    @pl.when(j == pl.num_programs(0) - 1)
    def _():
        drain(slot)
        drain(1 - slot)


def _dispatch(h2, info, pos, *, tm):
    t = h2.shape[0]
    nt = t // tm
    return pl.pallas_call(
        functools.partial(_dispatch_kernel, tm=tm),
        grid=(nt,),
        in_specs=[
            pl.BlockSpec((None, 1, tm), lambda j: (j, 0, 0), memory_space=pltpu.SMEM),
            pl.BlockSpec((tm, D_MODEL), lambda j: (j, 0)),
            pl.BlockSpec((tm, LANES), lambda j: (j, 0)),
        ],
        out_specs=pl.BlockSpec(memory_space=pl.ANY),
        out_shape=jax.ShapeDtypeStruct((t * IN_SLOT_ROWS, LANES), F32),
        scratch_shapes=[
            pltpu.VMEM((2, tm * IN_SLOT_ROWS, LANES), F32),
            pltpu.SemaphoreType.DMA((2,)),
        ],
        compiler_params=pltpu.CompilerParams(
            dimension_semantics=("arbitrary",), vmem_limit_bytes=VMEM_LIMIT),
        name="moe_dispatch",
    )(pos.reshape(nt, 1, tm), h2, info)


def _moe_kernel(tile_ref, ea_ref, eb_ref, lo_ref, hi_ref, nitems_ref,
                xs_ref, g_ref, w13a_ref, w13b_ref, w2a_ref, w2b_ref, ys_ref, acc):
    del ea_ref, eb_ref
    k = pl.program_id(0)

    @pl.when(k < nitems_ref[0])
    def _():
        h2 = _from_slots(xs_ref, MOE_TILE, IN_SLOT_ROWS)
        info = xs_ref[pl.ds(SLOT_ROWS, MOE_TILE, stride=IN_SLOT_ROWS), :]
        row = lax.broadcasted_iota(jnp.int32, (MOE_TILE, 1), 0)
        in_segment = (row >= lo_ref[k]) & (row < hi_ref[k])
        w_lo = jnp.where(in_segment, info[:, INFO_W_LO:INFO_W_LO + 1], 0.0)
        w_hi = jnp.where(in_segment, info[:, INFO_W_HI:INFO_W_HI + 1], 0.0)
        ms = jnp.mean(h2 * h2, axis=-1, keepdims=True)
        xn = (h2 * lax.rsqrt(ms + RMS_EPS) * g_ref[...]).astype(BF16)

        def expert(w13_ref, w2_ref, weight):
            h = jnp.dot(xn, w13_ref[...], preferred_element_type=F32)
            h1 = h[:, 0:EXPERT_HIDDEN]
            h3 = h[:, EXPERT_HIDDEN:2 * EXPERT_HIDDEN]
            hh = h1 * jax.nn.sigmoid(h1) * h3 * weight
            return jnp.dot(hh.astype(BF16), w2_ref[...], preferred_element_type=F32)

        y = expert(w13a_ref, w2a_ref, w_lo) + expert(w13b_ref, w2b_ref, w_hi)
        new_tile = (k == 0) | (tile_ref[k] != tile_ref[jnp.maximum(k - 1, 0)])

        @pl.when(new_tile)
        def _():
            acc[...] = h2 + y

        @pl.when(jnp.logical_not(new_tile))
        def _():
            acc[...] += y

        total = acc[...]
        for c in range(D_MODEL // LANES):
            ys_ref[pl.ds(c, MOE_TILE, stride=SLOT_ROWS), :] = total[:, c * LANES:(c + 1) * LANES]


def _combine_kernel(pos_ref, pos_next_ref, ys_hbm, out_ref, buf, sem, *, tm):
    j = pl.program_id(0)
    slot = j % 2

    def start_tile(pos, s):
        def body(g, c):
            for u in range(DMA_UNROLL):
                r = g * DMA_UNROLL + u
                src = pl.multiple_of(pos[0, r] * SLOT_ROWS, SLOT_ROWS)
                dst = pl.multiple_of(r * SLOT_ROWS, SLOT_ROWS)
                pltpu.make_async_copy(ys_hbm.at[pl.ds(src, SLOT_ROWS), :],
                                      buf.at[s, pl.ds(dst, SLOT_ROWS), :], sem.at[s]).start()
            return c
        lax.fori_loop(0, tm // DMA_UNROLL, body, 0)

    @pl.when(j == 0)
    def _():
        start_tile(pos_ref, 0)

    @pl.when(j + 1 < pl.num_programs(0))
    def _():
        start_tile(pos_next_ref, 1 - slot)

    pltpu.make_async_copy(ys_hbm.at[pl.ds(0, tm * SLOT_ROWS), :], buf.at[slot], sem.at[slot]).wait()
    out_ref[...] = _from_slots(buf.at[slot], tm, SLOT_ROWS)


def _combine(ys, pos, *, tm):
    t = pos.shape[0]
    nt = t // tm
    pos3 = pos.reshape(nt, 1, tm)
    smem_tile = lambda fn: pl.BlockSpec((None, 1, tm), fn, memory_space=pltpu.SMEM)
    return pl.pallas_call(
        functools.partial(_combine_kernel, tm=tm),
        grid=(nt,),
        in_specs=[
            smem_tile(lambda j: (j, 0, 0)),
            smem_tile(lambda j: (jnp.minimum(j + 1, nt - 1), 0, 0)),
            pl.BlockSpec(memory_space=pl.ANY),
        ],
        out_specs=pl.BlockSpec((tm, D_MODEL), lambda j: (j, 0)),
        out_shape=jax.ShapeDtypeStruct((t, D_MODEL), F32),
        scratch_shapes=[
            pltpu.VMEM((2, tm * SLOT_ROWS, LANES), F32),
            pltpu.SemaphoreType.DMA((2,)),
        ],
        compiler_params=pltpu.CompilerParams(
            dimension_semantics=("arbitrary",), vmem_limit_bytes=VMEM_LIMIT),
        name="moe_combine",
    )(pos3, pos3, ys)


def _class_experts():
    ea, eb = [], []
    for g in range(N_GROUPS):
        for a in range(N_PER_GROUP):
            for b in range(a + 1, N_PER_GROUP):
                ea.append(g * N_PER_GROUP + a)
                eb.append(g * N_PER_GROUP + b)
    fill = LANES - len(ea)
    return (jnp.array(ea + [ea[-1]] * fill, jnp.int32),
            jnp.array(eb + [eb[-1]] * fill, jnp.int32))


def _moe(h2, info, counts, p, *, tm):
    t = h2.shape[0]
    assert t % tm == 0 and t // tm >= 2 and t % MOE_TILE == 0
    n_tiles = t // MOE_TILE
    max_items = n_tiles + N_CLASSES
    class_ids = jnp.arange(LANES, dtype=jnp.int32)
    c_end = jnp.cumsum(counts)
    c_start = c_end - counts
    cls = info[:, INFO_CLASS].astype(jnp.int32)
    rank = info[:, INFO_RANK].astype(jnp.int32)
    pos = jnp.sum(jnp.where(cls[:, None] == class_ids[None, :], c_start[None, :], 0),
                  axis=1) + rank
    first_tile = c_start // MOE_TILE
    n_items_c = jnp.where(counts > 0, (c_end - 1) // MOE_TILE - first_tile + 1, 0)
    item_end = jnp.cumsum(n_items_c)
    item_start = item_end - n_items_c
    n_items = item_end[-1]
    item_ids = jnp.minimum(jnp.arange(max_items, dtype=jnp.int32), n_items - 1)
    icls = jnp.sum(item_end[None, :] <= item_ids[:, None], axis=1, dtype=jnp.int32)
    pick = lambda table: jnp.sum(
        jnp.where(icls[:, None] == class_ids[None, :], table[None, :], 0), axis=1)
    tile = pick(first_tile) + item_ids - pick(item_start)
    lo = jnp.maximum(pick(c_start) - tile * MOE_TILE, 0)
    hi = jnp.minimum(pick(c_end) - tile * MOE_TILE, MOE_TILE)
    ea_tab, eb_tab = _class_experts()
    ea, eb = pick(ea_tab), pick(eb_tab)

    xs = _dispatch(h2, info, pos, tm=tm)

    first = lambda ea, eb: ea
    second = lambda ea, eb: eb
    w13 = lambda sel: pl.BlockSpec((None, D_MODEL, 2 * EXPERT_HIDDEN),
                                   lambda k, tl, ea, eb, lo, hi, n: (sel(ea, eb)[k], 0, 0))
    w2 = lambda sel: pl.BlockSpec((None, EXPERT_HIDDEN, D_MODEL),
                                  lambda k, tl, ea, eb, lo, hi, n: (sel(ea, eb)[k], 0, 0))
    ys = pl.pallas_call(
        _moe_kernel,
        grid_spec=pltpu.PrefetchScalarGridSpec(
            num_scalar_prefetch=6,
            grid=(max_items,),
            in_specs=[
                pl.BlockSpec((MOE_TILE * IN_SLOT_ROWS, LANES), lambda k, tl, *_: (tl[k], 0)),
                pl.BlockSpec((1, D_MODEL), lambda k, *_: (0, 0)),
                w13(first), w13(second), w2(first), w2(second),
            ],
            out_specs=pl.BlockSpec((MOE_TILE * SLOT_ROWS, LANES), lambda k, tl, *_: (tl[k], 0)),
            scratch_shapes=[pltpu.VMEM((MOE_TILE, D_MODEL), F32)],
        ),
        out_shape=jax.ShapeDtypeStruct((t * SLOT_ROWS, LANES), F32),
        compiler_params=pltpu.CompilerParams(
            dimension_semantics=("arbitrary",), vmem_limit_bytes=VMEM_LIMIT),
        name="moe",
    )(tile, ea, eb, lo, hi, n_items.reshape(1), xs, p["g_ffn"],
      p["w13"], p["w13"], p["w2"], p["w2"])
    return _combine(ys, pos, tm=tm)


def _bias_selectors():
    selq = [[0.0] * LANES for _ in range(LANES)]
    selk = [[0.0] * LANES for _ in range(LANES)]
    ones_row = 3 * N_HEADS
    for h in range(N_HEADS):
        for part in range(3):
            selq[h * AUG_ROWS + part][part * N_HEADS + h] = 1.0
            selq[h * AUG_ROWS + 3 + part][ones_row] = 1.0
            selk[h * AUG_ROWS + part][ones_row] = 1.0
            selk[h * AUG_ROWS + 3 + part][part * N_HEADS + h] = -1.0
    return jnp.array(selq, BF16), jnp.array(selk, BF16)


def _prepare_params(norm_mix_g, w_in, b_forget, q_norm_g, k_norm_g, w_up_attn, w_pool,
                    pool_scale, w_up_pool, w_out, norm_ffn_g, w_group, b_group, w_router,
                    b_router, w1, w3, w2):
    aw, pw = ATTN_WIDTH, POOL_WIDTH
    f_off = 3 * aw
    p_off = f_off + N_HEADS
    g_off = p_off + pw
    selq, selk = _bias_selectors()
    pad_lanes = lambda a: jnp.pad(a, ((0, 0), (0, LANES - a.shape[1])))
    wpool = jnp.zeros((pw, pw), F32)
    for g in range(len(POOL_WINDOWS)):
        sl = slice(g * POOL_GROUP_DIM, (g + 1) * POOL_GROUP_DIM)
        wpool = wpool.at[sl, sl].set(w_pool[g])
    d, f = D_MODEL, EXPERT_HIDDEN
    return {
        "g_mix": norm_mix_g.reshape(1, d),
        "wqkv": w_in[:, 0:f_off].astype(BF16),
        "wf": pad_lanes(w_in[:, f_off:p_off]).astype(BF16),
        "bf": pad_lanes(b_forget.reshape(1, N_HEADS)),
        "wp": w_in[:, p_off:g_off].astype(BF16),
        "wg": w_in[:, g_off:].astype(BF16),
        "gq": q_norm_g, "gk": k_norm_g,
        "selq": selq, "selk": selk,
        "wpool": wpool.astype(BF16),
        "pscale": pool_scale.reshape(1, pw),
        "wua": w_up_attn.astype(BF16),
        "wup": w_up_pool.astype(BF16),
        "wout": w_out.astype(BF16),
        "g_ffn": norm_ffn_g.reshape(1, d),
        "wr": pad_lanes(jnp.concatenate([w_group, w_router], axis=1)).astype(BF16),
        "br": pad_lanes(jnp.concatenate([b_group, b_router]).reshape(1, -1)),
        "w13": jnp.concatenate([w1, w3], axis=-1).reshape(N_EXPERTS, d, 2 * f).astype(BF16),
        "w2": w2.reshape(N_EXPERTS, f, d).astype(BF16),
    }


def kernel(x, meta_tokens, norm_mix_g, w_in, b_forget, q_norm_g, k_norm_g, w_up_attn, w_pool,
           pool_scale, w_up_pool, w_out, norm_ffn_g, w_group, b_group, w_router, b_router,
           w1, w3, w2):
    nb, seq, d = x.shape
    p = _prepare_params(norm_mix_g[0], w_in[0], b_forget[0], q_norm_g[0], k_norm_g[0],
                        w_up_attn[0], w_pool[0], pool_scale[0], w_up_pool[0], w_out[0],
                        norm_ffn_g[0], w_group[0], b_group[0], w_router[0], b_router[0],
                        w1[0], w3[0], w2[0])

    meta = jnp.pad(meta_tokens.astype(x.dtype), ((0, LANES - N_META), (0, 0)))[None]
    _, ka_m, vt_m, _, _, u_meta, f_meta = _inproj(
        meta, p, jnp.zeros((MAX_WINDOW, POOL_WIDTH), F32), jnp.zeros((N_HEADS, LANES), F32),
        tm=LANES, n_valid=N_META, first_pos=0)

    tm = 512
    qta, ka, vt, pooled, gates, _, _ = _inproj(
        x, p, u_meta, f_meta, tm=tm, n_valid=tm, first_pos=N_META)
    ot = _attention(qta, ka, vt, ka_m, vt_m, tq=512)
    h2, info, cnt = _post(ot, pooled, gates, x, p, tm=512)
    t = nb * seq
    out = _moe(h2.reshape(t, d), info.reshape(t, LANES), cnt[0].astype(jnp.int32), p, tm=512)
    return out.reshape(nb, seq, d)
```

```python
import functools
import math

import jax
import jax.numpy as jnp
import numpy as np
from jax import lax
from jax.experimental import pallas as pl
from jax.experimental.pallas import tpu as pltpu

F32 = jnp.float32
BF16 = jnp.bfloat16

D_MODEL = 1024
N_META = 16
N_HEADS = 8
HEAD_DIM = 64
ATTN_WIDTH = N_HEADS * HEAD_DIM
POOL_WINDOWS = (2, 4, 8, 16)
POOL_WIDTH = 512
POOL_GROUP_DIM = POOL_WIDTH // len(POOL_WINDOWS)
MAX_WINDOW = max(POOL_WINDOWS)
N_GROUPS = 4
N_PER_GROUP = 8
N_EXPERTS = N_GROUPS * N_PER_GROUP
EXPERT_HIDDEN = 256
RMS_EPS = 1e-6
LOG2_E = math.log2(math.e)

LANES = 128
AUG_ROWS = 16
HEAD_ROWS = HEAD_DIM + AUG_ROWS
QK_ROWS = N_HEADS * HEAD_ROWS
ROUTER_LANE0 = N_GROUPS
PAIRS_PER_GROUP = N_PER_GROUP * (N_PER_GROUP - 1) // 2
N_CLASSES = N_GROUPS * PAIRS_PER_GROUP
ROUTER_ROWS = 40
PLAN_CLASS, PLAN_RANK = 0, 1
MOE_ROWS = 384
SLOT_ROWS = D_MODEL // LANES
DMA_UNROLL = 8
VMEM_LIMIT = 56 * 1024 * 1024

_TN = (((0,), (0,)), ((), ()))


def _const_spec(shape):
    zeros = (0,) * len(shape)
    return pl.BlockSpec(shape, lambda *_: zeros, pipeline_mode=pl.Buffered(1))


def _split3(x):
    hi = x.astype(BF16).astype(F32)
    r = x - hi
    mid = r.astype(BF16).astype(F32)
    lo = (r - mid).astype(BF16).astype(F32)
    return hi, mid, lo


def _inproj_kernel(x_ref, g_ref, wqkv_ref, wf_ref, bf_ref, wp_ref, wg_ref, gq_ref, gk_ref,
                   selq_ref, selk_ref, wpool_ref, pscale_ref, uprev_ref, fprev_ref,
                   qta_ref, ka_ref, vt_ref, pooled_ref, gates_ref, utail_ref, ftail_ref,
                   ubuf, fcarry, *, tm, n_valid, first_pos):
    j = pl.program_id(1)

    @pl.when(j == 0)
    def _():
        ubuf[0:MAX_WINDOW, :] = uprev_ref[...]
        fcarry[...] = fprev_ref[...]

    x = x_ref[...]
    ms = jnp.mean(x * x, axis=-1, keepdims=True)
    hn = (x * lax.rsqrt(ms + RMS_EPS) * g_ref[...]).astype(BF16)

    f = jnp.dot(hn, wf_ref[...], preferred_element_type=F32) + bf_ref[...]
    z = f.T[0:N_HEADS, :]
    logf = (jnp.minimum(z, 0.0) - jnp.log1p(jnp.exp(-jnp.abs(z)))) * LOG2_E
    parts = jnp.concatenate(_split3(logf), axis=0).astype(BF16)
    r_i = lax.broadcasted_iota(jnp.int32, (tm, tm), 0)
    c_i = lax.broadcasted_iota(jnp.int32, (tm, tm), 1)
    tri = jnp.where(r_i <= c_i, 1.0, 0.0).astype(BF16)
    cs3 = jnp.dot(parts, tri, preferred_element_type=F32)
    cs = cs3[0:8] + cs3[8:16] + cs3[16:24]
    fc = jnp.concatenate([fcarry[...]] * (tm // LANES), axis=1) + cs
    last = jnp.broadcast_to(fc[:, n_valid - 1:n_valid], (N_HEADS, LANES))
    fcarry[...] = last
    ftail_ref[...] = last

    pieces = jnp.concatenate(
        _split3(fc) + (jnp.ones((8, tm), F32), jnp.zeros((LANES - 32, tm), F32)),
        axis=0).astype(BF16)
    fq = jnp.dot(selq_ref[...], pieces, preferred_element_type=F32)
    fk = jnp.dot(selk_ref[...], pieces, preferred_element_type=F32)

    qkv = jnp.dot(hn, wqkv_ref[...], preferred_element_type=F32)
    qt = qkv[:, 0:ATTN_WIDTH].T
    kt = qkv[:, ATTN_WIDTH:2 * ATTN_WIDTH].T
    vt_ref[...] = qkv[:, 2 * ATTN_WIDTH:3 * ATTN_WIDTH].T.astype(BF16)

    def head_norm(src, gain_ref, h):
        xh = src[h * HEAD_DIM:(h + 1) * HEAD_DIM, :]
        ssq = jnp.mean(xh * xh, axis=0, keepdims=True)
        return xh * lax.rsqrt(ssq + RMS_EPS) * gain_ref[...]

    k_pad = jnp.zeros((LANES - HEAD_ROWS, tm), F32)
    for h in range(N_HEADS):
        aug = slice(h * AUG_ROWS, (h + 1) * AUG_ROWS)
        r0 = h * HEAD_ROWS
        qta_ref[r0:r0 + HEAD_DIM, :] = head_norm(qt, gq_ref, h).astype(BF16)
        qta_ref[r0 + HEAD_DIM:r0 + HEAD_ROWS, :] = fq[aug, :].astype(BF16)
        k_blk = jnp.concatenate([head_norm(kt, gk_ref, h), fk[aug, :], k_pad], axis=0)
        ka_ref[:, h * LANES:(h + 1) * LANES] = k_blk.T.astype(BF16)

    u = jnp.dot(hn, wp_ref[...], preferred_element_type=F32)
    ubuf[MAX_WINDOW:MAX_WINDOW + tm, :] = u
    mixed = []
    for g, w in enumerate(POOL_WINDOWS):
        c0 = g * POOL_GROUP_DIM
        acc = u[:, c0:c0 + POOL_GROUP_DIM]
        for s in range(1, w):
            acc = acc + ubuf[MAX_WINDOW - s:MAX_WINDOW - s + tm, c0:c0 + POOL_GROUP_DIM]
        if first_pos + 1 >= w:
            mean = acc * (1.0 / w)
        else:
            pos = first_pos + j * tm + lax.broadcasted_iota(jnp.int32, (tm, POOL_GROUP_DIM), 0)
            mean = acc / jnp.minimum(pos + 1, w).astype(F32)
        mixed.append(mean - u[:, c0:c0 + POOL_GROUP_DIM])
    mixed = jnp.concatenate(mixed, axis=1).astype(BF16)
    y = jnp.dot(mixed, wpool_ref[...], preferred_element_type=F32) * pscale_ref[...]
    pooled_ref[...] = y.astype(BF16)
    tail = u[n_valid - MAX_WINDOW:n_valid, :]
    ubuf[0:MAX_WINDOW, :] = tail
    utail_ref[...] = tail

    gl = jnp.dot(hn, wg_ref[...], preferred_element_type=F32)
    gates_ref[...] = jax.nn.sigmoid(gl).astype(BF16)


def _inproj(x3, p, uprev, fprev, *, tm, n_valid, first_pos):
    nb, seq, _ = x3.shape
    nt = seq // tm
    gq = jnp.broadcast_to((p["gq"] * (LOG2_E / math.sqrt(HEAD_DIM)))[:, None], (HEAD_DIM, tm))
    gk = jnp.broadcast_to(p["gk"][:, None], (HEAD_DIM, tm))
    tok = lambda width: pl.BlockSpec((None, tm, width), lambda b, j: (b, j, 0))
    chan = lambda rows: pl.BlockSpec((None, rows, tm), lambda b, j: (b, 0, j))
    kern = functools.partial(_inproj_kernel, tm=tm, n_valid=n_valid, first_pos=first_pos)
    return pl.pallas_call(
        kern,
        grid=(nb, nt),
        in_specs=[
            tok(D_MODEL),
            _const_spec((1, D_MODEL)),
            _const_spec((D_MODEL, 3 * ATTN_WIDTH)),
            _const_spec((D_MODEL, LANES)),
            _const_spec((1, LANES)),
            _const_spec((D_MODEL, POOL_WIDTH)),
            _const_spec((D_MODEL, 2 * D_MODEL)),
            _const_spec((HEAD_DIM, tm)),
            _const_spec((HEAD_DIM, tm)),
            _const_spec((LANES, LANES)),
            _const_spec((LANES, LANES)),
            _const_spec((POOL_WIDTH, POOL_WIDTH)),
            _const_spec((1, POOL_WIDTH)),
            _const_spec((MAX_WINDOW, POOL_WIDTH)),
            _const_spec((N_HEADS, LANES)),
        ],
        out_specs=[
            chan(QK_ROWS), tok(N_HEADS * LANES), chan(ATTN_WIDTH),
            tok(POOL_WIDTH), tok(2 * D_MODEL),
            pl.BlockSpec((MAX_WINDOW, POOL_WIDTH), lambda b, j: (0, 0)),
            pl.BlockSpec((N_HEADS, LANES), lambda b, j: (0, 0)),
        ],
        out_shape=[
            jax.ShapeDtypeStruct((nb, QK_ROWS, seq), BF16),
            jax.ShapeDtypeStruct((nb, seq, N_HEADS * LANES), BF16),
            jax.ShapeDtypeStruct((nb, ATTN_WIDTH, seq), BF16),
            jax.ShapeDtypeStruct((nb, seq, POOL_WIDTH), BF16),
            jax.ShapeDtypeStruct((nb, seq, 2 * D_MODEL), BF16),
            jax.ShapeDtypeStruct((MAX_WINDOW, POOL_WIDTH), F32),
            jax.ShapeDtypeStruct((N_HEADS, LANES), F32),
        ],
        scratch_shapes=[
            pltpu.VMEM((MAX_WINDOW + tm, POOL_WIDTH), F32),
            pltpu.VMEM((N_HEADS, LANES), F32),
        ],
        compiler_params=pltpu.CompilerParams(
            dimension_semantics=("arbitrary", "arbitrary"), vmem_limit_bytes=VMEM_LIMIT),
        name="inproj",
    )(x3, p["g_mix"], p["wqkv"], p["wf"], p["bf"], p["wp"], p["wg"], gq, gk,
      p["selq"], p["selk"], p["wpool"], p["pscale"], uprev, fprev)


def _attn_kernel(qta_ref, ka_ref, vt_ref, kam_ref, vtm_ref, bias_ref, ot_ref, m_sc, acc_sc, *, tq):
    qi = pl.program_id(1)
    m_sc[...] = jnp.full(m_sc.shape, -jnp.inf, F32)
    acc_sc[...] = jnp.zeros(acc_sc.shape, F32)

    def rows(h):
        return slice(h * HEAD_ROWS, (h + 1) * HEAD_ROWS)

    def vrows(h):
        return slice(h * HEAD_DIM, (h + 1) * HEAD_DIM)

    def klanes(h):
        return slice(h * LANES, (h + 1) * LANES)

    q_pad = jnp.zeros((LANES - HEAD_ROWS, tq), BF16)

    def scores(h, ka):
        qa = jnp.concatenate([qta_ref[rows(h), :], q_pad], axis=0)
        return jnp.dot(ka, qa, preferred_element_type=F32)

    def new_max(h, s):
        return jnp.maximum(m_sc[h], jnp.max(s, axis=0, keepdims=True))

    def absorb(h, s, m_new, v):
        p = jnp.exp2(s - m_new).astype(BF16)
        va = jnp.concatenate([v, jnp.ones((AUG_ROWS, v.shape[1]), BF16)], axis=0)
        acc_sc[h] = (jnp.exp2(m_sc[h] - m_new) * acc_sc[h]
                     + jnp.dot(va, p, preferred_element_type=F32))
        m_sc[h] = m_new

    def sweep_heads(scores_of, values_of):
        s = {0: scores_of(0), 1: scores_of(1)}
        m = {0: new_max(0, s[0])}
        for h in range(N_HEADS):
            if h + 2 < N_HEADS:
                s[h + 2] = scores_of(h + 2)
            if h + 1 < N_HEADS:
                m[h + 1] = new_max(h + 1, s[h + 1])
            absorb(h, s.pop(h), m.pop(h), values_of(h))

    def kv_body(i, carry):
        s0 = pl.multiple_of(i * tq, tq)
        sweep_heads(lambda h: scores(h, ka_ref[pl.ds(s0, tq), klanes(h)]),
                    lambda h: vt_ref[vrows(h), pl.ds(s0, tq)])
        return carry

    lax.fori_loop(0, qi, kv_body, 0)

    s0 = pl.multiple_of(qi * tq, tq)
    sweep_heads(
        lambda h: scores(h, jnp.concatenate(
            [ka_ref[pl.ds(s0, tq), klanes(h)], kam_ref[:, klanes(h)]], axis=0)) + bias_ref[...],
        lambda h: jnp.concatenate([vt_ref[vrows(h), pl.ds(s0, tq)], vtm_ref[vrows(h), :]], axis=1))
    for h in range(N_HEADS):
        acc = acc_sc[h]
        ot_ref[vrows(h), :] = (acc[0:HEAD_DIM, :] / acc[HEAD_DIM:HEAD_DIM + 1, :]).astype(BF16)


def _attention(qta, ka, vt, ka_m, vt_m, *, tq):
    nb, _, seq = qta.shape
    kw = N_HEADS * LANES
    key = np.arange(tq + LANES)[:, None]
    qry = np.arange(tq)[None, :]
    visible = np.where(key < tq, key <= qry, key - tq < N_META)
    bias = jnp.asarray(np.where(visible, 0.0, -np.inf), F32)
    return pl.pallas_call(
        functools.partial(_attn_kernel, tq=tq),
        grid=(nb, seq // tq),
        in_specs=[
            pl.BlockSpec((None, QK_ROWS, tq), lambda b, i: (b, 0, i)),
            pl.BlockSpec((None, seq, kw), lambda b, i: (b, 0, 0)),
            pl.BlockSpec((None, ATTN_WIDTH, seq), lambda b, i: (b, 0, 0)),
            pl.BlockSpec((None, LANES, kw), lambda b, i: (0, 0, 0)),
            pl.BlockSpec((None, ATTN_WIDTH, LANES), lambda b, i: (0, 0, 0)),
            _const_spec((tq + LANES, tq)),
        ],
        out_specs=pl.BlockSpec((None, ATTN_WIDTH, tq), lambda b, i: (b, 0, i)),
        out_shape=jax.ShapeDtypeStruct((nb, ATTN_WIDTH, seq), BF16),
        scratch_shapes=[
            pltpu.VMEM((N_HEADS, 1, tq), F32),
            pltpu.VMEM((N_HEADS, HEAD_ROWS, tq), F32),
        ],
        compiler_params=pltpu.CompilerParams(
            dimension_semantics=("arbitrary", "arbitrary"), vmem_limit_bytes=VMEM_LIMIT),
        name="fox_attention",
    )(qta, ka, vt, ka_m, vt_m, bias)


def _post_kernel(ot_ref, pooled_ref, gates_ref, x_ref, wua_ref, wup_ref, wout_ref, g_ref,
                 wr_ref, br_ref, h2_ref, plan_ref, cnt_ref, cnt_sc, *, tm):
    @pl.when((pl.program_id(0) == 0) & (pl.program_id(1) == 0))
    def _():
        cnt_sc[...] = jnp.zeros(cnt_sc.shape, F32)

    y_attn = lax.dot_general(ot_ref[...], wua_ref[...], _TN, preferred_element_type=F32)
    y_pool = jnp.dot(pooled_ref[...], wup_ref[...], preferred_element_type=F32)
    merged = (gates_ref[:, 0:D_MODEL].astype(F32) * y_attn
              + gates_ref[:, D_MODEL:2 * D_MODEL].astype(F32) * y_pool)
    h2 = x_ref[...] + jnp.dot(merged.astype(BF16), wout_ref[...], preferred_element_type=F32)
    h2_ref[...] = h2
    ms = jnp.mean(h2 * h2, axis=-1, keepdims=True)
    hn2 = (h2 * lax.rsqrt(ms + RMS_EPS) * g_ref[...]).astype(BF16)

    logits = jnp.dot(hn2, wr_ref[...], preferred_element_type=F32) + br_ref[...]
    lt = logits.T[0:ROUTER_ROWS, :]
    row = lax.broadcasted_iota(jnp.int32, lt.shape, 0)
    neg_inf = jnp.float32(-jnp.inf)

    def softmax_over(mask):
        z = jnp.where(mask, lt, neg_inf)
        e = jnp.exp(z - jnp.max(z, axis=0, keepdims=True))
        return jnp.where(mask, e / jnp.sum(e, axis=0, keepdims=True), -1.0)

    def argtop(prob):
        top = jnp.max(prob, axis=0, keepdims=True)
        return jnp.min(jnp.where(prob == top, row, LANES), axis=0, keepdims=True)

    g_idx = argtop(softmax_over(row < N_GROUPS))
    e_lo = ROUTER_LANE0 + N_PER_GROUP * g_idx
    e_prob = softmax_over((row >= e_lo) & (row < e_lo + N_PER_GROUP))
    i1 = argtop(e_prob)
    i2 = argtop(jnp.where(row == i1, -1.0, e_prob))
    a = jnp.minimum(i1, i2) - e_lo
    b = jnp.maximum(i1, i2) - e_lo
    pair = ((a * (2 * N_PER_GROUP - 1 - a)) >> 1) + (b - a - 1)
    cls = g_idx * PAIRS_PER_GROUP + pair

    crow = lax.broadcasted_iota(jnp.int32, (LANES, tm), 0)
    member = crow == cls
    onehot = jnp.where(member, 1.0, 0.0)
    r_i = lax.broadcasted_iota(jnp.int32, (tm, tm), 0)
    c_i = lax.broadcasted_iota(jnp.int32, (tm, tm), 1)
    earlier = jnp.where(r_i < c_i, 1.0, 0.0).astype(BF16)
    prefix = jnp.dot(onehot.astype(BF16), earlier, preferred_element_type=F32)
    base = jnp.concatenate([cnt_sc[...]] * (tm // LANES), axis=1)
    rank = jnp.sum(jnp.where(member, prefix + base, 0.0), axis=0, keepdims=True)
    total = cnt_sc[...] + jnp.broadcast_to(jnp.sum(onehot, axis=1, keepdims=True), cnt_sc.shape)
    cnt_sc[...] = total
    cnt_ref[...] = total

    prow = lax.broadcasted_iota(jnp.int32, (8, tm), 0)
    plan_ref[...] = jnp.where(prow == PLAN_CLASS, cls.astype(F32),
                              jnp.where(prow == PLAN_RANK, rank, 0.0))


def _post(ot, pooled, gates, x, p, *, tm):
    nb, seq, _ = x.shape
    nt = seq // tm
    tok = lambda width: pl.BlockSpec((None, tm, width), lambda b, j: (b, j, 0))
    return pl.pallas_call(
        functools.partial(_post_kernel, tm=tm),
        grid=(nb, nt),
        in_specs=[
            pl.BlockSpec((None, ATTN_WIDTH, tm), lambda b, j: (b, 0, j)),
            tok(POOL_WIDTH), tok(2 * D_MODEL), tok(D_MODEL),
            _const_spec((ATTN_WIDTH, D_MODEL)),
            _const_spec((POOL_WIDTH, D_MODEL)),
            _const_spec((D_MODEL, D_MODEL)),
            _const_spec((1, D_MODEL)),
            _const_spec((D_MODEL, LANES)),
            _const_spec((1, LANES)),
        ],
        out_specs=[tok(D_MODEL),
                   pl.BlockSpec((None, 8, tm), lambda b, j: (b * nt + j, 0, 0)),
                   pl.BlockSpec((LANES, LANES), lambda b, j: (0, 0))],
        out_shape=[
            jax.ShapeDtypeStruct((nb, seq, D_MODEL), F32),
            jax.ShapeDtypeStruct((nb * nt, 8, tm), F32),
            jax.ShapeDtypeStruct((LANES, LANES), F32),
        ],
        scratch_shapes=[pltpu.VMEM((LANES, LANES), F32)],
        compiler_params=pltpu.CompilerParams(
            dimension_semantics=("arbitrary", "arbitrary"), vmem_limit_bytes=VMEM_LIMIT),
        name="post_router",
    )(ot, pooled, gates, x, p["wua"], p["wup"], p["wout"], p["g_ffn"], p["wr"], p["br"])


def _to_slots(dst, values):
    n = values.shape[0]
    for c in range(SLOT_ROWS):
        dst[pl.ds(c, n, stride=SLOT_ROWS), :] = values[:, c * LANES:(c + 1) * LANES]


def _from_slots(src, n):
    return jnp.concatenate(
        [src[pl.ds(c, n, stride=SLOT_ROWS), :] for c in range(SLOT_ROWS)], axis=1)


def _dispatch_kernel(pos_ref, h2_ref, xs_hbm, buf, zeros, sem, zsem, *, tm, n_slots):
    j = pl.program_id(0)
    slot = j % 2

    def drain(s):
        pltpu.make_async_copy(buf.at[s], xs_hbm.at[pl.ds(0, tm * SLOT_ROWS), :], sem.at[s]).wait()

    @pl.when(j >= 2)
    def _():
        drain(slot)

    _to_slots(buf.at[slot], h2_ref[...])

    def body(g, c):
        for u in range(DMA_UNROLL):
            r = g * DMA_UNROLL + u
            src = pl.multiple_of(r * SLOT_ROWS, SLOT_ROWS)
            dst = pl.multiple_of(pos_ref[0, r] * SLOT_ROWS, SLOT_ROWS)
            pltpu.make_async_copy(buf.at[slot, pl.ds(src, SLOT_ROWS), :],
                                  xs_hbm.at[pl.ds(dst, SLOT_ROWS), :], sem.at[slot]).start()
        return c
    lax.fori_loop(0, tm // DMA_UNROLL, body, 0)

    @pl.when(j == pl.num_programs(0) - 1)
    def _():
        zeros[...] = jnp.zeros(zeros.shape, F32)
        pad = pltpu.make_async_copy(
            zeros, xs_hbm.at[pl.ds(n_slots * SLOT_ROWS, MOE_ROWS * SLOT_ROWS), :], zsem.at[0])
        pad.start()
        drain(slot)
        drain(1 - slot)
        pad.wait()


def _dispatch(h2, pos, *, tm):
    t = h2.shape[0]
    nt = t // tm
    return pl.pallas_call(
        functools.partial(_dispatch_kernel, tm=tm, n_slots=t),
        grid=(nt,),
        in_specs=[
            pl.BlockSpec((None, 1, tm), lambda j: (j, 0, 0), memory_space=pltpu.SMEM),
            pl.BlockSpec((tm, D_MODEL), lambda j: (j, 0)),
        ],
        out_specs=pl.BlockSpec(memory_space=pl.ANY),
        out_shape=jax.ShapeDtypeStruct(((t + MOE_ROWS) * SLOT_ROWS, LANES), F32),
        scratch_shapes=[
            pltpu.VMEM((2, tm * SLOT_ROWS, LANES), F32),
            pltpu.VMEM((MOE_ROWS * SLOT_ROWS, LANES), F32),
            pltpu.SemaphoreType.DMA((2,)),
            pltpu.SemaphoreType.DMA((1,)),
        ],
        compiler_params=pltpu.CompilerParams(
            dimension_semantics=("arbitrary",), vmem_limit_bytes=VMEM_LIMIT),
        name="moe_dispatch",
    )(pos.reshape(nt, 1, tm), h2)


def _moe_kernel(start_ref, nrows_ref, ea_ref, eb_ref, nitems_ref,
                g_ref, wr_ref, br_ref, w13a_ref, w13b_ref, w2a_ref, w2b_ref, xs_hbm,
                ys_hbm, xbuf, ybuf, isem, osem):
    k = pl.program_id(0)
    n_items = nitems_ref[0]
    slot = k % 2
    rows = MOE_ROWS * SLOT_ROWS

    def window(kk):
        return pl.ds(pl.multiple_of(start_ref[kk] * SLOT_ROWS, SLOT_ROWS), rows)

    def in_copy(kk, s):
        return pltpu.make_async_copy(xs_hbm.at[window(kk), :], xbuf.at[s], isem.at[s])

    def out_copy(kk, s):
        return pltpu.make_async_copy(ybuf.at[s], ys_hbm.at[window(kk), :], osem.at[s])

    @pl.when(k == 0)
    def _():
        in_copy(0, 0).start()
        ybuf[1] = jnp.zeros(ybuf.shape[1:], F32)
        pad = pltpu.make_async_copy(
            ybuf.at[1], ys_hbm.at[pl.ds(ys_hbm.shape[0] - rows, rows), :], osem.at[1])
        pad.start()
        pad.wait()

    @pl.when(k + 1 < n_items)
    def _():
        in_copy(k + 1, 1 - slot).start()

    @pl.when(k < n_items)
    def _():
        in_copy(k, slot).wait()
        h2 = _from_slots(xbuf.at[slot], MOE_ROWS)
        ms = jnp.mean(h2 * h2, axis=-1, keepdims=True)
        xn = (h2 * lax.rsqrt(ms + RMS_EPS) * g_ref[...]).astype(BF16)

        logits = jnp.dot(xn, wr_ref[...], preferred_element_type=F32) + br_ref[...]
        lane = lax.broadcasted_iota(jnp.int32, logits.shape, 1)
        pick = lambda idx: jnp.sum(jnp.where(lane == idx, logits, 0.0), axis=1, keepdims=True)
        ea, eb = ea_ref[k], eb_ref[k]
        gl = jnp.where(lane < N_GROUPS, logits, -jnp.inf)
        gmax = jnp.max(gl, axis=1, keepdims=True)
        p_group = (jnp.exp(pick(ea // N_PER_GROUP) - gmax)
                   / jnp.sum(jnp.exp(gl - gmax), axis=1, keepdims=True))
        la, lb = pick(ROUTER_LANE0 + ea), pick(ROUTER_LANE0 + eb)
        valid = lax.broadcasted_iota(jnp.int32, (MOE_ROWS, 1), 0) < nrows_ref[k]
        w_lo = jnp.where(valid, p_group / (1.0 + jnp.exp(lb - la)), 0.0)
        w_hi = jnp.where(valid, p_group / (1.0 + jnp.exp(la - lb)), 0.0)

        def expert(w13_ref, w2_ref, weight):
            h = jnp.dot(xn, w13_ref[...], preferred_element_type=F32)
            h1 = h[:, 0:EXPERT_HIDDEN]
            h3 = h[:, EXPERT_HIDDEN:2 * EXPERT_HIDDEN]
            hh = h1 * jax.nn.sigmoid(h1) * h3 * weight
            return jnp.dot(hh.astype(BF16), w2_ref[...], preferred_element_type=F32)

        y = h2 + (expert(w13a_ref, w2a_ref, w_lo) + expert(w13b_ref, w2b_ref, w_hi))

        @pl.when(k >= 1)
        def _():
            out_copy(k - 1, 1 - slot).wait()

        _to_slots(ybuf.at[slot], y)
        out_copy(k, slot).start()

        @pl.when(k == n_items - 1)
        def _():
            out_copy(k, slot).wait()


def _class_experts():
    ea, eb = [], []
    for g in range(N_GROUPS):
        for a in range(N_PER_GROUP):
            for b in range(a + 1, N_PER_GROUP):
                ea.append(g * N_PER_GROUP + a)
                eb.append(g * N_PER_GROUP + b)
    fill = LANES - len(ea)
    return (jnp.array(ea + [ea[-1]] * fill, jnp.int32),
            jnp.array(eb + [eb[-1]] * fill, jnp.int32))


def _combine_kernel(pos_ref, pos_next_ref, ys_hbm, out_ref, buf, sem, *, tm):
    j = pl.program_id(0)
    slot = j % 2

    def start_tile(pos, s):
        def body(g, c):
            for u in range(DMA_UNROLL):
                r = g * DMA_UNROLL + u
                src = pl.multiple_of(pos[0, r] * SLOT_ROWS, SLOT_ROWS)
                dst = pl.multiple_of(r * SLOT_ROWS, SLOT_ROWS)
                pltpu.make_async_copy(ys_hbm.at[pl.ds(src, SLOT_ROWS), :],
                                      buf.at[s, pl.ds(dst, SLOT_ROWS), :], sem.at[s]).start()
            return c
        lax.fori_loop(0, tm // DMA_UNROLL, body, 0)

    @pl.when(j == 0)
    def _():
        start_tile(pos_ref, 0)

    @pl.when(j + 1 < pl.num_programs(0))
    def _():
        start_tile(pos_next_ref, 1 - slot)

    pltpu.make_async_copy(ys_hbm.at[pl.ds(0, tm * SLOT_ROWS), :], buf.at[slot], sem.at[slot]).wait()
    out_ref[...] = _from_slots(buf.at[slot], tm)


def _combine(ys, pos, *, tm):
    t = pos.shape[0]
    nt = t // tm
    pos3 = pos.reshape(nt, 1, tm)
    smem_tile = lambda fn: pl.BlockSpec((None, 1, tm), fn, memory_space=pltpu.SMEM)
    return pl.pallas_call(
        functools.partial(_combine_kernel, tm=tm),
        grid=(nt,),
        in_specs=[
            smem_tile(lambda j: (j, 0, 0)),
            smem_tile(lambda j: (jnp.minimum(j + 1, nt - 1), 0, 0)),
            pl.BlockSpec(memory_space=pl.ANY),
        ],
        out_specs=pl.BlockSpec((tm, D_MODEL), lambda j: (j, 0)),
        out_shape=jax.ShapeDtypeStruct((t, D_MODEL), F32),
        scratch_shapes=[
            pltpu.VMEM((2, tm * SLOT_ROWS, LANES), F32),
            pltpu.SemaphoreType.DMA((2,)),
        ],
        compiler_params=pltpu.CompilerParams(
            dimension_semantics=("arbitrary",), vmem_limit_bytes=VMEM_LIMIT),
        name="moe_combine",
    )(pos3, pos3, ys)


def _moe(h2, plan, counts, p, *, tm):
    t = h2.shape[0]
    assert t % tm == 0 and t // tm >= 2
    max_items = -(-t // MOE_ROWS) + N_CLASSES
    class_ids = jnp.arange(LANES, dtype=jnp.int32)
    c_end = jnp.cumsum(counts)
    c_start = c_end - counts
    cls = plan[:, PLAN_CLASS, :].reshape(t).astype(jnp.int32)
    rank = plan[:, PLAN_RANK, :].reshape(t).astype(jnp.int32)
    pos = jnp.sum(jnp.where(cls[:, None] == class_ids[None, :], c_start[None, :], 0),
                  axis=1) + rank
    n_items_c = (counts + MOE_ROWS - 1) // MOE_ROWS
    item_end = jnp.cumsum(n_items_c)
    item_start = item_end - n_items_c
    n_items = item_end[-1]
    item_ids = jnp.minimum(jnp.arange(max_items, dtype=jnp.int32), n_items - 1)
    icls = jnp.sum(item_end[None, :] <= item_ids[:, None], axis=1, dtype=jnp.int32)
    pick = lambda table: jnp.sum(
        jnp.where(icls[:, None] == class_ids[None, :], table[None, :], 0), axis=1)
    offset = (item_ids - pick(item_start)) * MOE_ROWS
    start = pick(c_start) + offset
    nrows = jnp.minimum(pick(counts) - offset, MOE_ROWS)
    ea_tab, eb_tab = _class_experts()
    ea, eb = pick(ea_tab), pick(eb_tab)

    xs = _dispatch(h2, pos, tm=tm)

    first = lambda ea, eb: ea
    second = lambda ea, eb: eb
    w13 = lambda sel: pl.BlockSpec((None, D_MODEL, 2 * EXPERT_HIDDEN),
                                   lambda k, st, nr, ea, eb, n: (sel(ea, eb)[k], 0, 0))
    w2 = lambda sel: pl.BlockSpec((None, EXPERT_HIDDEN, D_MODEL),
                                  lambda k, st, nr, ea, eb, n: (sel(ea, eb)[k], 0, 0))
    const = lambda shape: pl.BlockSpec(shape, lambda k, *_: (0, 0))
    ys = pl.pallas_call(
        _moe_kernel,
        grid_spec=pltpu.PrefetchScalarGridSpec(
            num_scalar_prefetch=5,
            grid=(max_items,),
            in_specs=[
                const((1, D_MODEL)), const((D_MODEL, LANES)), const((1, LANES)),
                w13(first), w13(second), w2(first), w2(second),
                pl.BlockSpec(memory_space=pl.ANY),
            ],
            out_specs=pl.BlockSpec(memory_space=pl.ANY),
            scratch_shapes=[
                pltpu.VMEM((2, MOE_ROWS * SLOT_ROWS, LANES), F32),
                pltpu.VMEM((2, MOE_ROWS * SLOT_ROWS, LANES), F32),
                pltpu.SemaphoreType.DMA((2,)),
                pltpu.SemaphoreType.DMA((2,)),
            ],
        ),
        out_shape=jax.ShapeDtypeStruct(((t + MOE_ROWS) * SLOT_ROWS, LANES), F32),
        compiler_params=pltpu.CompilerParams(
            dimension_semantics=("arbitrary",), vmem_limit_bytes=VMEM_LIMIT),
        name="moe",
    )(start, nrows, ea, eb, n_items.reshape(1), p["g_ffn"], p["wr"], p["br"],
      p["w13"], p["w13"], p["w2"], p["w2"], xs)
    return _combine(ys, pos, tm=tm)


def _bias_selectors():
    selq = [[0.0] * LANES for _ in range(LANES)]
    selk = [[0.0] * LANES for _ in range(LANES)]
    ones_row = 3 * N_HEADS
    for h in range(N_HEADS):
        for part in range(3):
            selq[h * AUG_ROWS + part][part * N_HEADS + h] = 1.0
            selq[h * AUG_ROWS + 3 + part][ones_row] = 1.0
            selk[h * AUG_ROWS + part][ones_row] = 1.0
            selk[h * AUG_ROWS + 3 + part][part * N_HEADS + h] = -1.0
    return jnp.array(selq, BF16), jnp.array(selk, BF16)


def _prepare_params(norm_mix_g, w_in, b_forget, q_norm_g, k_norm_g, w_up_attn, w_pool,
                    pool_scale, w_up_pool, w_out, norm_ffn_g, w_group, b_group, w_router,
                    b_router, w1, w3, w2):
    aw, pw = ATTN_WIDTH, POOL_WIDTH
    f_off = 3 * aw
    p_off = f_off + N_HEADS
    g_off = p_off + pw
    selq, selk = _bias_selectors()
    pad_lanes = lambda a: jnp.pad(a, ((0, 0), (0, LANES - a.shape[1])))
    wpool = jnp.zeros((pw, pw), F32)
    for g in range(len(POOL_WINDOWS)):
        sl = slice(g * POOL_GROUP_DIM, (g + 1) * POOL_GROUP_DIM)
        wpool = wpool.at[sl, sl].set(w_pool[g])
    d, f = D_MODEL, EXPERT_HIDDEN
    return {
        "g_mix": norm_mix_g.reshape(1, d),
        "wqkv": w_in[:, 0:f_off].astype(BF16),
        "wf": pad_lanes(w_in[:, f_off:p_off]).astype(BF16),
        "bf": pad_lanes(b_forget.reshape(1, N_HEADS)),
        "wp": w_in[:, p_off:g_off].astype(BF16),
        "wg": w_in[:, g_off:].astype(BF16),
        "gq": q_norm_g, "gk": k_norm_g,
        "selq": selq, "selk": selk,
        "wpool": wpool.astype(BF16),
        "pscale": pool_scale.reshape(1, pw),
        "wua": w_up_attn.astype(BF16),
        "wup": w_up_pool.astype(BF16),
        "wout": w_out.astype(BF16),
        "g_ffn": norm_ffn_g.reshape(1, d),
        "wr": pad_lanes(jnp.concatenate([w_group, w_router], axis=1)).astype(BF16),
        "br": pad_lanes(jnp.concatenate([b_group, b_router]).reshape(1, -1)),
        "w13": jnp.concatenate([w1, w3], axis=-1).reshape(N_EXPERTS, d, 2 * f).astype(BF16),
        "w2": w2.reshape(N_EXPERTS, f, d).astype(BF16),
    }


def kernel(x, meta_tokens, norm_mix_g, w_in, b_forget, q_norm_g, k_norm_g, w_up_attn, w_pool,
           pool_scale, w_up_pool, w_out, norm_ffn_g, w_group, b_group, w_router, b_router,
           w1, w3, w2):
    nb, seq, d = x.shape
    p = _prepare_params(norm_mix_g[0], w_in[0], b_forget[0], q_norm_g[0], k_norm_g[0],
                        w_up_attn[0], w_pool[0], pool_scale[0], w_up_pool[0], w_out[0],
                        norm_ffn_g[0], w_group[0], b_group[0], w_router[0], b_router[0],
                        w1[0], w3[0], w2[0])

    meta = jnp.pad(meta_tokens.astype(x.dtype), ((0, LANES - N_META), (0, 0)))[None]
    _, ka_m, vt_m, _, _, u_meta, f_meta = _inproj(
        meta, p, jnp.zeros((MAX_WINDOW, POOL_WIDTH), F32), jnp.zeros((N_HEADS, LANES), F32),
        tm=LANES, n_valid=N_META, first_pos=0)

    tm = 512
    qta, ka, vt, pooled, gates, _, _ = _inproj(
        x, p, u_meta, f_meta, tm=tm, n_valid=tm, first_pos=N_META)
    ot = _attention(qta, ka, vt, ka_m, vt_m, tq=512)
    h2, plan, cnt = _post(ot, pooled, gates, x, p, tm=512)
    out = _moe(h2.reshape(nb * seq, d), plan, cnt[:, 0].astype(jnp.int32), p, tm=512)
    return out.reshape(nb, seq, d)
```

```python
import functools
import math

import jax
import jax.numpy as jnp
import numpy as np
from jax import lax
from jax.experimental import pallas as pl
from jax.experimental.pallas import tpu as pltpu

F32 = jnp.float32
BF16 = jnp.bfloat16

D_MODEL = 1024
N_META = 16
N_HEADS = 8
HEAD_DIM = 64
ATTN_WIDTH = N_HEADS * HEAD_DIM
POOL_WINDOWS = (2, 4, 8, 16)
POOL_WIDTH = 512
POOL_GROUP_DIM = POOL_WIDTH // len(POOL_WINDOWS)
MAX_WINDOW = max(POOL_WINDOWS)
N_GROUPS = 4
N_PER_GROUP = 8
N_EXPERTS = N_GROUPS * N_PER_GROUP
EXPERT_HIDDEN = 256
RMS_EPS = 1e-6
LOG2_E = math.log2(math.e)

LANES = 128
AUG_ROWS = 16
HEAD_ROWS = HEAD_DIM + AUG_ROWS
QK_ROWS = N_HEADS * HEAD_ROWS
ROUTER_LANE0 = N_GROUPS
PAIRS_PER_GROUP = N_PER_GROUP * (N_PER_GROUP - 1) // 2
N_CLASSES = N_GROUPS * PAIRS_PER_GROUP
ROUTER_ROWS = 40
PLAN_CLASS, PLAN_RANK = 0, 1
MOE_ROWS = 352
SLOT_ROWS = D_MODEL // LANES
DMA_UNROLL = 8
DMA_QUEUES = 2
VMEM_LIMIT = 56 * 1024 * 1024

_TN = (((0,), (0,)), ((), ()))


def _const_spec(shape):
    zeros = (0,) * len(shape)
    return pl.BlockSpec(shape, lambda *_: zeros, pipeline_mode=pl.Buffered(1))


def _split3(x):
    hi = x.astype(BF16).astype(F32)
    r = x - hi
    mid = r.astype(BF16).astype(F32)
    lo = (r - mid).astype(BF16).astype(F32)
    return hi, mid, lo


def _inproj_kernel(x_ref, g_ref, wqkv_ref, wf_ref, bf_ref, wp_ref, wg_ref, gq_ref, gk_ref,
                   selq_ref, selk_ref, wpool_ref, pscale_ref, uprev_ref, fprev_ref,
                   qta_ref, ka_ref, vt_ref, pooled_ref, gates_ref, utail_ref, ftail_ref,
                   ubuf, fcarry, *, tm, n_valid, first_pos):
    j = pl.program_id(1)

    @pl.when(j == 0)
    def _():
        ubuf[0:MAX_WINDOW, :] = uprev_ref[...]
        fcarry[...] = fprev_ref[...]

    x = x_ref[...]
    ms = jnp.mean(x * x, axis=-1, keepdims=True)
    hn = (x * lax.rsqrt(ms + RMS_EPS) * g_ref[...]).astype(BF16)

    f = jnp.dot(hn, wf_ref[...], preferred_element_type=F32) + bf_ref[...]
    z = f.T[0:N_HEADS, :]
    logf = (jnp.minimum(z, 0.0) - jnp.log1p(jnp.exp(-jnp.abs(z)))) * LOG2_E
    parts = jnp.concatenate(_split3(logf), axis=0).astype(BF16)
    r_i = lax.broadcasted_iota(jnp.int32, (tm, tm), 0)
    c_i = lax.broadcasted_iota(jnp.int32, (tm, tm), 1)
    tri = jnp.where(r_i <= c_i, 1.0, 0.0).astype(BF16)
    cs3 = jnp.dot(parts, tri, preferred_element_type=F32)
    cs = cs3[0:8] + cs3[8:16] + cs3[16:24]
    fc = jnp.concatenate([fcarry[...]] * (tm // LANES), axis=1) + cs
    last = jnp.broadcast_to(fc[:, n_valid - 1:n_valid], (N_HEADS, LANES))
    fcarry[...] = last
    ftail_ref[...] = last

    pieces = jnp.concatenate(
        _split3(fc) + (jnp.ones((8, tm), F32), jnp.zeros((LANES - 32, tm), F32)),
        axis=0).astype(BF16)
    fq = jnp.dot(selq_ref[...], pieces, preferred_element_type=F32)
    fk = jnp.dot(selk_ref[...], pieces, preferred_element_type=F32)

    qkv = jnp.dot(hn, wqkv_ref[...], preferred_element_type=F32)
    qt = qkv[:, 0:ATTN_WIDTH].T
    kt = qkv[:, ATTN_WIDTH:2 * ATTN_WIDTH].T
    vt_ref[...] = qkv[:, 2 * ATTN_WIDTH:3 * ATTN_WIDTH].T.astype(BF16)

    def head_norm(src, gain_ref, h):
        xh = src[h * HEAD_DIM:(h + 1) * HEAD_DIM, :]
        ssq = jnp.mean(xh * xh, axis=0, keepdims=True)
        return xh * lax.rsqrt(ssq + RMS_EPS) * gain_ref[...]

    k_pad = jnp.zeros((LANES - HEAD_ROWS, tm), F32)
    for h in range(N_HEADS):
        aug = slice(h * AUG_ROWS, (h + 1) * AUG_ROWS)
        r0 = h * HEAD_ROWS
        qta_ref[r0:r0 + HEAD_DIM, :] = head_norm(qt, gq_ref, h).astype(BF16)
        qta_ref[r0 + HEAD_DIM:r0 + HEAD_ROWS, :] = fq[aug, :].astype(BF16)
        k_blk = jnp.concatenate([head_norm(kt, gk_ref, h), fk[aug, :], k_pad], axis=0)
        ka_ref[:, h * LANES:(h + 1) * LANES] = k_blk.T.astype(BF16)

    u = jnp.dot(hn, wp_ref[...], preferred_element_type=F32)
    ubuf[MAX_WINDOW:MAX_WINDOW + tm, :] = u
    mixed = []
    for g, w in enumerate(POOL_WINDOWS):
        c0 = g * POOL_GROUP_DIM
        acc = u[:, c0:c0 + POOL_GROUP_DIM]
        for s in range(1, w):
            acc = acc + ubuf[MAX_WINDOW - s:MAX_WINDOW - s + tm, c0:c0 + POOL_GROUP_DIM]
        if first_pos + 1 >= w:
            mean = acc * (1.0 / w)
        else:
            pos = first_pos + j * tm + lax.broadcasted_iota(jnp.int32, (tm, POOL_GROUP_DIM), 0)
            mean = acc / jnp.minimum(pos + 1, w).astype(F32)
        mixed.append(mean - u[:, c0:c0 + POOL_GROUP_DIM])
    mixed = jnp.concatenate(mixed, axis=1).astype(BF16)
    y = jnp.dot(mixed, wpool_ref[...], preferred_element_type=F32) * pscale_ref[...]
    pooled_ref[...] = y.astype(BF16)
    tail = u[n_valid - MAX_WINDOW:n_valid, :]
    ubuf[0:MAX_WINDOW, :] = tail
    utail_ref[...] = tail

    gl = jnp.dot(hn, wg_ref[...], preferred_element_type=F32)
    gates_ref[...] = jax.nn.sigmoid(gl).astype(BF16)


def _inproj(x3, p, uprev, fprev, *, tm, n_valid, first_pos):
    nb, seq, _ = x3.shape
    nt = seq // tm
    gq = jnp.broadcast_to((p["gq"] * (LOG2_E / math.sqrt(HEAD_DIM)))[:, None], (HEAD_DIM, tm))
    gk = jnp.broadcast_to(p["gk"][:, None], (HEAD_DIM, tm))
    tok = lambda width: pl.BlockSpec((None, tm, width), lambda b, j: (b, j, 0))
    chan = lambda rows: pl.BlockSpec((None, rows, tm), lambda b, j: (b, 0, j))
    kern = functools.partial(_inproj_kernel, tm=tm, n_valid=n_valid, first_pos=first_pos)
    return pl.pallas_call(
        kern,
        grid=(nb, nt),
        in_specs=[
            tok(D_MODEL),
            _const_spec((1, D_MODEL)),
            _const_spec((D_MODEL, 3 * ATTN_WIDTH)),
            _const_spec((D_MODEL, LANES)),
            _const_spec((1, LANES)),
            _const_spec((D_MODEL, POOL_WIDTH)),
            _const_spec((D_MODEL, 2 * D_MODEL)),
            _const_spec((HEAD_DIM, tm)),
            _const_spec((HEAD_DIM, tm)),
            _const_spec((LANES, LANES)),
            _const_spec((LANES, LANES)),
            _const_spec((POOL_WIDTH, POOL_WIDTH)),
            _const_spec((1, POOL_WIDTH)),
            _const_spec((MAX_WINDOW, POOL_WIDTH)),
            _const_spec((N_HEADS, LANES)),
        ],
        out_specs=[
            chan(QK_ROWS), tok(N_HEADS * LANES), chan(ATTN_WIDTH),
            tok(POOL_WIDTH), tok(2 * D_MODEL),
            pl.BlockSpec((MAX_WINDOW, POOL_WIDTH), lambda b, j: (0, 0)),
            pl.BlockSpec((N_HEADS, LANES), lambda b, j: (0, 0)),
        ],
        out_shape=[
            jax.ShapeDtypeStruct((nb, QK_ROWS, seq), BF16),
            jax.ShapeDtypeStruct((nb, seq, N_HEADS * LANES), BF16),
            jax.ShapeDtypeStruct((nb, ATTN_WIDTH, seq), BF16),
            jax.ShapeDtypeStruct((nb, seq, POOL_WIDTH), BF16),
            jax.ShapeDtypeStruct((nb, seq, 2 * D_MODEL), BF16),
            jax.ShapeDtypeStruct((MAX_WINDOW, POOL_WIDTH), F32),
            jax.ShapeDtypeStruct((N_HEADS, LANES), F32),
        ],
        scratch_shapes=[
            pltpu.VMEM((MAX_WINDOW + tm, POOL_WIDTH), F32),
            pltpu.VMEM((N_HEADS, LANES), F32),
        ],
        compiler_params=pltpu.CompilerParams(
            dimension_semantics=("arbitrary", "arbitrary"), vmem_limit_bytes=VMEM_LIMIT),
        name="inproj",
    )(x3, p["g_mix"], p["wqkv"], p["wf"], p["bf"], p["wp"], p["wg"], gq, gk,
      p["selq"], p["selk"], p["wpool"], p["pscale"], uprev, fprev)


def _attn_kernel(qta_ref, ka_ref, vt_ref, kam_ref, vtm_ref, bias_ref, ot_ref, m_sc, acc_sc,
                 *, tq):
    qi = pl.program_id(1)
    m_sc[...] = jnp.full(m_sc.shape, -jnp.inf, F32)
    acc_sc[...] = jnp.zeros(acc_sc.shape, F32)

    def rows(h):
        return slice(h * HEAD_ROWS, (h + 1) * HEAD_ROWS)

    def vrows(h):
        return slice(h * HEAD_DIM, (h + 1) * HEAD_DIM)

    def klanes(h):
        return slice(h * LANES, (h + 1) * LANES)

    q_pad = jnp.zeros((LANES - HEAD_ROWS, tq), BF16)

    def scores(h, ka):
        qa = jnp.concatenate([qta_ref[rows(h), :], q_pad], axis=0)
        return jnp.dot(ka, qa, preferred_element_type=F32)

    def new_max(h, s):
        return jnp.maximum(m_sc[h], jnp.max(s, axis=0, keepdims=True))

    def absorb(h, s, m_new, v):
        p = jnp.exp2(s - m_new).astype(BF16)
        va = jnp.concatenate([v, jnp.ones((AUG_ROWS, v.shape[1]), BF16)], axis=0)
        acc_sc[h] = (jnp.exp2(m_sc[h] - m_new) * acc_sc[h]
                     + jnp.dot(va, p, preferred_element_type=F32))
        m_sc[h] = m_new

    def sweep_heads(scores_of, values_of):
        s = {0: scores_of(0), 1: scores_of(1)}
        m = {0: new_max(0, s[0])}
        for h in range(N_HEADS):
            if h + 2 < N_HEADS:
                s[h + 2] = scores_of(h + 2)
            if h + 1 < N_HEADS:
                m[h + 1] = new_max(h + 1, s[h + 1])
            absorb(h, s.pop(h), m.pop(h), values_of(h))

    def kv_body(i, carry):
        s0 = pl.multiple_of(i * tq, tq)
        sweep_heads(lambda h: scores(h, ka_ref[pl.ds(s0, tq), klanes(h)]),
                    lambda h: vt_ref[vrows(h), pl.ds(s0, tq)])
        return carry

    lax.fori_loop(0, qi, kv_body, 0)

    s0 = pl.multiple_of(qi * tq, tq)
    sweep_heads(
        lambda h: scores(h, jnp.concatenate(
            [ka_ref[pl.ds(s0, tq), klanes(h)], kam_ref[:, klanes(h)]], axis=0)) + bias_ref[...],
        lambda h: jnp.concatenate([vt_ref[vrows(h), pl.ds(s0, tq)], vtm_ref[vrows(h), :]], axis=1))
    for h in range(N_HEADS):
        acc = acc_sc[h]
        ot_ref[vrows(h), :] = (acc[0:HEAD_DIM, :] / acc[HEAD_DIM:HEAD_DIM + 1, :]).astype(BF16)


def _attention(qta, ka, vt, ka_m, vt_m, *, tq):
    nb, _, seq = qta.shape
    kw = N_HEADS * LANES
    key = np.arange(tq + LANES)[:, None]
    qry = np.arange(tq)[None, :]
    visible = np.where(key < tq, key <= qry, key - tq < N_META)
    bias = jnp.asarray(np.where(visible, 0.0, -np.inf), F32)
    return pl.pallas_call(
        functools.partial(_attn_kernel, tq=tq),
        grid=(nb, seq // tq),
        in_specs=[
            pl.BlockSpec((None, QK_ROWS, tq), lambda b, i: (b, 0, i)),
            pl.BlockSpec((None, seq, kw), lambda b, i: (b, 0, 0)),
            pl.BlockSpec((None, ATTN_WIDTH, seq), lambda b, i: (b, 0, 0)),
            pl.BlockSpec((None, LANES, kw), lambda b, i: (0, 0, 0)),
            pl.BlockSpec((None, ATTN_WIDTH, LANES), lambda b, i: (0, 0, 0)),
            _const_spec((tq + LANES, tq)),
        ],
        out_specs=pl.BlockSpec((None, ATTN_WIDTH, tq), lambda b, i: (b, 0, i)),
        out_shape=jax.ShapeDtypeStruct((nb, ATTN_WIDTH, seq), BF16),
        scratch_shapes=[
            pltpu.VMEM((N_HEADS, 1, tq), F32),
            pltpu.VMEM((N_HEADS, HEAD_ROWS, tq), F32),
        ],
        compiler_params=pltpu.CompilerParams(
            dimension_semantics=("arbitrary", "arbitrary"), vmem_limit_bytes=VMEM_LIMIT),
        name="fox_attention",
    )(qta, ka, vt, ka_m, vt_m, bias)


def _post_kernel(ot_ref, pooled_ref, gates_ref, x_ref, wua_ref, wup_ref, wout_ref, g_ref,
                 wr_ref, br_ref, h2_ref, plan_ref, cnt_ref, cnt_sc, *, tm):
    @pl.when((pl.program_id(0) == 0) & (pl.program_id(1) == 0))
    def _():
        cnt_sc[...] = jnp.zeros(cnt_sc.shape, F32)

    y_attn = lax.dot_general(ot_ref[...], wua_ref[...], _TN, preferred_element_type=F32)
    y_pool = jnp.dot(pooled_ref[...], wup_ref[...], preferred_element_type=F32)
    merged = (gates_ref[:, 0:D_MODEL].astype(F32) * y_attn
              + gates_ref[:, D_MODEL:2 * D_MODEL].astype(F32) * y_pool)
    h2 = x_ref[...] + jnp.dot(merged.astype(BF16), wout_ref[...], preferred_element_type=F32)
    h2_ref[...] = h2
    ms = jnp.mean(h2 * h2, axis=-1, keepdims=True)
    hn2 = (h2 * lax.rsqrt(ms + RMS_EPS) * g_ref[...]).astype(BF16)

    logits = jnp.dot(hn2, wr_ref[...], preferred_element_type=F32) + br_ref[...]
    lt = logits.T[0:ROUTER_ROWS, :]
    row = lax.broadcasted_iota(jnp.int32, lt.shape, 0)
    neg_inf = jnp.float32(-jnp.inf)

    def softmax_over(mask):
        z = jnp.where(mask, lt, neg_inf)
        e = jnp.exp(z - jnp.max(z, axis=0, keepdims=True))
        return jnp.where(mask, e / jnp.sum(e, axis=0, keepdims=True), -1.0)

    def argtop(prob):
        top = jnp.max(prob, axis=0, keepdims=True)
        return jnp.min(jnp.where(prob == top, row, LANES), axis=0, keepdims=True)

    g_idx = argtop(softmax_over(row < N_GROUPS))
    e_lo = ROUTER_LANE0 + N_PER_GROUP * g_idx
    e_prob = softmax_over((row >= e_lo) & (row < e_lo + N_PER_GROUP))
    i1 = argtop(e_prob)
    i2 = argtop(jnp.where(row == i1, -1.0, e_prob))
    a = jnp.minimum(i1, i2) - e_lo
    b = jnp.maximum(i1, i2) - e_lo
    pair = ((a * (2 * N_PER_GROUP - 1 - a)) >> 1) + (b - a - 1)
    cls = g_idx * PAIRS_PER_GROUP + pair

    crow = lax.broadcasted_iota(jnp.int32, (LANES, tm), 0)
    member = crow == cls
    onehot = jnp.where(member, 1.0, 0.0)
    r_i = lax.broadcasted_iota(jnp.int32, (tm, tm), 0)
    c_i = lax.broadcasted_iota(jnp.int32, (tm, tm), 1)
    earlier = jnp.where(r_i < c_i, 1.0, 0.0).astype(BF16)
    prefix = jnp.dot(onehot.astype(BF16), earlier, preferred_element_type=F32)
    base = jnp.concatenate([cnt_sc[...]] * (tm // LANES), axis=1)
    rank = jnp.sum(jnp.where(member, prefix + base, 0.0), axis=0, keepdims=True)
    total = cnt_sc[...] + jnp.broadcast_to(jnp.sum(onehot, axis=1, keepdims=True), cnt_sc.shape)
    cnt_sc[...] = total
    cnt_ref[...] = total

    prow = lax.broadcasted_iota(jnp.int32, (8, tm), 0)
    plan_ref[...] = jnp.where(prow == PLAN_CLASS, cls.astype(F32),
                              jnp.where(prow == PLAN_RANK, rank, 0.0))


def _post(ot, pooled, gates, x, p, *, tm):
    nb, seq, _ = x.shape
    nt = seq // tm
    tok = lambda width: pl.BlockSpec((None, tm, width), lambda b, j: (b, j, 0))
    return pl.pallas_call(
        functools.partial(_post_kernel, tm=tm),
        grid=(nb, nt),
        in_specs=[
            pl.BlockSpec((None, ATTN_WIDTH, tm), lambda b, j: (b, 0, j)),
            tok(POOL_WIDTH), tok(2 * D_MODEL), tok(D_MODEL),
            _const_spec((ATTN_WIDTH, D_MODEL)),
            _const_spec((POOL_WIDTH, D_MODEL)),
            _const_spec((D_MODEL, D_MODEL)),
            _const_spec((1, D_MODEL)),
            _const_spec((D_MODEL, LANES)),
            _const_spec((1, LANES)),
        ],
        out_specs=[tok(D_MODEL),
                   pl.BlockSpec((None, 8, tm), lambda b, j: (b * nt + j, 0, 0)),
                   pl.BlockSpec((LANES, LANES), lambda b, j: (0, 0))],
        out_shape=[
            jax.ShapeDtypeStruct((nb, seq, D_MODEL), F32),
            jax.ShapeDtypeStruct((nb * nt, 8, tm), F32),
            jax.ShapeDtypeStruct((LANES, LANES), F32),
        ],
        scratch_shapes=[pltpu.VMEM((LANES, LANES), F32)],
        compiler_params=pltpu.CompilerParams(
            dimension_semantics=("arbitrary", "arbitrary"), vmem_limit_bytes=VMEM_LIMIT),
        name="post_router",
    )(ot, pooled, gates, x, p["wua"], p["wup"], p["wout"], p["g_ffn"], p["wr"], p["br"])


def _to_slots(dst, values):
    n = values.shape[0]
    for c in range(SLOT_ROWS):
        dst[pl.ds(c, n, stride=SLOT_ROWS), :] = values[:, c * LANES:(c + 1) * LANES]


def _from_slots(src, n):
    return jnp.concatenate(
        [src[pl.ds(c, n, stride=SLOT_ROWS), :] for c in range(SLOT_ROWS)], axis=1)


def _dispatch_kernel(pos_ref, h2_ref, xs_hbm, buf, zeros, sem, zsem, *, tm, n_slots):
    j = pl.program_id(0)
    slot = j % 2

    def drain(s):
        pltpu.make_async_copy(buf.at[s], xs_hbm.at[pl.ds(0, tm * SLOT_ROWS), :], sem.at[s]).wait()

    @pl.when(j >= 2)
    def _():
        drain(slot)

    _to_slots(buf.at[slot], h2_ref[...])

    def body(g, c):
        for u in range(DMA_UNROLL):
            r = g * DMA_UNROLL + u
            src = pl.multiple_of(r * SLOT_ROWS, SLOT_ROWS)
            dst = pl.multiple_of(pos_ref[0, r] * SLOT_ROWS, SLOT_ROWS)
            pltpu.make_async_copy(buf.at[slot, pl.ds(src, SLOT_ROWS), :],
                                  xs_hbm.at[pl.ds(dst, SLOT_ROWS), :],
                                  sem.at[slot]).start(priority=u % DMA_QUEUES)
        return c
    lax.fori_loop(0, tm // DMA_UNROLL, body, 0)

    @pl.when(j == pl.num_programs(0) - 1)
    def _():
        zeros[...] = jnp.zeros(zeros.shape, F32)
        pad = pltpu.make_async_copy(
            zeros, xs_hbm.at[pl.ds(n_slots * SLOT_ROWS, MOE_ROWS * SLOT_ROWS), :], zsem.at[0])
        pad.start()
        drain(slot)
        drain(1 - slot)
        pad.wait()


def _dispatch(h2, pos, *, tm):
    t = h2.shape[0]
    nt = t // tm
    return pl.pallas_call(
        functools.partial(_dispatch_kernel, tm=tm, n_slots=t),
        grid=(nt,),
        in_specs=[
            pl.BlockSpec((None, 1, tm), lambda j: (j, 0, 0), memory_space=pltpu.SMEM),
            pl.BlockSpec((tm, D_MODEL), lambda j: (j, 0)),
        ],
        out_specs=pl.BlockSpec(memory_space=pl.ANY),
        out_shape=jax.ShapeDtypeStruct(((t + MOE_ROWS) * SLOT_ROWS, LANES), F32),
        scratch_shapes=[
            pltpu.VMEM((2, tm * SLOT_ROWS, LANES), F32),
            pltpu.VMEM((MOE_ROWS * SLOT_ROWS, LANES), F32),
            pltpu.SemaphoreType.DMA((2,)),
            pltpu.SemaphoreType.DMA((1,)),
        ],
        compiler_params=pltpu.CompilerParams(
            dimension_semantics=("arbitrary",), vmem_limit_bytes=VMEM_LIMIT),
        name="moe_dispatch",
    )(pos.reshape(nt, 1, tm), h2)


def _moe_kernel(start_ref, nrows_ref, ea_ref, eb_ref, nitems_ref,
                g_ref, wr_ref, br_ref, w13a_ref, w13b_ref, w2a_ref, w2b_ref, xs_hbm,
                ys_hbm, xbuf, ybuf, isem, osem):
    k = pl.program_id(0)
    n_items = nitems_ref[0]
    slot = k % 2
    rows = MOE_ROWS * SLOT_ROWS

    def window(kk):
        return pl.ds(pl.multiple_of(start_ref[kk] * SLOT_ROWS, SLOT_ROWS), rows)

    def in_copy(kk, s):
        return pltpu.make_async_copy(xs_hbm.at[window(kk), :], xbuf.at[s], isem.at[s])

    def out_copy(kk, s):
        return pltpu.make_async_copy(ybuf.at[s], ys_hbm.at[window(kk), :], osem.at[s])

    @pl.when(k == 0)
    def _():
        in_copy(0, 0).start()
        ybuf[1] = jnp.zeros(ybuf.shape[1:], F32)
        pad = pltpu.make_async_copy(
            ybuf.at[1], ys_hbm.at[pl.ds(ys_hbm.shape[0] - rows, rows), :], osem.at[1])
        pad.start()
        pad.wait()

    @pl.when(k + 1 < n_items)
    def _():
        in_copy(k + 1, 1 - slot).start()

    @pl.when(k < n_items)
    def _():
        in_copy(k, slot).wait()
        h2 = _from_slots(xbuf.at[slot], MOE_ROWS)
        ms = jnp.mean(h2 * h2, axis=-1, keepdims=True)
        xn = (h2 * lax.rsqrt(ms + RMS_EPS) * g_ref[...]).astype(BF16)

        logits = jnp.dot(xn, wr_ref[...], preferred_element_type=F32) + br_ref[...]
        lane = lax.broadcasted_iota(jnp.int32, logits.shape, 1)
        pick = lambda idx: jnp.sum(jnp.where(lane == idx, logits, 0.0), axis=1, keepdims=True)
        ea, eb = ea_ref[k], eb_ref[k]
        gl = jnp.where(lane < N_GROUPS, logits, -jnp.inf)
        gmax = jnp.max(gl, axis=1, keepdims=True)
        p_group = (jnp.exp(pick(ea // N_PER_GROUP) - gmax)
                   / jnp.sum(jnp.exp(gl - gmax), axis=1, keepdims=True))
        la, lb = pick(ROUTER_LANE0 + ea), pick(ROUTER_LANE0 + eb)
        valid = lax.broadcasted_iota(jnp.int32, (MOE_ROWS, 1), 0) < nrows_ref[k]
        w_lo = jnp.where(valid, p_group / (1.0 + jnp.exp(lb - la)), 0.0)
        w_hi = jnp.where(valid, p_group / (1.0 + jnp.exp(la - lb)), 0.0)

        def expert(w13_ref, w2_ref, weight):
            h = jnp.dot(xn, w13_ref[...], preferred_element_type=F32)
            h1 = h[:, 0:EXPERT_HIDDEN]
            h3 = h[:, EXPERT_HIDDEN:2 * EXPERT_HIDDEN]
            hh = h1 * jax.nn.sigmoid(h1) * h3 * weight
            return jnp.dot(hh.astype(BF16), w2_ref[...], preferred_element_type=F32)

        y = h2 + (expert(w13a_ref, w2a_ref, w_lo) + expert(w13b_ref, w2b_ref, w_hi))

        @pl.when(k >= 1)
        def _():
            out_copy(k - 1, 1 - slot).wait()

        _to_slots(ybuf.at[slot], y)
        out_copy(k, slot).start()

        @pl.when(k == n_items - 1)
        def _():
            out_copy(k, slot).wait()


def _class_experts():
    ea, eb = [], []
    for g in range(N_GROUPS):
        for a in range(N_PER_GROUP):
            for b in range(a + 1, N_PER_GROUP):
                ea.append(g * N_PER_GROUP + a)
                eb.append(g * N_PER_GROUP + b)
    fill = LANES - len(ea)
    return (jnp.array(ea + [ea[-1]] * fill, jnp.int32),
            jnp.array(eb + [eb[-1]] * fill, jnp.int32))


def _combine_kernel(pos_ref, pos_next_ref, ys_hbm, out_ref, buf, sem, *, tm):
    j = pl.program_id(0)
    slot = j % 2

    def start_tile(pos, s):
        def body(g, c):
            for u in range(DMA_UNROLL):
                r = g * DMA_UNROLL + u
                src = pl.multiple_of(pos[0, r] * SLOT_ROWS, SLOT_ROWS)
                dst = pl.multiple_of(r * SLOT_ROWS, SLOT_ROWS)
                pltpu.make_async_copy(ys_hbm.at[pl.ds(src, SLOT_ROWS), :],
                                      buf.at[s, pl.ds(dst, SLOT_ROWS), :],
                                      sem.at[s]).start(priority=u % DMA_QUEUES)
            return c
        lax.fori_loop(0, tm // DMA_UNROLL, body, 0)

    @pl.when(j == 0)
    def _():
        start_tile(pos_ref, 0)

    @pl.when(j + 1 < pl.num_programs(0))
    def _():
        start_tile(pos_next_ref, 1 - slot)

    pltpu.make_async_copy(ys_hbm.at[pl.ds(0, tm * SLOT_ROWS), :], buf.at[slot], sem.at[slot]).wait()
    out_ref[...] = _from_slots(buf.at[slot], tm)


def _combine(ys, pos, *, tm):
    t = pos.shape[0]
    nt = t // tm
    pos3 = pos.reshape(nt, 1, tm)
    smem_tile = lambda fn: pl.BlockSpec((None, 1, tm), fn, memory_space=pltpu.SMEM)
    return pl.pallas_call(
        functools.partial(_combine_kernel, tm=tm),
        grid=(nt,),
        in_specs=[
            smem_tile(lambda j: (j, 0, 0)),
            smem_tile(lambda j: (jnp.minimum(j + 1, nt - 1), 0, 0)),
            pl.BlockSpec(memory_space=pl.ANY),
        ],
        out_specs=pl.BlockSpec((tm, D_MODEL), lambda j: (j, 0)),
        out_shape=jax.ShapeDtypeStruct((t, D_MODEL), F32),
        scratch_shapes=[
            pltpu.VMEM((2, tm * SLOT_ROWS, LANES), F32),
            pltpu.SemaphoreType.DMA((2,)),
        ],
        compiler_params=pltpu.CompilerParams(
            dimension_semantics=("arbitrary",), vmem_limit_bytes=VMEM_LIMIT),
        name="moe_combine",
    )(pos3, pos3, ys)


def _moe(h2, plan, counts, p, *, tm):
    t = h2.shape[0]
    assert t % tm == 0 and t // tm >= 2
    max_items = -(-t // MOE_ROWS) + N_CLASSES
    class_ids = jnp.arange(LANES, dtype=jnp.int32)
    c_end = jnp.cumsum(counts)
    c_start = c_end - counts
    cls = plan[:, PLAN_CLASS, :].reshape(t).astype(jnp.int32)
    rank = plan[:, PLAN_RANK, :].reshape(t).astype(jnp.int32)
    pos = jnp.sum(jnp.where(cls[:, None] == class_ids[None, :], c_start[None, :], 0),
                  axis=1) + rank
    n_items_c = (counts + MOE_ROWS - 1) // MOE_ROWS
    item_end = jnp.cumsum(n_items_c)
    item_start = item_end - n_items_c
    n_items = item_end[-1]
    item_ids = jnp.minimum(jnp.arange(max_items, dtype=jnp.int32), n_items - 1)
    icls = jnp.sum(item_end[None, :] <= item_ids[:, None], axis=1, dtype=jnp.int32)
    pick = lambda table: jnp.sum(
        jnp.where(icls[:, None] == class_ids[None, :], table[None, :], 0), axis=1)
    offset = (item_ids - pick(item_start)) * MOE_ROWS
    start = pick(c_start) + offset
    nrows = jnp.minimum(pick(counts) - offset, MOE_ROWS)
    ea_tab, eb_tab = _class_experts()
    ea, eb = pick(ea_tab), pick(eb_tab)

    xs = _dispatch(h2, pos, tm=tm)

    first = lambda ea, eb: ea
    second = lambda ea, eb: eb
    w13 = lambda sel: pl.BlockSpec((None, D_MODEL, 2 * EXPERT_HIDDEN),
                                   lambda k, st, nr, ea, eb, n: (sel(ea, eb)[k], 0, 0))
    w2 = lambda sel: pl.BlockSpec((None, EXPERT_HIDDEN, D_MODEL),
                                  lambda k, st, nr, ea, eb, n: (sel(ea, eb)[k], 0, 0))
    const = lambda shape: pl.BlockSpec(shape, lambda k, *_: (0, 0))
    ys = pl.pallas_call(
        _moe_kernel,
        grid_spec=pltpu.PrefetchScalarGridSpec(
            num_scalar_prefetch=5,
            grid=(max_items,),
            in_specs=[
                const((1, D_MODEL)), const((D_MODEL, LANES)), const((1, LANES)),
                w13(first), w13(second), w2(first), w2(second),
                pl.BlockSpec(memory_space=pl.ANY),
            ],
            out_specs=pl.BlockSpec(memory_space=pl.ANY),
            scratch_shapes=[
                pltpu.VMEM((2, MOE_ROWS * SLOT_ROWS, LANES), F32),
                pltpu.VMEM((2, MOE_ROWS * SLOT_ROWS, LANES), F32),
                pltpu.SemaphoreType.DMA((2,)),
                pltpu.SemaphoreType.DMA((2,)),
            ],
        ),
        out_shape=jax.ShapeDtypeStruct(((t + MOE_ROWS) * SLOT_ROWS, LANES), F32),
        compiler_params=pltpu.CompilerParams(
            dimension_semantics=("arbitrary",), vmem_limit_bytes=VMEM_LIMIT),
        name="moe",
    )(start, nrows, ea, eb, n_items.reshape(1), p["g_ffn"], p["wr"], p["br"],
      p["w13"], p["w13"], p["w2"], p["w2"], xs)
    return _combine(ys, pos, tm=tm)


def _bias_selectors():
    selq = [[0.0] * LANES for _ in range(LANES)]
    selk = [[0.0] * LANES for _ in range(LANES)]
    ones_row = 3 * N_HEADS
    for h in range(N_HEADS):
        for part in range(3):
            selq[h * AUG_ROWS + part][part * N_HEADS + h] = 1.0
            selq[h * AUG_ROWS + 3 + part][ones_row] = 1.0
            selk[h * AUG_ROWS + part][ones_row] = 1.0
            selk[h * AUG_ROWS + 3 + part][part * N_HEADS + h] = -1.0
    return jnp.array(selq, BF16), jnp.array(selk, BF16)


def _prepare_params(norm_mix_g, w_in, b_forget, q_norm_g, k_norm_g, w_up_attn, w_pool,
                    pool_scale, w_up_pool, w_out, norm_ffn_g, w_group, b_group, w_router,
                    b_router, w1, w3, w2):
    aw, pw = ATTN_WIDTH, POOL_WIDTH
    f_off = 3 * aw
    p_off = f_off + N_HEADS
    g_off = p_off + pw
    selq, selk = _bias_selectors()
    pad_lanes = lambda a: jnp.pad(a, ((0, 0), (0, LANES - a.shape[1])))
    wpool = jnp.zeros((pw, pw), F32)
    for g in range(len(POOL_WINDOWS)):
        sl = slice(g * POOL_GROUP_DIM, (g + 1) * POOL_GROUP_DIM)
        wpool = wpool.at[sl, sl].set(w_pool[g])
    d, f = D_MODEL, EXPERT_HIDDEN
    return {
        "g_mix": norm_mix_g.reshape(1, d),
        "wqkv": w_in[:, 0:f_off].astype(BF16),
        "wf": pad_lanes(w_in[:, f_off:p_off]).astype(BF16),
        "bf": pad_lanes(b_forget.reshape(1, N_HEADS)),
        "wp": w_in[:, p_off:g_off].astype(BF16),
        "wg": w_in[:, g_off:].astype(BF16),
        "gq": q_norm_g, "gk": k_norm_g,
        "selq": selq, "selk": selk,
        "wpool": wpool.astype(BF16),
        "pscale": pool_scale.reshape(1, pw),
        "wua": w_up_attn.astype(BF16),
        "wup": w_up_pool.astype(BF16),
        "wout": w_out.astype(BF16),
        "g_ffn": norm_ffn_g.reshape(1, d),
        "wr": pad_lanes(jnp.concatenate([w_group, w_router], axis=1)).astype(BF16),
        "br": pad_lanes(jnp.concatenate([b_group, b_router]).reshape(1, -1)),
        "w13": jnp.concatenate([w1, w3], axis=-1).reshape(N_EXPERTS, d, 2 * f).astype(BF16),
        "w2": w2.reshape(N_EXPERTS, f, d).astype(BF16),
    }


def kernel(x, meta_tokens, norm_mix_g, w_in, b_forget, q_norm_g, k_norm_g, w_up_attn, w_pool,
           pool_scale, w_up_pool, w_out, norm_ffn_g, w_group, b_group, w_router, b_router,
           w1, w3, w2):
    nb, seq, d = x.shape
    p = _prepare_params(norm_mix_g[0], w_in[0], b_forget[0], q_norm_g[0], k_norm_g[0],
                        w_up_attn[0], w_pool[0], pool_scale[0], w_up_pool[0], w_out[0],
                        norm_ffn_g[0], w_group[0], b_group[0], w_router[0], b_router[0],
                        w1[0], w3[0], w2[0])

    meta = jnp.pad(meta_tokens.astype(x.dtype), ((0, LANES - N_META), (0, 0)))[None]
    _, ka_m, vt_m, _, _, u_meta, f_meta = _inproj(
        meta, p, jnp.zeros((MAX_WINDOW, POOL_WIDTH), F32), jnp.zeros((N_HEADS, LANES), F32),
        tm=LANES, n_valid=N_META, first_pos=0)

    tm = 512
    qta, ka, vt, pooled, gates, _, _ = _inproj(
        x, p, u_meta, f_meta, tm=tm, n_valid=tm, first_pos=N_META)
    ot = _attention(qta, ka, vt, ka_m, vt_m, tq=512)
    h2, plan, cnt = _post(ot, pooled, gates, x, p, tm=512)
    out = _moe(h2.reshape(nb * seq, d), plan, cnt[:, 0].astype(jnp.int32), p, tm=512)
    return out.reshape(nb, seq, d)
```

```python
import functools
import math

import jax
import jax.numpy as jnp
import numpy as np
from jax import lax
from jax.experimental import pallas as pl
from jax.experimental.pallas import tpu as pltpu

F32 = jnp.float32
BF16 = jnp.bfloat16

D_MODEL = 1024
N_META = 16
N_HEADS = 8
HEAD_DIM = 64
ATTN_WIDTH = N_HEADS * HEAD_DIM
POOL_WINDOWS = (2, 4, 8, 16)
POOL_WIDTH = 512
POOL_GROUP_DIM = POOL_WIDTH // len(POOL_WINDOWS)
MAX_WINDOW = max(POOL_WINDOWS)
N_GROUPS = 4
N_PER_GROUP = 8
N_EXPERTS = N_GROUPS * N_PER_GROUP
EXPERT_HIDDEN = 256
RMS_EPS = 1e-6
LOG2_E = math.log2(math.e)

LANES = 128
AUG_ROWS = 16
HEAD_ROWS = HEAD_DIM + AUG_ROWS
QK_ROWS = N_HEADS * HEAD_ROWS
ROUTER_LANE0 = N_GROUPS
PAIRS_PER_GROUP = N_PER_GROUP * (N_PER_GROUP - 1) // 2
N_CLASSES = N_GROUPS * PAIRS_PER_GROUP
ROUTER_ROWS = 40
PLAN_CLASS, PLAN_RANK = 0, 1
MOE_ROWS = 352
SLOT_ROWS = D_MODEL // LANES
DMA_UNROLL = 8
DMA_QUEUES = 2
MOE_CHAINS = 2
VMEM_LIMIT = 56 * 1024 * 1024

_TN = (((0,), (0,)), ((), ()))


def _const_spec(shape):
    zeros = (0,) * len(shape)
    return pl.BlockSpec(shape, lambda *_: zeros, pipeline_mode=pl.Buffered(1))


def _split3(x):
    hi = x.astype(BF16).astype(F32)
    r = x - hi
    mid = r.astype(BF16).astype(F32)
    lo = (r - mid).astype(BF16).astype(F32)
    return hi, mid, lo


def _inproj_kernel(x_ref, g_ref, wqkv_ref, wf_ref, bf_ref, wp_ref, wg_ref, gq_ref, gk_ref,
                   selq_ref, selk_ref, wpool_ref, pscale_ref, uprev_ref, fprev_ref,
                   qta_ref, ka_ref, vt_ref, pooled_ref, gates_ref, utail_ref, ftail_ref,
                   ubuf, fcarry, *, tm, n_valid, first_pos):
    j = pl.program_id(1)

    @pl.when(j == 0)
    def _():
        ubuf[0:MAX_WINDOW, :] = uprev_ref[...]
        fcarry[...] = fprev_ref[...]

    x = x_ref[...]
    ms = jnp.mean(x * x, axis=-1, keepdims=True)
    hn = (x * lax.rsqrt(ms + RMS_EPS) * g_ref[...]).astype(BF16)

    f = jnp.dot(hn, wf_ref[...], preferred_element_type=F32) + bf_ref[...]
    z = f.T[0:N_HEADS, :]
    logf = (jnp.minimum(z, 0.0) - jnp.log1p(jnp.exp(-jnp.abs(z)))) * LOG2_E
    parts = jnp.concatenate(_split3(logf), axis=0).astype(BF16)
    r_i = lax.broadcasted_iota(jnp.int32, (tm, tm), 0)
    c_i = lax.broadcasted_iota(jnp.int32, (tm, tm), 1)
    tri = jnp.where(r_i <= c_i, 1.0, 0.0).astype(BF16)
    cs3 = jnp.dot(parts, tri, preferred_element_type=F32)
    cs = cs3[0:8] + cs3[8:16] + cs3[16:24]
    fc = jnp.concatenate([fcarry[...]] * (tm // LANES), axis=1) + cs
    last = jnp.broadcast_to(fc[:, n_valid - 1:n_valid], (N_HEADS, LANES))
    fcarry[...] = last
    ftail_ref[...] = last

    pieces = jnp.concatenate(
        _split3(fc) + (jnp.ones((8, tm), F32), jnp.zeros((LANES - 32, tm), F32)),
        axis=0).astype(BF16)
    fq = jnp.dot(selq_ref[...], pieces, preferred_element_type=F32)
    fk = jnp.dot(selk_ref[...], pieces, preferred_element_type=F32)

    qkv = jnp.dot(hn, wqkv_ref[...], preferred_element_type=F32)
    qt = qkv[:, 0:ATTN_WIDTH].T
    kt = qkv[:, ATTN_WIDTH:2 * ATTN_WIDTH].T
    vt_ref[...] = qkv[:, 2 * ATTN_WIDTH:3 * ATTN_WIDTH].T.astype(BF16)

    def head_norm(src, gain_ref, h):
        xh = src[h * HEAD_DIM:(h + 1) * HEAD_DIM, :]
        ssq = jnp.mean(xh * xh, axis=0, keepdims=True)
        return xh * lax.rsqrt(ssq + RMS_EPS) * gain_ref[...]

    k_pad = jnp.zeros((LANES - HEAD_ROWS, tm), F32)
    for h in range(N_HEADS):
        aug = slice(h * AUG_ROWS, (h + 1) * AUG_ROWS)
        r0 = h * HEAD_ROWS
        qta_ref[r0:r0 + HEAD_DIM, :] = head_norm(qt, gq_ref, h).astype(BF16)
        qta_ref[r0 + HEAD_DIM:r0 + HEAD_ROWS, :] = fq[aug, :].astype(BF16)
        k_blk = jnp.concatenate([head_norm(kt, gk_ref, h), fk[aug, :], k_pad], axis=0)
        ka_ref[:, h * LANES:(h + 1) * LANES] = k_blk.T.astype(BF16)

    u = jnp.dot(hn, wp_ref[...], preferred_element_type=F32)
    ubuf[MAX_WINDOW:MAX_WINDOW + tm, :] = u
    mixed = []
    for g, w in enumerate(POOL_WINDOWS):
        c0 = g * POOL_GROUP_DIM
        acc = u[:, c0:c0 + POOL_GROUP_DIM]
        for s in range(1, w):
            acc = acc + ubuf[MAX_WINDOW - s:MAX_WINDOW - s + tm, c0:c0 + POOL_GROUP_DIM]
        if first_pos + 1 >= w:
            mean = acc * (1.0 / w)
        else:
            pos = first_pos + j * tm + lax.broadcasted_iota(jnp.int32, (tm, POOL_GROUP_DIM), 0)
            mean = acc / jnp.minimum(pos + 1, w).astype(F32)
        mixed.append(mean - u[:, c0:c0 + POOL_GROUP_DIM])
    mixed = jnp.concatenate(mixed, axis=1).astype(BF16)
    y = jnp.dot(mixed, wpool_ref[...], preferred_element_type=F32) * pscale_ref[...]
    pooled_ref[...] = y.astype(BF16)
    tail = u[n_valid - MAX_WINDOW:n_valid, :]
    ubuf[0:MAX_WINDOW, :] = tail
    utail_ref[...] = tail

    gl = jnp.dot(hn, wg_ref[...], preferred_element_type=F32)
    gates_ref[...] = jax.nn.sigmoid(gl).astype(BF16)


def _inproj(x3, p, uprev, fprev, *, tm, n_valid, first_pos):
    nb, seq, _ = x3.shape
    nt = seq // tm
    gq = jnp.broadcast_to((p["gq"] * (LOG2_E / math.sqrt(HEAD_DIM)))[:, None], (HEAD_DIM, tm))
    gk = jnp.broadcast_to(p["gk"][:, None], (HEAD_DIM, tm))
    tok = lambda width: pl.BlockSpec((None, tm, width), lambda b, j: (b, j, 0))
    chan = lambda rows: pl.BlockSpec((None, rows, tm), lambda b, j: (b, 0, j))
    kern = functools.partial(_inproj_kernel, tm=tm, n_valid=n_valid, first_pos=first_pos)
    return pl.pallas_call(
        kern,
        grid=(nb, nt),
        in_specs=[
            tok(D_MODEL),
            _const_spec((1, D_MODEL)),
            _const_spec((D_MODEL, 3 * ATTN_WIDTH)),
            _const_spec((D_MODEL, LANES)),
            _const_spec((1, LANES)),
            _const_spec((D_MODEL, POOL_WIDTH)),
            _const_spec((D_MODEL, 2 * D_MODEL)),
            _const_spec((HEAD_DIM, tm)),
            _const_spec((HEAD_DIM, tm)),
            _const_spec((LANES, LANES)),
            _const_spec((LANES, LANES)),
            _const_spec((POOL_WIDTH, POOL_WIDTH)),
            _const_spec((1, POOL_WIDTH)),
            _const_spec((MAX_WINDOW, POOL_WIDTH)),
            _const_spec((N_HEADS, LANES)),
        ],
        out_specs=[
            chan(QK_ROWS), tok(N_HEADS * LANES), chan(ATTN_WIDTH),
            tok(POOL_WIDTH), tok(2 * D_MODEL),
            pl.BlockSpec((MAX_WINDOW, POOL_WIDTH), lambda b, j: (0, 0)),
            pl.BlockSpec((N_HEADS, LANES), lambda b, j: (0, 0)),
        ],
        out_shape=[
            jax.ShapeDtypeStruct((nb, QK_ROWS, seq), BF16),
            jax.ShapeDtypeStruct((nb, seq, N_HEADS * LANES), BF16),
            jax.ShapeDtypeStruct((nb, ATTN_WIDTH, seq), BF16),
            jax.ShapeDtypeStruct((nb, seq, POOL_WIDTH), BF16),
            jax.ShapeDtypeStruct((nb, seq, 2 * D_MODEL), BF16),
            jax.ShapeDtypeStruct((MAX_WINDOW, POOL_WIDTH), F32),
            jax.ShapeDtypeStruct((N_HEADS, LANES), F32),
        ],
        scratch_shapes=[
            pltpu.VMEM((MAX_WINDOW + tm, POOL_WIDTH), F32),
            pltpu.VMEM((N_HEADS, LANES), F32),
        ],
        compiler_params=pltpu.CompilerParams(
            dimension_semantics=("arbitrary", "arbitrary"), vmem_limit_bytes=VMEM_LIMIT),
        name="inproj",
    )(x3, p["g_mix"], p["wqkv"], p["wf"], p["bf"], p["wp"], p["wg"], gq, gk,
      p["selq"], p["selk"], p["wpool"], p["pscale"], uprev, fprev)


def _attn_kernel(qta_ref, ka_ref, vt_ref, kam_ref, vtm_ref, bias_ref, ot_ref, m_sc, acc_sc,
                 *, tq):
    qi = pl.program_id(1)
    m_sc[...] = jnp.full(m_sc.shape, -jnp.inf, F32)
    acc_sc[...] = jnp.zeros(acc_sc.shape, F32)

    def rows(h):
        return slice(h * HEAD_ROWS, (h + 1) * HEAD_ROWS)

    def vrows(h):
        return slice(h * HEAD_DIM, (h + 1) * HEAD_DIM)

    def klanes(h):
        return slice(h * LANES, (h + 1) * LANES)

    q_pad = jnp.zeros((LANES - HEAD_ROWS, tq), BF16)

    def scores(h, ka):
        qa = jnp.concatenate([qta_ref[rows(h), :], q_pad], axis=0)
        return jnp.dot(ka, qa, preferred_element_type=F32)

    def new_max(h, s):
        return jnp.maximum(m_sc[h], jnp.max(s, axis=0, keepdims=True))

    def absorb(h, s, m_new, v):
        p = jnp.exp2(s - m_new).astype(BF16)
        va = jnp.concatenate([v, jnp.ones((AUG_ROWS, v.shape[1]), BF16)], axis=0)
        acc_sc[h] = (jnp.exp2(m_sc[h] - m_new) * acc_sc[h]
                     + jnp.dot(va, p, preferred_element_type=F32))
        m_sc[h] = m_new

    def sweep_heads(scores_of, values_of):
        s = {0: scores_of(0), 1: scores_of(1)}
        m = {0: new_max(0, s[0])}
        for h in range(N_HEADS):
            if h + 2 < N_HEADS:
                s[h + 2] = scores_of(h + 2)
            if h + 1 < N_HEADS:
                m[h + 1] = new_max(h + 1, s[h + 1])
            absorb(h, s.pop(h), m.pop(h), values_of(h))

    def kv_body(i, carry):
        s0 = pl.multiple_of(i * tq, tq)
        sweep_heads(lambda h: scores(h, ka_ref[pl.ds(s0, tq), klanes(h)]),
                    lambda h: vt_ref[vrows(h), pl.ds(s0, tq)])
        return carry

    lax.fori_loop(0, qi, kv_body, 0)

    s0 = pl.multiple_of(qi * tq, tq)
    sweep_heads(
        lambda h: scores(h, jnp.concatenate(
            [ka_ref[pl.ds(s0, tq), klanes(h)], kam_ref[:, klanes(h)]], axis=0)) + bias_ref[...],
        lambda h: jnp.concatenate([vt_ref[vrows(h), pl.ds(s0, tq)], vtm_ref[vrows(h), :]], axis=1))
    for h in range(N_HEADS):
        acc = acc_sc[h]
        ot_ref[vrows(h), :] = (acc[0:HEAD_DIM, :] / acc[HEAD_DIM:HEAD_DIM + 1, :]).astype(BF16)


def _attention(qta, ka, vt, ka_m, vt_m, *, tq):
    nb, _, seq = qta.shape
    kw = N_HEADS * LANES
    key = np.arange(tq + LANES)[:, None]
    qry = np.arange(tq)[None, :]
    visible = np.where(key < tq, key <= qry, key - tq < N_META)
    bias = jnp.asarray(np.where(visible, 0.0, -np.inf), F32)
    return pl.pallas_call(
        functools.partial(_attn_kernel, tq=tq),
        grid=(nb, seq // tq),
        in_specs=[
            pl.BlockSpec((None, QK_ROWS, tq), lambda b, i: (b, 0, i)),
            pl.BlockSpec((None, seq, kw), lambda b, i: (b, 0, 0)),
            pl.BlockSpec((None, ATTN_WIDTH, seq), lambda b, i: (b, 0, 0)),
            pl.BlockSpec((None, LANES, kw), lambda b, i: (0, 0, 0)),
            pl.BlockSpec((None, ATTN_WIDTH, LANES), lambda b, i: (0, 0, 0)),
            _const_spec((tq + LANES, tq)),
        ],
        out_specs=pl.BlockSpec((None, ATTN_WIDTH, tq), lambda b, i: (b, 0, i)),
        out_shape=jax.ShapeDtypeStruct((nb, ATTN_WIDTH, seq), BF16),
        scratch_shapes=[
            pltpu.VMEM((N_HEADS, 1, tq), F32),
            pltpu.VMEM((N_HEADS, HEAD_ROWS, tq), F32),
        ],
        compiler_params=pltpu.CompilerParams(
            dimension_semantics=("arbitrary", "arbitrary"), vmem_limit_bytes=VMEM_LIMIT),
        name="fox_attention",
    )(qta, ka, vt, ka_m, vt_m, bias)


def _post_kernel(ot_ref, pooled_ref, gates_ref, x_ref, wua_ref, wup_ref, wout_ref, g_ref,
                 wr_ref, br_ref, h2_ref, plan_ref, cnt_ref, cnt_sc, *, tm):
    @pl.when((pl.program_id(0) == 0) & (pl.program_id(1) == 0))
    def _():
        cnt_sc[...] = jnp.zeros(cnt_sc.shape, F32)

    y_attn = lax.dot_general(ot_ref[...], wua_ref[...], _TN, preferred_element_type=F32)
    y_pool = jnp.dot(pooled_ref[...], wup_ref[...], preferred_element_type=F32)
    merged = (gates_ref[:, 0:D_MODEL].astype(F32) * y_attn
              + gates_ref[:, D_MODEL:2 * D_MODEL].astype(F32) * y_pool)
    h2 = x_ref[...] + jnp.dot(merged.astype(BF16), wout_ref[...], preferred_element_type=F32)
    h2_ref[...] = h2
    ms = jnp.mean(h2 * h2, axis=-1, keepdims=True)
    hn2 = (h2 * lax.rsqrt(ms + RMS_EPS) * g_ref[...]).astype(BF16)

    logits = jnp.dot(hn2, wr_ref[...], preferred_element_type=F32) + br_ref[...]
    lt = logits.T[0:ROUTER_ROWS, :]
    row = lax.broadcasted_iota(jnp.int32, lt.shape, 0)
    neg_inf = jnp.float32(-jnp.inf)

    def softmax_over(mask):
        z = jnp.where(mask, lt, neg_inf)
        e = jnp.exp(z - jnp.max(z, axis=0, keepdims=True))
        return jnp.where(mask, e / jnp.sum(e, axis=0, keepdims=True), -1.0)

    def argtop(prob):
        top = jnp.max(prob, axis=0, keepdims=True)
        return jnp.min(jnp.where(prob == top, row, LANES), axis=0, keepdims=True)

    g_idx = argtop(softmax_over(row < N_GROUPS))
    e_lo = ROUTER_LANE0 + N_PER_GROUP * g_idx
    e_prob = softmax_over((row >= e_lo) & (row < e_lo + N_PER_GROUP))
    i1 = argtop(e_prob)
    i2 = argtop(jnp.where(row == i1, -1.0, e_prob))
    a = jnp.minimum(i1, i2) - e_lo
    b = jnp.maximum(i1, i2) - e_lo
    pair = ((a * (2 * N_PER_GROUP - 1 - a)) >> 1) + (b - a - 1)
    cls = g_idx * PAIRS_PER_GROUP + pair

    crow = lax.broadcasted_iota(jnp.int32, (LANES, tm), 0)
    member = crow == cls
    onehot = jnp.where(member, 1.0, 0.0)
    r_i = lax.broadcasted_iota(jnp.int32, (tm, tm), 0)
    c_i = lax.broadcasted_iota(jnp.int32, (tm, tm), 1)
    earlier = jnp.where(r_i < c_i, 1.0, 0.0).astype(BF16)
    prefix = jnp.dot(onehot.astype(BF16), earlier, preferred_element_type=F32)
    base = jnp.concatenate([cnt_sc[...]] * (tm // LANES), axis=1)
    rank = jnp.sum(jnp.where(member, prefix + base, 0.0), axis=0, keepdims=True)
    total = cnt_sc[...] + jnp.broadcast_to(jnp.sum(onehot, axis=1, keepdims=True), cnt_sc.shape)
    cnt_sc[...] = total
    cnt_ref[...] = total

    prow = lax.broadcasted_iota(jnp.int32, (8, tm), 0)
    plan_ref[...] = jnp.where(prow == PLAN_CLASS, cls.astype(F32),
                              jnp.where(prow == PLAN_RANK, rank, 0.0))


def _post(ot, pooled, gates, x, p, *, tm):
    nb, seq, _ = x.shape
    nt = seq // tm
    tok = lambda width: pl.BlockSpec((None, tm, width), lambda b, j: (b, j, 0))
    return pl.pallas_call(
        functools.partial(_post_kernel, tm=tm),
        grid=(nb, nt),
        in_specs=[
            pl.BlockSpec((None, ATTN_WIDTH, tm), lambda b, j: (b, 0, j)),
            tok(POOL_WIDTH), tok(2 * D_MODEL), tok(D_MODEL),
            _const_spec((ATTN_WIDTH, D_MODEL)),
            _const_spec((POOL_WIDTH, D_MODEL)),
            _const_spec((D_MODEL, D_MODEL)),
            _const_spec((1, D_MODEL)),
            _const_spec((D_MODEL, LANES)),
            _const_spec((1, LANES)),
        ],
        out_specs=[tok(D_MODEL),
                   pl.BlockSpec((None, 8, tm), lambda b, j: (b * nt + j, 0, 0)),
                   pl.BlockSpec((LANES, LANES), lambda b, j: (0, 0))],
        out_shape=[
            jax.ShapeDtypeStruct((nb, seq, D_MODEL), F32),
            jax.ShapeDtypeStruct((nb * nt, 8, tm), F32),
            jax.ShapeDtypeStruct((LANES, LANES), F32),
        ],
        scratch_shapes=[pltpu.VMEM((LANES, LANES), F32)],
        compiler_params=pltpu.CompilerParams(
            dimension_semantics=("arbitrary", "arbitrary"), vmem_limit_bytes=VMEM_LIMIT),
        name="post_router",
    )(ot, pooled, gates, x, p["wua"], p["wup"], p["wout"], p["g_ffn"], p["wr"], p["br"])


def _to_slots(dst, values):
    n = values.shape[0]
    for c in range(SLOT_ROWS):
        dst[pl.ds(c, n, stride=SLOT_ROWS), :] = values[:, c * LANES:(c + 1) * LANES]


def _from_slots(src, n):
    return jnp.concatenate(
        [src[pl.ds(c, n, stride=SLOT_ROWS), :] for c in range(SLOT_ROWS)], axis=1)


def _dispatch_kernel(pos_ref, h2_ref, xs_hbm, buf, zeros, sem, zsem, *, tm, n_slots):
    j = pl.program_id(0)
    slot = j % 2

    def drain(s):
        pltpu.make_async_copy(buf.at[s], xs_hbm.at[pl.ds(0, tm * SLOT_ROWS), :], sem.at[s]).wait()

    @pl.when(j >= 2)
    def _():
        drain(slot)

    _to_slots(buf.at[slot], h2_ref[...])

    def body(g, c):
        for u in range(DMA_UNROLL):
            r = g * DMA_UNROLL + u
            src = pl.multiple_of(r * SLOT_ROWS, SLOT_ROWS)
            dst = pl.multiple_of(pos_ref[0, r] * SLOT_ROWS, SLOT_ROWS)
            pltpu.make_async_copy(buf.at[slot, pl.ds(src, SLOT_ROWS), :],
                                  xs_hbm.at[pl.ds(dst, SLOT_ROWS), :],
                                  sem.at[slot]).start(priority=u % DMA_QUEUES)
        return c
    lax.fori_loop(0, tm // DMA_UNROLL, body, 0)

    @pl.when(j == pl.num_programs(0) - 1)
    def _():
        zeros[...] = jnp.zeros(zeros.shape, F32)
        pad = pltpu.make_async_copy(
            zeros, xs_hbm.at[pl.ds(n_slots * SLOT_ROWS, MOE_ROWS * SLOT_ROWS), :], zsem.at[0])
        pad.start()
        drain(slot)
        drain(1 - slot)
        pad.wait()


def _dispatch(h2, pos, *, tm):
    t = h2.shape[0]
    nt = t // tm
    return pl.pallas_call(
        functools.partial(_dispatch_kernel, tm=tm, n_slots=t),
        grid=(nt,),
        in_specs=[
            pl.BlockSpec((None, 1, tm), lambda j: (j, 0, 0), memory_space=pltpu.SMEM),
            pl.BlockSpec((tm, D_MODEL), lambda j: (j, 0)),
        ],
        out_specs=pl.BlockSpec(memory_space=pl.ANY),
        out_shape=jax.ShapeDtypeStruct(((t + MOE_ROWS) * SLOT_ROWS, LANES), F32),
        scratch_shapes=[
            pltpu.VMEM((2, tm * SLOT_ROWS, LANES), F32),
            pltpu.VMEM((MOE_ROWS * SLOT_ROWS, LANES), F32),
            pltpu.SemaphoreType.DMA((2,)),
            pltpu.SemaphoreType.DMA((1,)),
        ],
        compiler_params=pltpu.CompilerParams(
            dimension_semantics=("arbitrary",), vmem_limit_bytes=VMEM_LIMIT),
        name="moe_dispatch",
    )(pos.reshape(nt, 1, tm), h2)


def _moe_kernel(start_ref, nrows_ref, ea_ref, eb_ref, nitems_ref,
                g_ref, wr_ref, br_ref, w1a_ref, w3a_ref, w2a_ref, w1b_ref, w3b_ref, w2b_ref, xs_hbm,
                ys_hbm, xbuf, ybuf, isem, osem):
    k = pl.program_id(0)
    n_items = nitems_ref[0]
    slot = k % 2
    rows = MOE_ROWS * SLOT_ROWS

    def window(kk):
        return pl.ds(pl.multiple_of(start_ref[kk] * SLOT_ROWS, SLOT_ROWS), rows)

    def in_copy(kk, s):
        return pltpu.make_async_copy(xs_hbm.at[window(kk), :], xbuf.at[s], isem.at[s])

    def out_copy(kk, s):
        return pltpu.make_async_copy(ybuf.at[s], ys_hbm.at[window(kk), :], osem.at[s])

    @pl.when(k == 0)
    def _():
        in_copy(0, 0).start()
        ybuf[1] = jnp.zeros(ybuf.shape[1:], F32)
        pad = pltpu.make_async_copy(
            ybuf.at[1], ys_hbm.at[pl.ds(ys_hbm.shape[0] - rows, rows), :], osem.at[1])
        pad.start()
        pad.wait()

    @pl.when(k + 1 < n_items)
    def _():
        in_copy(k + 1, 1 - slot).start()

    @pl.when(k < n_items)
    def _():
        in_copy(k, slot).wait()
        ea, eb = ea_ref[k], eb_ref[k]
        sub = MOE_ROWS // MOE_CHAINS
        for chain in range(MOE_CHAINS):
            block = pl.ds(chain * sub * SLOT_ROWS, sub * SLOT_ROWS)
            h2 = _from_slots(xbuf.at[slot, block], sub)
            ms = jnp.mean(h2 * h2, axis=-1, keepdims=True)
            xn = (h2 * lax.rsqrt(ms + RMS_EPS) * g_ref[...]).astype(BF16)

            logits = jnp.dot(xn, wr_ref[...], preferred_element_type=F32) + br_ref[...]
            lane = lax.broadcasted_iota(jnp.int32, logits.shape, 1)
            pick = lambda idx: jnp.sum(jnp.where(lane == idx, logits, 0.0), axis=1,
                                       keepdims=True)
            gl = jnp.where(lane < N_GROUPS, logits, -jnp.inf)
            gmax = jnp.max(gl, axis=1, keepdims=True)
            p_group = (jnp.exp(pick(ea // N_PER_GROUP) - gmax)
                       / jnp.sum(jnp.exp(gl - gmax), axis=1, keepdims=True))
            la, lb = pick(ROUTER_LANE0 + ea), pick(ROUTER_LANE0 + eb)
            row = chain * sub + lax.broadcasted_iota(jnp.int32, (sub, 1), 0)
            valid = row < nrows_ref[k]
            w_lo = jnp.where(valid, p_group / (1.0 + jnp.exp(lb - la)), 0.0)
            w_hi = jnp.where(valid, p_group / (1.0 + jnp.exp(la - lb)), 0.0)

            def expert(w1_ref, w3_ref, w2_ref, weight):
                h1 = jnp.dot(xn, w1_ref[...], preferred_element_type=F32)
                h3 = jnp.dot(xn, w3_ref[...], preferred_element_type=F32)
                hh = h1 * jax.nn.sigmoid(h1) * h3 * weight
                return jnp.dot(hh.astype(BF16), w2_ref[...], preferred_element_type=F32)

            y = h2 + (expert(w1a_ref, w3a_ref, w2a_ref, w_lo)
                      + expert(w1b_ref, w3b_ref, w2b_ref, w_hi))
            _to_slots(ybuf.at[slot, block], y)

        @pl.when(k >= 1)
        def _():
            out_copy(k - 1, 1 - slot).wait()

        out_copy(k, slot).start()

        @pl.when(k == n_items - 1)
        def _():
            out_copy(k, slot).wait()


def _class_experts():
    ea, eb = [], []
    for g in range(N_GROUPS):
        for a in range(N_PER_GROUP):
            for b in range(a + 1, N_PER_GROUP):
                ea.append(g * N_PER_GROUP + a)
                eb.append(g * N_PER_GROUP + b)
    fill = LANES - len(ea)
    return (jnp.array(ea + [ea[-1]] * fill, jnp.int32),
            jnp.array(eb + [eb[-1]] * fill, jnp.int32))


def _combine_kernel(pos_ref, pos_next_ref, ys_hbm, out_ref, buf, sem, *, tm):
    j = pl.program_id(0)
    slot = j % 2

    def start_tile(pos, s):
        def body(g, c):
            for u in range(DMA_UNROLL):
                r = g * DMA_UNROLL + u
                src = pl.multiple_of(pos[0, r] * SLOT_ROWS, SLOT_ROWS)
                dst = pl.multiple_of(r * SLOT_ROWS, SLOT_ROWS)
                pltpu.make_async_copy(ys_hbm.at[pl.ds(src, SLOT_ROWS), :],
                                      buf.at[s, pl.ds(dst, SLOT_ROWS), :],
                                      sem.at[s]).start(priority=u % DMA_QUEUES)
            return c
        lax.fori_loop(0, tm // DMA_UNROLL, body, 0)

    @pl.when(j == 0)
    def _():
        start_tile(pos_ref, 0)

    @pl.when(j + 1 < pl.num_programs(0))
    def _():
        start_tile(pos_next_ref, 1 - slot)

    pltpu.make_async_copy(ys_hbm.at[pl.ds(0, tm * SLOT_ROWS), :], buf.at[slot], sem.at[slot]).wait()
    out_ref[...] = _from_slots(buf.at[slot], tm)


def _combine(ys, pos, *, tm):
    t = pos.shape[0]
    nt = t // tm
    pos3 = pos.reshape(nt, 1, tm)
    smem_tile = lambda fn: pl.BlockSpec((None, 1, tm), fn, memory_space=pltpu.SMEM)
    return pl.pallas_call(
        functools.partial(_combine_kernel, tm=tm),
        grid=(nt,),
        in_specs=[
            smem_tile(lambda j: (j, 0, 0)),
            smem_tile(lambda j: (jnp.minimum(j + 1, nt - 1), 0, 0)),
            pl.BlockSpec(memory_space=pl.ANY),
        ],
        out_specs=pl.BlockSpec((tm, D_MODEL), lambda j: (j, 0)),
        out_shape=jax.ShapeDtypeStruct((t, D_MODEL), F32),
        scratch_shapes=[
            pltpu.VMEM((2, tm * SLOT_ROWS, LANES), F32),
            pltpu.SemaphoreType.DMA((2,)),
        ],
        compiler_params=pltpu.CompilerParams(
            dimension_semantics=("arbitrary",), vmem_limit_bytes=VMEM_LIMIT),
        name="moe_combine",
    )(pos3, pos3, ys)


def _moe(h2, plan, counts, p, *, tm):
    t = h2.shape[0]
    assert t % tm == 0 and t // tm >= 2
    max_items = -(-t // MOE_ROWS) + N_CLASSES
    class_ids = jnp.arange(LANES, dtype=jnp.int32)
    c_end = jnp.cumsum(counts)
    c_start = c_end - counts
    cls = plan[:, PLAN_CLASS, :].reshape(t).astype(jnp.int32)
    rank = plan[:, PLAN_RANK, :].reshape(t).astype(jnp.int32)
    pos = jnp.sum(jnp.where(cls[:, None] == class_ids[None, :], c_start[None, :], 0),
                  axis=1) + rank
    n_items_c = (counts + MOE_ROWS - 1) // MOE_ROWS
    item_end = jnp.cumsum(n_items_c)
    item_start = item_end - n_items_c
    n_items = item_end[-1]
    item_ids = jnp.minimum(jnp.arange(max_items, dtype=jnp.int32), n_items - 1)
    icls = jnp.sum(item_end[None, :] <= item_ids[:, None], axis=1, dtype=jnp.int32)
    pick = lambda table: jnp.sum(
        jnp.where(icls[:, None] == class_ids[None, :], table[None, :], 0), axis=1)
    offset = (item_ids - pick(item_start)) * MOE_ROWS
    start = pick(c_start) + offset
    nrows = jnp.minimum(pick(counts) - offset, MOE_ROWS)
    ea_tab, eb_tab = _class_experts()
    ea, eb = pick(ea_tab), pick(eb_tab)

    xs = _dispatch(h2, pos, tm=tm)

    first = lambda ea, eb: ea
    second = lambda ea, eb: eb
    w13 = lambda sel: pl.BlockSpec((None, D_MODEL, EXPERT_HIDDEN),
                                   lambda k, st, nr, ea, eb, n: (sel(ea, eb)[k], 0, 0))
    w2 = lambda sel: pl.BlockSpec((None, EXPERT_HIDDEN, D_MODEL),
                                  lambda k, st, nr, ea, eb, n: (sel(ea, eb)[k], 0, 0))
    const = lambda shape: pl.BlockSpec(shape, lambda k, *_: (0, 0))
    ys = pl.pallas_call(
        _moe_kernel,
        grid_spec=pltpu.PrefetchScalarGridSpec(
            num_scalar_prefetch=5,
            grid=(max_items,),
            in_specs=[
                const((1, D_MODEL)), const((D_MODEL, LANES)), const((1, LANES)),
                w13(first), w13(first), w2(first), w13(second), w13(second), w2(second),
                pl.BlockSpec(memory_space=pl.ANY),
            ],
            out_specs=pl.BlockSpec(memory_space=pl.ANY),
            scratch_shapes=[
                pltpu.VMEM((2, MOE_ROWS * SLOT_ROWS, LANES), F32),
                pltpu.VMEM((2, MOE_ROWS * SLOT_ROWS, LANES), F32),
                pltpu.SemaphoreType.DMA((2,)),
                pltpu.SemaphoreType.DMA((2,)),
            ],
        ),
        out_shape=jax.ShapeDtypeStruct(((t + MOE_ROWS) * SLOT_ROWS, LANES), F32),
        compiler_params=pltpu.CompilerParams(
            dimension_semantics=("arbitrary",), vmem_limit_bytes=VMEM_LIMIT),
        name="moe",
    )(start, nrows, ea, eb, n_items.reshape(1), p["g_ffn"], p["wr"], p["br"],
      p["w1"], p["w3"], p["w2"], p["w1"], p["w3"], p["w2"], xs)
    return _combine(ys, pos, tm=tm)


def _bias_selectors():
    selq = [[0.0] * LANES for _ in range(LANES)]
    selk = [[0.0] * LANES for _ in range(LANES)]
    ones_row = 3 * N_HEADS
    for h in range(N_HEADS):
        for part in range(3):
            selq[h * AUG_ROWS + part][part * N_HEADS + h] = 1.0
            selq[h * AUG_ROWS + 3 + part][ones_row] = 1.0
            selk[h * AUG_ROWS + part][ones_row] = 1.0
            selk[h * AUG_ROWS + 3 + part][part * N_HEADS + h] = -1.0
    return jnp.array(selq, BF16), jnp.array(selk, BF16)


def _prepare_params(norm_mix_g, w_in, b_forget, q_norm_g, k_norm_g, w_up_attn, w_pool,
                    pool_scale, w_up_pool, w_out, norm_ffn_g, w_group, b_group, w_router,
                    b_router, w1, w3, w2):
    aw, pw = ATTN_WIDTH, POOL_WIDTH
    f_off = 3 * aw
    p_off = f_off + N_HEADS
    g_off = p_off + pw
    selq, selk = _bias_selectors()
    pad_lanes = lambda a: jnp.pad(a, ((0, 0), (0, LANES - a.shape[1])))
    wpool = jnp.zeros((pw, pw), F32)
    for g in range(len(POOL_WINDOWS)):
        sl = slice(g * POOL_GROUP_DIM, (g + 1) * POOL_GROUP_DIM)
        wpool = wpool.at[sl, sl].set(w_pool[g])
    d, f = D_MODEL, EXPERT_HIDDEN
    return {
        "g_mix": norm_mix_g.reshape(1, d),
        "wqkv": w_in[:, 0:f_off].astype(BF16),
        "wf": pad_lanes(w_in[:, f_off:p_off]).astype(BF16),
        "bf": pad_lanes(b_forget.reshape(1, N_HEADS)),
        "wp": w_in[:, p_off:g_off].astype(BF16),
        "wg": w_in[:, g_off:].astype(BF16),
        "gq": q_norm_g, "gk": k_norm_g,
        "selq": selq, "selk": selk,
        "wpool": wpool.astype(BF16),
        "pscale": pool_scale.reshape(1, pw),
        "wua": w_up_attn.astype(BF16),
        "wup": w_up_pool.astype(BF16),
        "wout": w_out.astype(BF16),
        "g_ffn": norm_ffn_g.reshape(1, d),
        "wr": pad_lanes(jnp.concatenate([w_group, w_router], axis=1)).astype(BF16),
        "br": pad_lanes(jnp.concatenate([b_group, b_router]).reshape(1, -1)),
        "w1": w1.reshape(N_EXPERTS, d, f).astype(BF16),
        "w3": w3.reshape(N_EXPERTS, d, f).astype(BF16),
        "w2": w2.reshape(N_EXPERTS, f, d).astype(BF16),
    }


def kernel(x, meta_tokens, norm_mix_g, w_in, b_forget, q_norm_g, k_norm_g, w_up_attn, w_pool,
           pool_scale, w_up_pool, w_out, norm_ffn_g, w_group, b_group, w_router, b_router,
           w1, w3, w2):
    nb, seq, d = x.shape
    p = _prepare_params(norm_mix_g[0], w_in[0], b_forget[0], q_norm_g[0], k_norm_g[0],
                        w_up_attn[0], w_pool[0], pool_scale[0], w_up_pool[0], w_out[0],
                        norm_ffn_g[0], w_group[0], b_group[0], w_router[0], b_router[0],
                        w1[0], w3[0], w2[0])

    meta = jnp.pad(meta_tokens.astype(x.dtype), ((0, LANES - N_META), (0, 0)))[None]
    _, ka_m, vt_m, _, _, u_meta, f_meta = _inproj(
        meta, p, jnp.zeros((MAX_WINDOW, POOL_WIDTH), F32), jnp.zeros((N_HEADS, LANES), F32),
        tm=LANES, n_valid=N_META, first_pos=0)

    tm = 512
    qta, ka, vt, pooled, gates, _, _ = _inproj(
        x, p, u_meta, f_meta, tm=tm, n_valid=tm, first_pos=N_META)
    ot = _attention(qta, ka, vt, ka_m, vt_m, tq=512)
    h2, plan, cnt = _post(ot, pooled, gates, x, p, tm=512)
    out = _moe(h2.reshape(nb * seq, d), plan, cnt[:, 0].astype(jnp.int32), p, tm=512)
    return out.reshape(nb, seq, d)
```

```python
import functools
import math

import jax
import jax.numpy as jnp
import numpy as np
from jax import lax
from jax.experimental import pallas as pl
from jax.experimental.pallas import tpu as pltpu

F32 = jnp.float32
BF16 = jnp.bfloat16

D_MODEL = 1024
N_META = 16
N_HEADS = 8
HEAD_DIM = 64
ATTN_WIDTH = N_HEADS * HEAD_DIM
POOL_WINDOWS = (2, 4, 8, 16)
POOL_WIDTH = 512
POOL_GROUP_DIM = POOL_WIDTH // len(POOL_WINDOWS)
MAX_WINDOW = max(POOL_WINDOWS)
N_GROUPS = 4
N_PER_GROUP = 8
N_EXPERTS = N_GROUPS * N_PER_GROUP
EXPERT_HIDDEN = 256
RMS_EPS = 1e-6
LOG2_E = math.log2(math.e)

LANES = 128
AUG_ROWS = 16
HEAD_ROWS = HEAD_DIM + AUG_ROWS
QK_ROWS = N_HEADS * HEAD_ROWS
ROUTER_LANE0 = N_GROUPS
PAIRS_PER_GROUP = N_PER_GROUP * (N_PER_GROUP - 1) // 2
N_CLASSES = N_GROUPS * PAIRS_PER_GROUP
ROUTER_ROWS = 40
PLAN_CLASS, PLAN_RANK = 0, 1
MOE_ROWS = 352
SLOT_ROWS = D_MODEL // LANES
DMA_UNROLL = 8
DMA_QUEUES = 2
MOE_CHAINS = 2
SAFE_EXP2 = 64.0
NORM_SLACK = 1.02
VMEM_LIMIT = 56 * 1024 * 1024

_TN = (((0,), (0,)), ((), ()))


def _const_spec(shape):
    zeros = (0,) * len(shape)
    return pl.BlockSpec(shape, lambda *_: zeros, pipeline_mode=pl.Buffered(1))


def _split3(x):
    hi = x.astype(BF16).astype(F32)
    r = x - hi
    mid = r.astype(BF16).astype(F32)
    lo = (r - mid).astype(BF16).astype(F32)
    return hi, mid, lo


def _inproj_kernel(x_ref, g_ref, wqkv_ref, wf_ref, bf_ref, wp_ref, wg_ref, gq_ref, gk_ref,
                   selq_ref, selk_ref, wpool_ref, pscale_ref, uprev_ref, fprev_ref,
                   qta_ref, ka_ref, vt_ref, pooled_ref, gates_ref, utail_ref, ftail_ref, kmax_ref,
                   ubuf, fcarry, *, tm, n_valid, first_pos):
    j = pl.program_id(1)

    @pl.when(j == 0)
    def _():
        ubuf[0:MAX_WINDOW, :] = uprev_ref[...]
        fcarry[...] = fprev_ref[...]

    x = x_ref[...]
    ms = jnp.mean(x * x, axis=-1, keepdims=True)
    hn = (x * lax.rsqrt(ms + RMS_EPS) * g_ref[...]).astype(BF16)

    f = jnp.dot(hn, wf_ref[...], preferred_element_type=F32) + bf_ref[...]
    z = f.T[0:N_HEADS, :]
    logf = (jnp.minimum(z, 0.0) - jnp.log1p(jnp.exp(-jnp.abs(z)))) * LOG2_E
    parts = jnp.concatenate(_split3(logf), axis=0).astype(BF16)
    r_i = lax.broadcasted_iota(jnp.int32, (tm, tm), 0)
    c_i = lax.broadcasted_iota(jnp.int32, (tm, tm), 1)
    tri = jnp.where(r_i <= c_i, 1.0, 0.0).astype(BF16)
    cs3 = jnp.dot(parts, tri, preferred_element_type=F32)
    cs = cs3[0:8] + cs3[8:16] + cs3[16:24]
    fc = jnp.concatenate([fcarry[...]] * (tm // LANES), axis=1) + cs
    last = jnp.broadcast_to(fc[:, n_valid - 1:n_valid], (N_HEADS, LANES))
    fcarry[...] = last
    ftail_ref[...] = last

    pieces = jnp.concatenate(
        _split3(fc) + (jnp.ones((8, tm), F32), jnp.zeros((LANES - 32, tm), F32)),
        axis=0).astype(BF16)
    fq = jnp.dot(selq_ref[...], pieces, preferred_element_type=F32)
    fk = jnp.dot(selk_ref[...], pieces, preferred_element_type=F32)

    qkv = jnp.dot(hn, wqkv_ref[...], preferred_element_type=F32)
    qt = qkv[:, 0:ATTN_WIDTH].T
    kt = qkv[:, ATTN_WIDTH:2 * ATTN_WIDTH].T
    vt_ref[...] = qkv[:, 2 * ATTN_WIDTH:3 * ATTN_WIDTH].T.astype(BF16)

    def head_norm(src, gain_ref, h):
        xh = src[h * HEAD_DIM:(h + 1) * HEAD_DIM, :]
        ssq = jnp.mean(xh * xh, axis=0, keepdims=True)
        return xh * lax.rsqrt(ssq + RMS_EPS) * gain_ref[...]

    k_pad = jnp.zeros((LANES - HEAD_ROWS, tm), F32)
    for h in range(N_HEADS):
        aug = slice(h * AUG_ROWS, (h + 1) * AUG_ROWS)
        r0 = h * HEAD_ROWS
        qta_ref[r0:r0 + HEAD_DIM, :] = head_norm(qt, gq_ref, h).astype(BF16)
        qta_ref[r0 + HEAD_DIM:r0 + HEAD_ROWS, :] = fq[aug, :].astype(BF16)
        kn = head_norm(kt, gk_ref, h)
        k_blk = jnp.concatenate([kn, fk[aug, :], k_pad], axis=0)
        ka_ref[:, h * LANES:(h + 1) * LANES] = k_blk.T.astype(BF16)
        k_sq = jnp.max(jnp.sum(kn * kn, axis=0, keepdims=True), axis=1, keepdims=True)
        kmax_ref[h:h + 1, :] = jnp.broadcast_to(k_sq, (1, LANES))

    u = jnp.dot(hn, wp_ref[...], preferred_element_type=F32)
    ubuf[MAX_WINDOW:MAX_WINDOW + tm, :] = u
    mixed = []
    for g, w in enumerate(POOL_WINDOWS):
        c0 = g * POOL_GROUP_DIM
        acc = u[:, c0:c0 + POOL_GROUP_DIM]
        for s in range(1, w):
            acc = acc + ubuf[MAX_WINDOW - s:MAX_WINDOW - s + tm, c0:c0 + POOL_GROUP_DIM]
        if first_pos + 1 >= w:
            mean = acc * (1.0 / w)
        else:
            pos = first_pos + j * tm + lax.broadcasted_iota(jnp.int32, (tm, POOL_GROUP_DIM), 0)
            mean = acc / jnp.minimum(pos + 1, w).astype(F32)
        mixed.append(mean - u[:, c0:c0 + POOL_GROUP_DIM])
    mixed = jnp.concatenate(mixed, axis=1).astype(BF16)
    y = jnp.dot(mixed, wpool_ref[...], preferred_element_type=F32) * pscale_ref[...]
    pooled_ref[...] = y.astype(BF16)
    tail = u[n_valid - MAX_WINDOW:n_valid, :]
    ubuf[0:MAX_WINDOW, :] = tail
    utail_ref[...] = tail

    gl = jnp.dot(hn, wg_ref[...], preferred_element_type=F32)
    gates_ref[...] = jax.nn.sigmoid(gl).astype(BF16)


def _inproj(x3, p, uprev, fprev, *, tm, n_valid, first_pos):
    nb, seq, _ = x3.shape
    nt = seq // tm
    gq = jnp.broadcast_to((p["gq"] * (LOG2_E / math.sqrt(HEAD_DIM)))[:, None], (HEAD_DIM, tm))
    gk = jnp.broadcast_to(p["gk"][:, None], (HEAD_DIM, tm))
    tok = lambda width: pl.BlockSpec((None, tm, width), lambda b, j: (b, j, 0))
    chan = lambda rows: pl.BlockSpec((None, rows, tm), lambda b, j: (b, 0, j))
    kern = functools.partial(_inproj_kernel, tm=tm, n_valid=n_valid, first_pos=first_pos)
    return pl.pallas_call(
        kern,
        grid=(nb, nt),
        in_specs=[
            tok(D_MODEL),
            _const_spec((1, D_MODEL)),
            _const_spec((D_MODEL, 3 * ATTN_WIDTH)),
            _const_spec((D_MODEL, LANES)),
            _const_spec((1, LANES)),
            _const_spec((D_MODEL, POOL_WIDTH)),
            _const_spec((D_MODEL, 2 * D_MODEL)),
            _const_spec((HEAD_DIM, tm)),
            _const_spec((HEAD_DIM, tm)),
            _const_spec((LANES, LANES)),
            _const_spec((LANES, LANES)),
            _const_spec((POOL_WIDTH, POOL_WIDTH)),
            _const_spec((1, POOL_WIDTH)),
            _const_spec((MAX_WINDOW, POOL_WIDTH)),
            _const_spec((N_HEADS, LANES)),
        ],
        out_specs=[
            chan(QK_ROWS), tok(N_HEADS * LANES), chan(ATTN_WIDTH),
            tok(POOL_WIDTH), tok(2 * D_MODEL),
            pl.BlockSpec((MAX_WINDOW, POOL_WIDTH), lambda b, j: (0, 0)),
            pl.BlockSpec((N_HEADS, LANES), lambda b, j: (0, 0)),
            pl.BlockSpec((None, N_HEADS, LANES), lambda b, j: (b * nt + j, 0, 0)),
        ],
        out_shape=[
            jax.ShapeDtypeStruct((nb, QK_ROWS, seq), BF16),
            jax.ShapeDtypeStruct((nb, seq, N_HEADS * LANES), BF16),
            jax.ShapeDtypeStruct((nb, ATTN_WIDTH, seq), BF16),
            jax.ShapeDtypeStruct((nb, seq, POOL_WIDTH), BF16),
            jax.ShapeDtypeStruct((nb, seq, 2 * D_MODEL), BF16),
            jax.ShapeDtypeStruct((MAX_WINDOW, POOL_WIDTH), F32),
            jax.ShapeDtypeStruct((N_HEADS, LANES), F32),
            jax.ShapeDtypeStruct((nb * nt, N_HEADS, LANES), F32),
        ],
        scratch_shapes=[
            pltpu.VMEM((MAX_WINDOW + tm, POOL_WIDTH), F32),
            pltpu.VMEM((N_HEADS, LANES), F32),
        ],
        compiler_params=pltpu.CompilerParams(
            dimension_semantics=("arbitrary", "arbitrary"), vmem_limit_bytes=VMEM_LIMIT),
        name="inproj",
    )(x3, p["g_mix"], p["wqkv"], p["wf"], p["bf"], p["wp"], p["wg"], gq, gk,
      p["selq"], p["selk"], p["wpool"], p["pscale"], uprev, fprev)


def _attn_kernel(qta_ref, ka_ref, vt_ref, kam_ref, vtm_ref, bias_ref, kmax_ref, ot_ref,
                 m_sc, acc_sc, *, tq):
    qi = pl.program_id(1)
    m_sc[...] = jnp.full(m_sc.shape, -jnp.inf, F32)
    acc_sc[...] = jnp.zeros(acc_sc.shape, F32)

    def rows(h):
        return slice(h * HEAD_ROWS, (h + 1) * HEAD_ROWS)

    def vrows(h):
        return slice(h * HEAD_DIM, (h + 1) * HEAD_DIM)

    def klanes(h):
        return slice(h * LANES, (h + 1) * LANES)

    q_pad = jnp.zeros((LANES - HEAD_ROWS, tq), BF16)

    def scores(h, ka):
        qa = jnp.concatenate([qta_ref[rows(h), :], q_pad], axis=0)
        return jnp.dot(ka, qa, preferred_element_type=F32)

    def new_max(h, s):
        return jnp.maximum(m_sc[h], jnp.max(s, axis=0, keepdims=True))

    def absorb(h, s, m_new, v):
        p = jnp.exp2(s - m_new).astype(BF16)
        va = jnp.concatenate([v, jnp.ones((AUG_ROWS, v.shape[1]), BF16)], axis=0)
        acc_sc[h] = (jnp.exp2(m_sc[h] - m_new) * acc_sc[h]
                     + jnp.dot(va, p, preferred_element_type=F32))
        m_sc[h] = m_new

    def sweep_heads(scores_of, values_of):
        s = {0: scores_of(0), 1: scores_of(1)}
        m = {0: new_max(0, s[0])}
        for h in range(N_HEADS):
            if h + 2 < N_HEADS:
                s[h + 2] = scores_of(h + 2)
            if h + 1 < N_HEADS:
                m[h + 1] = new_max(h + 1, s[h + 1])
            absorb(h, s.pop(h), m.pop(h), values_of(h))

    s0 = pl.multiple_of(qi * tq, tq)
    sweep_heads(
        lambda h: scores(h, jnp.concatenate(
            [ka_ref[pl.ds(s0, tq), klanes(h)], kam_ref[:, klanes(h)]], axis=0)) + bias_ref[...],
        lambda h: jnp.concatenate([vt_ref[vrows(h), pl.ds(s0, tq)], vtm_ref[vrows(h), :]], axis=1))

    worst = None
    for h in range(N_HEADS):
        q = qta_ref[h * HEAD_ROWS:h * HEAD_ROWS + HEAD_DIM, :].astype(F32)
        q_norm = jnp.sqrt(jnp.sum(q * q, axis=0, keepdims=True))
        k_norm = jnp.sqrt(jnp.concatenate([kmax_ref[h:h + 1, :]] * (tq // LANES), axis=1))
        gap = q_norm * k_norm * NORM_SLACK - SAFE_EXP2 - m_sc[h]
        worst = gap if worst is None else jnp.maximum(worst, gap)
    bounded = jnp.max(worst) <= 0.0

    def kv_body(i, carry):
        s0 = pl.multiple_of(i * tq, tq)
        sweep_heads(lambda h: scores(h, ka_ref[pl.ds(s0, tq), klanes(h)]),
                    lambda h: vt_ref[vrows(h), pl.ds(s0, tq)])
        return carry

    def kv_body_bounded(i, carry):
        s0 = pl.multiple_of(i * tq, tq)
        s_next = scores(0, ka_ref[pl.ds(s0, tq), klanes(0)])
        for h in range(N_HEADS):
            s = s_next
            if h + 1 < N_HEADS:
                s_next = scores(h + 1, ka_ref[pl.ds(s0, tq), klanes(h + 1)])
            p = jnp.exp2(s - m_sc[h]).astype(BF16)
            v = vt_ref[vrows(h), pl.ds(s0, tq)]
            va = jnp.concatenate([v, jnp.ones((AUG_ROWS, tq), BF16)], axis=0)
            acc_sc[h] += jnp.dot(va, p, preferred_element_type=F32)
        return carry

    @pl.when(bounded)
    def _():
        lax.fori_loop(0, qi, kv_body_bounded, 0)

    @pl.when(jnp.logical_not(bounded))
    def _():
        lax.fori_loop(0, qi, kv_body, 0)

    for h in range(N_HEADS):
        acc = acc_sc[h]
        ot_ref[vrows(h), :] = (acc[0:HEAD_DIM, :] / acc[HEAD_DIM:HEAD_DIM + 1, :]).astype(BF16)


def _attention(qta, ka, vt, ka_m, vt_m, k_sq_max, *, tq):
    nb, _, seq = qta.shape
    kw = N_HEADS * LANES
    key = np.arange(tq + LANES)[:, None]
    qry = np.arange(tq)[None, :]
    visible = np.where(key < tq, key <= qry, key - tq < N_META)
    bias = jnp.asarray(np.where(visible, 0.0, -np.inf), F32)
    return pl.pallas_call(
        functools.partial(_attn_kernel, tq=tq),
        grid=(nb, seq // tq),
        in_specs=[
            pl.BlockSpec((None, QK_ROWS, tq), lambda b, i: (b, 0, i)),
            pl.BlockSpec((None, seq, kw), lambda b, i: (b, 0, 0)),
            pl.BlockSpec((None, ATTN_WIDTH, seq), lambda b, i: (b, 0, 0)),
            pl.BlockSpec((None, LANES, kw), lambda b, i: (0, 0, 0)),
            pl.BlockSpec((None, ATTN_WIDTH, LANES), lambda b, i: (0, 0, 0)),
            _const_spec((tq + LANES, tq)),
            pl.BlockSpec((None, N_HEADS, LANES), lambda b, i: (b, 0, 0)),
        ],
        out_specs=pl.BlockSpec((None, ATTN_WIDTH, tq), lambda b, i: (b, 0, i)),
        out_shape=jax.ShapeDtypeStruct((nb, ATTN_WIDTH, seq), BF16),
        scratch_shapes=[
            pltpu.VMEM((N_HEADS, 1, tq), F32),
            pltpu.VMEM((N_HEADS, HEAD_ROWS, tq), F32),
        ],
        compiler_params=pltpu.CompilerParams(
            dimension_semantics=("arbitrary", "arbitrary"), vmem_limit_bytes=VMEM_LIMIT),
        name="fox_attention",
    )(qta, ka, vt, ka_m, vt_m, bias, k_sq_max)


def _post_kernel(ot_ref, pooled_ref, gates_ref, x_ref, wua_ref, wup_ref, wout_ref, g_ref,
                 wr_ref, br_ref, h2_ref, plan_ref, cnt_ref, cnt_sc, *, tm):
    @pl.when((pl.program_id(0) == 0) & (pl.program_id(1) == 0))
    def _():
        cnt_sc[...] = jnp.zeros(cnt_sc.shape, F32)

    y_attn = lax.dot_general(ot_ref[...], wua_ref[...], _TN, preferred_element_type=F32)
    y_pool = jnp.dot(pooled_ref[...], wup_ref[...], preferred_element_type=F32)
    merged = (gates_ref[:, 0:D_MODEL].astype(F32) * y_attn
              + gates_ref[:, D_MODEL:2 * D_MODEL].astype(F32) * y_pool)
    h2 = x_ref[...] + jnp.dot(merged.astype(BF16), wout_ref[...], preferred_element_type=F32)
    h2_ref[...] = h2
    ms = jnp.mean(h2 * h2, axis=-1, keepdims=True)
    hn2 = (h2 * lax.rsqrt(ms + RMS_EPS) * g_ref[...]).astype(BF16)

    logits = jnp.dot(hn2, wr_ref[...], preferred_element_type=F32) + br_ref[...]
    lt = logits.T[0:ROUTER_ROWS, :]
    row = lax.broadcasted_iota(jnp.int32, lt.shape, 0)
    neg_inf = jnp.float32(-jnp.inf)

    def softmax_over(mask):
        z = jnp.where(mask, lt, neg_inf)
        e = jnp.exp(z - jnp.max(z, axis=0, keepdims=True))
        return jnp.where(mask, e / jnp.sum(e, axis=0, keepdims=True), -1.0)

    def argtop(prob):
        top = jnp.max(prob, axis=0, keepdims=True)
        return jnp.min(jnp.where(prob == top, row, LANES), axis=0, keepdims=True)

    g_idx = argtop(softmax_over(row < N_GROUPS))
    e_lo = ROUTER_LANE0 + N_PER_GROUP * g_idx
    e_prob = softmax_over((row >= e_lo) & (row < e_lo + N_PER_GROUP))
    i1 = argtop(e_prob)
    i2 = argtop(jnp.where(row == i1, -1.0, e_prob))
    a = jnp.minimum(i1, i2) - e_lo
    b = jnp.maximum(i1, i2) - e_lo
    pair = ((a * (2 * N_PER_GROUP - 1 - a)) >> 1) + (b - a - 1)
    cls = g_idx * PAIRS_PER_GROUP + pair

    crow = lax.broadcasted_iota(jnp.int32, (LANES, tm), 0)
    member = crow == cls
    onehot = jnp.where(member, 1.0, 0.0)
    r_i = lax.broadcasted_iota(jnp.int32, (tm, tm), 0)
    c_i = lax.broadcasted_iota(jnp.int32, (tm, tm), 1)
    earlier = jnp.where(r_i < c_i, 1.0, 0.0).astype(BF16)
    prefix = jnp.dot(onehot.astype(BF16), earlier, preferred_element_type=F32)
    base = jnp.concatenate([cnt_sc[...]] * (tm // LANES), axis=1)
    rank = jnp.sum(jnp.where(member, prefix + base, 0.0), axis=0, keepdims=True)
    total = cnt_sc[...] + jnp.broadcast_to(jnp.sum(onehot, axis=1, keepdims=True), cnt_sc.shape)
    cnt_sc[...] = total
    cnt_ref[...] = total

    prow = lax.broadcasted_iota(jnp.int32, (8, tm), 0)
    plan_ref[...] = jnp.where(prow == PLAN_CLASS, cls.astype(F32),
                              jnp.where(prow == PLAN_RANK, rank, 0.0))


def _post(ot, pooled, gates, x, p, *, tm):
    nb, seq, _ = x.shape
    nt = seq // tm
    tok = lambda width: pl.BlockSpec((None, tm, width), lambda b, j: (b, j, 0))
    return pl.pallas_call(
        functools.partial(_post_kernel, tm=tm),
        grid=(nb, nt),
        in_specs=[
            pl.BlockSpec((None, ATTN_WIDTH, tm), lambda b, j: (b, 0, j)),
            tok(POOL_WIDTH), tok(2 * D_MODEL), tok(D_MODEL),
            _const_spec((ATTN_WIDTH, D_MODEL)),
            _const_spec((POOL_WIDTH, D_MODEL)),
            _const_spec((D_MODEL, D_MODEL)),
            _const_spec((1, D_MODEL)),
            _const_spec((D_MODEL, LANES)),
            _const_spec((1, LANES)),
        ],
        out_specs=[tok(D_MODEL),
                   pl.BlockSpec((None, 8, tm), lambda b, j: (b * nt + j, 0, 0)),
                   pl.BlockSpec((LANES, LANES), lambda b, j: (0, 0))],
        out_shape=[
            jax.ShapeDtypeStruct((nb, seq, D_MODEL), F32),
            jax.ShapeDtypeStruct((nb * nt, 8, tm), F32),
            jax.ShapeDtypeStruct((LANES, LANES), F32),
        ],
        scratch_shapes=[pltpu.VMEM((LANES, LANES), F32)],
        compiler_params=pltpu.CompilerParams(
            dimension_semantics=("arbitrary", "arbitrary"), vmem_limit_bytes=VMEM_LIMIT),
        name="post_router",
    )(ot, pooled, gates, x, p["wua"], p["wup"], p["wout"], p["g_ffn"], p["wr"], p["br"])


def _to_slots(dst, values):
    n = values.shape[0]
    for c in range(SLOT_ROWS):
        dst[pl.ds(c, n, stride=SLOT_ROWS), :] = values[:, c * LANES:(c + 1) * LANES]


def _from_slots(src, n):
    return jnp.concatenate(
        [src[pl.ds(c, n, stride=SLOT_ROWS), :] for c in range(SLOT_ROWS)], axis=1)


def _dispatch_kernel(pos_ref, h2_ref, xs_hbm, buf, zeros, sem, zsem, *, tm, n_slots):
    j = pl.program_id(0)
    slot = j % 2

    def drain(s):
        pltpu.make_async_copy(buf.at[s], xs_hbm.at[pl.ds(0, tm * SLOT_ROWS), :], sem.at[s]).wait()

    @pl.when(j >= 2)
    def _():
        drain(slot)

    _to_slots(buf.at[slot], h2_ref[...])

    def body(g, c):
        for u in range(DMA_UNROLL):
            r = g * DMA_UNROLL + u
            src = pl.multiple_of(r * SLOT_ROWS, SLOT_ROWS)
            dst = pl.multiple_of(pos_ref[0, r] * SLOT_ROWS, SLOT_ROWS)
            pltpu.make_async_copy(buf.at[slot, pl.ds(src, SLOT_ROWS), :],
                                  xs_hbm.at[pl.ds(dst, SLOT_ROWS), :],
                                  sem.at[slot]).start(priority=u % DMA_QUEUES)
        return c
    lax.fori_loop(0, tm // DMA_UNROLL, body, 0)

    @pl.when(j == pl.num_programs(0) - 1)
    def _():
        zeros[...] = jnp.zeros(zeros.shape, F32)
        pad = pltpu.make_async_copy(
            zeros, xs_hbm.at[pl.ds(n_slots * SLOT_ROWS, MOE_ROWS * SLOT_ROWS), :], zsem.at[0])
        pad.start()
        drain(slot)
        drain(1 - slot)
        pad.wait()


def _dispatch(h2, pos, *, tm):
    t = h2.shape[0]
    nt = t // tm
    return pl.pallas_call(
        functools.partial(_dispatch_kernel, tm=tm, n_slots=t),
        grid=(nt,),
        in_specs=[
            pl.BlockSpec((None, 1, tm), lambda j: (j, 0, 0), memory_space=pltpu.SMEM),
            pl.BlockSpec((tm, D_MODEL), lambda j: (j, 0)),
        ],
        out_specs=pl.BlockSpec(memory_space=pl.ANY),
        out_shape=jax.ShapeDtypeStruct(((t + MOE_ROWS) * SLOT_ROWS, LANES), F32),
        scratch_shapes=[
            pltpu.VMEM((2, tm * SLOT_ROWS, LANES), F32),
            pltpu.VMEM((MOE_ROWS * SLOT_ROWS, LANES), F32),
            pltpu.SemaphoreType.DMA((2,)),
            pltpu.SemaphoreType.DMA((1,)),
        ],
        compiler_params=pltpu.CompilerParams(
            dimension_semantics=("arbitrary",), vmem_limit_bytes=VMEM_LIMIT),
        name="moe_dispatch",
    )(pos.reshape(nt, 1, tm), h2)


def _moe_kernel(start_ref, nrows_ref, ea_ref, eb_ref, nitems_ref,
                g_ref, wr_ref, br_ref, w1a_ref, w3a_ref, w2a_ref, w1b_ref, w3b_ref, w2b_ref, xs_hbm,
                ys_hbm, xbuf, ybuf, isem, osem):
    k = pl.program_id(0)
    n_items = nitems_ref[0]
    slot = k % 2
    rows = MOE_ROWS * SLOT_ROWS

    def window(kk):
        return pl.ds(pl.multiple_of(start_ref[kk] * SLOT_ROWS, SLOT_ROWS), rows)

    def in_copy(kk, s):
        return pltpu.make_async_copy(xs_hbm.at[window(kk), :], xbuf.at[s], isem.at[s])

    def out_copy(kk, s):
        return pltpu.make_async_copy(ybuf.at[s], ys_hbm.at[window(kk), :], osem.at[s])

    @pl.when(k == 0)
    def _():
        in_copy(0, 0).start()
        ybuf[1] = jnp.zeros(ybuf.shape[1:], F32)
        pad = pltpu.make_async_copy(
            ybuf.at[1], ys_hbm.at[pl.ds(ys_hbm.shape[0] - rows, rows), :], osem.at[1])
        pad.start()
        pad.wait()

    @pl.when(k + 1 < n_items)
    def _():
        in_copy(k + 1, 1 - slot).start()

    @pl.when(k < n_items)
    def _():
        in_copy(k, slot).wait()
        ea, eb = ea_ref[k], eb_ref[k]
        sub = MOE_ROWS // MOE_CHAINS
        for chain in range(MOE_CHAINS):
            block = pl.ds(chain * sub * SLOT_ROWS, sub * SLOT_ROWS)
            h2 = _from_slots(xbuf.at[slot, block], sub)
            ms = jnp.mean(h2 * h2, axis=-1, keepdims=True)
            xn = (h2 * lax.rsqrt(ms + RMS_EPS) * g_ref[...]).astype(BF16)

            logits = jnp.dot(xn, wr_ref[...], preferred_element_type=F32) + br_ref[...]
            lane = lax.broadcasted_iota(jnp.int32, logits.shape, 1)
            pick = lambda idx: jnp.sum(jnp.where(lane == idx, logits, 0.0), axis=1,
                                       keepdims=True)
            gl = jnp.where(lane < N_GROUPS, logits, -jnp.inf)
            gmax = jnp.max(gl, axis=1, keepdims=True)
            p_group = (jnp.exp(pick(ea // N_PER_GROUP) - gmax)
                       / jnp.sum(jnp.exp(gl - gmax), axis=1, keepdims=True))
            la, lb = pick(ROUTER_LANE0 + ea), pick(ROUTER_LANE0 + eb)
            row = chain * sub + lax.broadcasted_iota(jnp.int32, (sub, 1), 0)
            valid = row < nrows_ref[k]
            w_lo = jnp.where(valid, p_group / (1.0 + jnp.exp(lb - la)), 0.0)
            w_hi = jnp.where(valid, p_group / (1.0 + jnp.exp(la - lb)), 0.0)

            def expert(w1_ref, w3_ref, w2_ref, weight):
                h1 = jnp.dot(xn, w1_ref[...], preferred_element_type=F32)
                h3 = jnp.dot(xn, w3_ref[...], preferred_element_type=F32)
                hh = h1 * jax.nn.sigmoid(h1) * h3 * weight
                return jnp.dot(hh.astype(BF16), w2_ref[...], preferred_element_type=F32)

            y = h2 + (expert(w1a_ref, w3a_ref, w2a_ref, w_lo)
                      + expert(w1b_ref, w3b_ref, w2b_ref, w_hi))
            _to_slots(ybuf.at[slot, block], y)

        @pl.when(k >= 1)
        def _():
            out_copy(k - 1, 1 - slot).wait()

        out_copy(k, slot).start()

        @pl.when(k == n_items - 1)
        def _():
            out_copy(k, slot).wait()


def _class_experts():
    ea, eb = [], []
    for g in range(N_GROUPS):
        for a in range(N_PER_GROUP):
            for b in range(a + 1, N_PER_GROUP):
                ea.append(g * N_PER_GROUP + a)
                eb.append(g * N_PER_GROUP + b)
    fill = LANES - len(ea)
    return (jnp.array(ea + [ea[-1]] * fill, jnp.int32),
            jnp.array(eb + [eb[-1]] * fill, jnp.int32))


def _combine_kernel(pos_ref, pos_next_ref, ys_hbm, out_ref, buf, sem, *, tm):
    j = pl.program_id(0)
    slot = j % 2

    def start_tile(pos, s):
        def body(g, c):
            for u in range(DMA_UNROLL):
                r = g * DMA_UNROLL + u
                src = pl.multiple_of(pos[0, r] * SLOT_ROWS, SLOT_ROWS)
                dst = pl.multiple_of(r * SLOT_ROWS, SLOT_ROWS)
                pltpu.make_async_copy(ys_hbm.at[pl.ds(src, SLOT_ROWS), :],
                                      buf.at[s, pl.ds(dst, SLOT_ROWS), :],
                                      sem.at[s]).start(priority=u % DMA_QUEUES)
            return c
        lax.fori_loop(0, tm // DMA_UNROLL, body, 0)

    @pl.when(j == 0)
    def _():
        start_tile(pos_ref, 0)

    @pl.when(j + 1 < pl.num_programs(0))
    def _():
        start_tile(pos_next_ref, 1 - slot)

    pltpu.make_async_copy(ys_hbm.at[pl.ds(0, tm * SLOT_ROWS), :], buf.at[slot], sem.at[slot]).wait()
    out_ref[...] = _from_slots(buf.at[slot], tm)


def _combine(ys, pos, *, tm):
    t = pos.shape[0]
    nt = t // tm
    pos3 = pos.reshape(nt, 1, tm)
    smem_tile = lambda fn: pl.BlockSpec((None, 1, tm), fn, memory_space=pltpu.SMEM)
    return pl.pallas_call(
        functools.partial(_combine_kernel, tm=tm),
        grid=(nt,),
        in_specs=[
            smem_tile(lambda j: (j, 0, 0)),
            smem_tile(lambda j: (jnp.minimum(j + 1, nt - 1), 0, 0)),
            pl.BlockSpec(memory_space=pl.ANY),
        ],
        out_specs=pl.BlockSpec((tm, D_MODEL), lambda j: (j, 0)),
        out_shape=jax.ShapeDtypeStruct((t, D_MODEL), F32),
        scratch_shapes=[
            pltpu.VMEM((2, tm * SLOT_ROWS, LANES), F32),
            pltpu.SemaphoreType.DMA((2,)),
        ],
        compiler_params=pltpu.CompilerParams(
            dimension_semantics=("arbitrary",), vmem_limit_bytes=VMEM_LIMIT),
        name="moe_combine",
    )(pos3, pos3, ys)


def _moe(h2, plan, counts, p, *, tm):
    t = h2.shape[0]
    assert t % tm == 0 and t // tm >= 2
    max_items = -(-t // MOE_ROWS) + N_CLASSES
    class_ids = jnp.arange(LANES, dtype=jnp.int32)
    c_end = jnp.cumsum(counts)
    c_start = c_end - counts
    cls = plan[:, PLAN_CLASS, :].reshape(t).astype(jnp.int32)
    rank = plan[:, PLAN_RANK, :].reshape(t).astype(jnp.int32)
    pos = jnp.sum(jnp.where(cls[:, None] == class_ids[None, :], c_start[None, :], 0),
                  axis=1) + rank
    n_items_c = (counts + MOE_ROWS - 1) // MOE_ROWS
    item_end = jnp.cumsum(n_items_c)
    item_start = item_end - n_items_c
    n_items = item_end[-1]
    item_ids = jnp.minimum(jnp.arange(max_items, dtype=jnp.int32), n_items - 1)
    icls = jnp.sum(item_end[None, :] <= item_ids[:, None], axis=1, dtype=jnp.int32)
    pick = lambda table: jnp.sum(
        jnp.where(icls[:, None] == class_ids[None, :], table[None, :], 0), axis=1)
    offset = (item_ids - pick(item_start)) * MOE_ROWS
    start = pick(c_start) + offset
    nrows = jnp.minimum(pick(counts) - offset, MOE_ROWS)
    ea_tab, eb_tab = _class_experts()
    ea, eb = pick(ea_tab), pick(eb_tab)

    xs = _dispatch(h2, pos, tm=tm)

    first = lambda ea, eb: ea
    second = lambda ea, eb: eb
    w13 = lambda sel: pl.BlockSpec((None, D_MODEL, EXPERT_HIDDEN),
                                   lambda k, st, nr, ea, eb, n: (sel(ea, eb)[k], 0, 0))
    w2 = lambda sel: pl.BlockSpec((None, EXPERT_HIDDEN, D_MODEL),
                                  lambda k, st, nr, ea, eb, n: (sel(ea, eb)[k], 0, 0))
    const = lambda shape: pl.BlockSpec(shape, lambda k, *_: (0, 0))
    ys = pl.pallas_call(
        _moe_kernel,
        grid_spec=pltpu.PrefetchScalarGridSpec(
            num_scalar_prefetch=5,
            grid=(max_items,),
            in_specs=[
                const((1, D_MODEL)), const((D_MODEL, LANES)), const((1, LANES)),
                w13(first), w13(first), w2(first), w13(second), w13(second), w2(second),
                pl.BlockSpec(memory_space=pl.ANY),
            ],
            out_specs=pl.BlockSpec(memory_space=pl.ANY),
            scratch_shapes=[
                pltpu.VMEM((2, MOE_ROWS * SLOT_ROWS, LANES), F32),
                pltpu.VMEM((2, MOE_ROWS * SLOT_ROWS, LANES), F32),
                pltpu.SemaphoreType.DMA((2,)),
                pltpu.SemaphoreType.DMA((2,)),
            ],
        ),
        out_shape=jax.ShapeDtypeStruct(((t + MOE_ROWS) * SLOT_ROWS, LANES), F32),
        compiler_params=pltpu.CompilerParams(
            dimension_semantics=("arbitrary",), vmem_limit_bytes=VMEM_LIMIT),
        name="moe",
    )(start, nrows, ea, eb, n_items.reshape(1), p["g_ffn"], p["wr"], p["br"],
      p["w1"], p["w3"], p["w2"], p["w1"], p["w3"], p["w2"], xs)
    return _combine(ys, pos, tm=tm)


def _bias_selectors():
    selq = [[0.0] * LANES for _ in range(LANES)]
    selk = [[0.0] * LANES for _ in range(LANES)]
    ones_row = 3 * N_HEADS
    for h in range(N_HEADS):
        for part in range(3):
            selq[h * AUG_ROWS + part][part * N_HEADS + h] = 1.0
            selq[h * AUG_ROWS + 3 + part][ones_row] = 1.0
            selk[h * AUG_ROWS + part][ones_row] = 1.0
            selk[h * AUG_ROWS + 3 + part][part * N_HEADS + h] = -1.0
    return jnp.array(selq, BF16), jnp.array(selk, BF16)


def _prepare_params(norm_mix_g, w_in, b_forget, q_norm_g, k_norm_g, w_up_attn, w_pool,
                    pool_scale, w_up_pool, w_out, norm_ffn_g, w_group, b_group, w_router,
                    b_router, w1, w3, w2):
    aw, pw = ATTN_WIDTH, POOL_WIDTH
    f_off = 3 * aw
    p_off = f_off + N_HEADS
    g_off = p_off + pw
    selq, selk = _bias_selectors()
    pad_lanes = lambda a: jnp.pad(a, ((0, 0), (0, LANES - a.shape[1])))
    wpool = jnp.zeros((pw, pw), F32)
    for g in range(len(POOL_WINDOWS)):
        sl = slice(g * POOL_GROUP_DIM, (g + 1) * POOL_GROUP_DIM)
        wpool = wpool.at[sl, sl].set(w_pool[g])
    d, f = D_MODEL, EXPERT_HIDDEN
    return {
        "g_mix": norm_mix_g.reshape(1, d),
        "wqkv": w_in[:, 0:f_off].astype(BF16),
        "wf": pad_lanes(w_in[:, f_off:p_off]).astype(BF16),
        "bf": pad_lanes(b_forget.reshape(1, N_HEADS)),
        "wp": w_in[:, p_off:g_off].astype(BF16),
        "wg": w_in[:, g_off:].astype(BF16),
        "gq": q_norm_g, "gk": k_norm_g,
        "selq": selq, "selk": selk,
        "wpool": wpool.astype(BF16),
        "pscale": pool_scale.reshape(1, pw),
        "wua": w_up_attn.astype(BF16),
        "wup": w_up_pool.astype(BF16),
        "wout": w_out.astype(BF16),
        "g_ffn": norm_ffn_g.reshape(1, d),
        "wr": pad_lanes(jnp.concatenate([w_group, w_router], axis=1)).astype(BF16),
        "br": pad_lanes(jnp.concatenate([b_group, b_router]).reshape(1, -1)),
        "w1": w1.reshape(N_EXPERTS, d, f).astype(BF16),
        "w3": w3.reshape(N_EXPERTS, d, f).astype(BF16),
        "w2": w2.reshape(N_EXPERTS, f, d).astype(BF16),
    }


def kernel(x, meta_tokens, norm_mix_g, w_in, b_forget, q_norm_g, k_norm_g, w_up_attn, w_pool,
           pool_scale, w_up_pool, w_out, norm_ffn_g, w_group, b_group, w_router, b_router,
           w1, w3, w2):
    nb, seq, d = x.shape
    p = _prepare_params(norm_mix_g[0], w_in[0], b_forget[0], q_norm_g[0], k_norm_g[0],
                        w_up_attn[0], w_pool[0], pool_scale[0], w_up_pool[0], w_out[0],
                        norm_ffn_g[0], w_group[0], b_group[0], w_router[0], b_router[0],
                        w1[0], w3[0], w2[0])

    meta = jnp.pad(meta_tokens.astype(x.dtype), ((0, LANES - N_META), (0, 0)))[None]
    _, ka_m, vt_m, _, _, u_meta, f_meta, _ = _inproj(
        meta, p, jnp.zeros((MAX_WINDOW, POOL_WIDTH), F32), jnp.zeros((N_HEADS, LANES), F32),
        tm=LANES, n_valid=N_META, first_pos=0)

    tm = 512
    qta, ka, vt, pooled, gates, _, _, k_sq = _inproj(
        x, p, u_meta, f_meta, tm=tm, n_valid=tm, first_pos=N_META)
    k_sq_max = jnp.max(k_sq.reshape(nb, seq // tm, N_HEADS, LANES), axis=1)
    ot = _attention(qta, ka, vt, ka_m, vt_m, k_sq_max, tq=512)
    h2, plan, cnt = _post(ot, pooled, gates, x, p, tm=512)
    out = _moe(h2.reshape(nb * seq, d), plan, cnt[:, 0].astype(jnp.int32), p, tm=512)
    return out.reshape(nb, seq, d)
```

```python
import functools
import math

import jax
import jax.numpy as jnp
import numpy as np
from jax import lax
from jax.experimental import pallas as pl
from jax.experimental.pallas import tpu as pltpu

F32 = jnp.float32
BF16 = jnp.bfloat16

D_MODEL = 1024
N_META = 16
N_HEADS = 8
HEAD_DIM = 64
ATTN_WIDTH = N_HEADS * HEAD_DIM
POOL_WINDOWS = (2, 4, 8, 16)
POOL_WIDTH = 512
POOL_GROUP_DIM = POOL_WIDTH // len(POOL_WINDOWS)
MAX_WINDOW = max(POOL_WINDOWS)
N_GROUPS = 4
N_PER_GROUP = 8
N_EXPERTS = N_GROUPS * N_PER_GROUP
EXPERT_HIDDEN = 256
RMS_EPS = 1e-6
LOG2_E = math.log2(math.e)

LANES = 128
AUG_ROWS = 16
HEAD_ROWS = HEAD_DIM + AUG_ROWS
QK_ROWS = N_HEADS * HEAD_ROWS
ROUTER_LANE0 = N_GROUPS
PAIRS_PER_GROUP = N_PER_GROUP * (N_PER_GROUP - 1) // 2
N_CLASSES = N_GROUPS * PAIRS_PER_GROUP
ROUTER_ROWS = 40
PLAN_CLASS, PLAN_RANK = 0, 1
MOE_ROWS = 352
SLOT_ROWS = D_MODEL // LANES
DMA_UNROLL = 8
DMA_QUEUES = 2
MOE_CHAINS = 2
SAFE_EXP2 = 64.0
NORM_SLACK = 1.02
VMEM_LIMIT = 56 * 1024 * 1024

_TN = (((0,), (0,)), ((), ()))


def _const_spec(shape):
    zeros = (0,) * len(shape)
    return pl.BlockSpec(shape, lambda *_: zeros, pipeline_mode=pl.Buffered(1))


def _split3(x):
    hi = x.astype(BF16).astype(F32)
    r = x - hi
    mid = r.astype(BF16).astype(F32)
    lo = (r - mid).astype(BF16).astype(F32)
    return hi, mid, lo


def _inproj_kernel(x_ref, g_ref, wqkv_ref, wf_ref, bf_ref, wp_ref, wg_ref, gq_ref, gk_ref,
                   selq_ref, selk_ref, wpool_ref, pscale_ref, uprev_ref, fprev_ref,
                   qta_ref, ka_ref, vt_ref, pooled_ref, gates_ref, utail_ref, ftail_ref, kmax_ref,
                   self_ref,
                   ubuf, fcarry, *, tm, n_valid, first_pos):
    j = pl.program_id(1)

    @pl.when(j == 0)
    def _():
        ubuf[0:MAX_WINDOW, :] = uprev_ref[...]
        fcarry[...] = fprev_ref[...]

    x = x_ref[...]
    ms = jnp.mean(x * x, axis=-1, keepdims=True)
    hn = (x * lax.rsqrt(ms + RMS_EPS) * g_ref[...]).astype(BF16)

    f = jnp.dot(hn, wf_ref[...], preferred_element_type=F32) + bf_ref[...]
    z = f.T[0:N_HEADS, :]
    logf = (jnp.minimum(z, 0.0) - jnp.log1p(jnp.exp(-jnp.abs(z)))) * LOG2_E
    parts = jnp.concatenate(_split3(logf), axis=0).astype(BF16)
    r_i = lax.broadcasted_iota(jnp.int32, (tm, tm), 0)
    c_i = lax.broadcasted_iota(jnp.int32, (tm, tm), 1)
    tri = jnp.where(r_i <= c_i, 1.0, 0.0).astype(BF16)
    cs3 = jnp.dot(parts, tri, preferred_element_type=F32)
    cs = cs3[0:8] + cs3[8:16] + cs3[16:24]
    fc = jnp.concatenate([fcarry[...]] * (tm // LANES), axis=1) + cs
    last = jnp.broadcast_to(fc[:, n_valid - 1:n_valid], (N_HEADS, LANES))
    fcarry[...] = last
    ftail_ref[...] = last

    pieces = jnp.concatenate(
        _split3(fc) + (jnp.ones((8, tm), F32), jnp.zeros((LANES - 32, tm), F32)),
        axis=0).astype(BF16)
    fq = jnp.dot(selq_ref[...], pieces, preferred_element_type=F32)
    fk = jnp.dot(selk_ref[...], pieces, preferred_element_type=F32)

    qkv = jnp.dot(hn, wqkv_ref[...], preferred_element_type=F32)
    qt = qkv[:, 0:ATTN_WIDTH].T
    kt = qkv[:, ATTN_WIDTH:2 * ATTN_WIDTH].T
    vt_ref[...] = qkv[:, 2 * ATTN_WIDTH:3 * ATTN_WIDTH].T.astype(BF16)

    def head_norm(src, gain_ref, h):
        xh = src[h * HEAD_DIM:(h + 1) * HEAD_DIM, :]
        ssq = jnp.mean(xh * xh, axis=0, keepdims=True)
        return xh * lax.rsqrt(ssq + RMS_EPS) * gain_ref[...]

    k_pad = jnp.zeros((LANES - HEAD_ROWS, tm), F32)
    for h in range(N_HEADS):
        aug = slice(h * AUG_ROWS, (h + 1) * AUG_ROWS)
        r0 = h * HEAD_ROWS
        qn = head_norm(qt, gq_ref, h)
        qta_ref[r0:r0 + HEAD_DIM, :] = qn.astype(BF16)
        qta_ref[r0 + HEAD_DIM:r0 + HEAD_ROWS, :] = fq[aug, :].astype(BF16)
        kn = head_norm(kt, gk_ref, h)
        self_ref[h:h + 1, :] = jnp.sum(qn * kn, axis=0, keepdims=True)
        k_blk = jnp.concatenate([kn, fk[aug, :], k_pad], axis=0)
        ka_ref[:, h * LANES:(h + 1) * LANES] = k_blk.T.astype(BF16)
        k_sq = jnp.max(jnp.sum(kn * kn, axis=0, keepdims=True), axis=1, keepdims=True)
        kmax_ref[h:h + 1, :] = jnp.broadcast_to(k_sq, (1, LANES))

    u = jnp.dot(hn, wp_ref[...], preferred_element_type=F32)
    ubuf[MAX_WINDOW:MAX_WINDOW + tm, :] = u
    mixed = []
    for g, w in enumerate(POOL_WINDOWS):
        c0 = g * POOL_GROUP_DIM
        acc = u[:, c0:c0 + POOL_GROUP_DIM]
        for s in range(1, w):
            acc = acc + ubuf[MAX_WINDOW - s:MAX_WINDOW - s + tm, c0:c0 + POOL_GROUP_DIM]
        if first_pos + 1 >= w:
            mean = acc * (1.0 / w)
        else:
            pos = first_pos + j * tm + lax.broadcasted_iota(jnp.int32, (tm, POOL_GROUP_DIM), 0)
            mean = acc / jnp.minimum(pos + 1, w).astype(F32)
        mixed.append(mean - u[:, c0:c0 + POOL_GROUP_DIM])
    mixed = jnp.concatenate(mixed, axis=1).astype(BF16)
    y = jnp.dot(mixed, wpool_ref[...], preferred_element_type=F32) * pscale_ref[...]
    pooled_ref[...] = y.astype(BF16)
    tail = u[n_valid - MAX_WINDOW:n_valid, :]
    ubuf[0:MAX_WINDOW, :] = tail
    utail_ref[...] = tail

    gl = jnp.dot(hn, wg_ref[...], preferred_element_type=F32)
    gates_ref[...] = jax.nn.sigmoid(gl).astype(BF16)


def _inproj(x3, p, uprev, fprev, *, tm, n_valid, first_pos):
    nb, seq, _ = x3.shape
    nt = seq // tm
    gq = jnp.broadcast_to((p["gq"] * (LOG2_E / math.sqrt(HEAD_DIM)))[:, None], (HEAD_DIM, tm))
    gk = jnp.broadcast_to(p["gk"][:, None], (HEAD_DIM, tm))
    tok = lambda width: pl.BlockSpec((None, tm, width), lambda b, j: (b, j, 0))
    chan = lambda rows: pl.BlockSpec((None, rows, tm), lambda b, j: (b, 0, j))
    kern = functools.partial(_inproj_kernel, tm=tm, n_valid=n_valid, first_pos=first_pos)
    return pl.pallas_call(
        kern,
        grid=(nb, nt),
        in_specs=[
            tok(D_MODEL),
            _const_spec((1, D_MODEL)),
            _const_spec((D_MODEL, 3 * ATTN_WIDTH)),
            _const_spec((D_MODEL, LANES)),
            _const_spec((1, LANES)),
            _const_spec((D_MODEL, POOL_WIDTH)),
            _const_spec((D_MODEL, 2 * D_MODEL)),
            _const_spec((HEAD_DIM, tm)),
            _const_spec((HEAD_DIM, tm)),
            _const_spec((LANES, LANES)),
            _const_spec((LANES, LANES)),
            _const_spec((POOL_WIDTH, POOL_WIDTH)),
            _const_spec((1, POOL_WIDTH)),
            _const_spec((MAX_WINDOW, POOL_WIDTH)),
            _const_spec((N_HEADS, LANES)),
        ],
        out_specs=[
            chan(QK_ROWS), tok(N_HEADS * LANES), chan(ATTN_WIDTH),
            tok(POOL_WIDTH), tok(2 * D_MODEL),
            pl.BlockSpec((MAX_WINDOW, POOL_WIDTH), lambda b, j: (0, 0)),
            pl.BlockSpec((N_HEADS, LANES), lambda b, j: (0, 0)),
            pl.BlockSpec((None, N_HEADS, LANES), lambda b, j: (b * nt + j, 0, 0)),
            chan(N_HEADS),
        ],
        out_shape=[
            jax.ShapeDtypeStruct((nb, QK_ROWS, seq), BF16),
            jax.ShapeDtypeStruct((nb, seq, N_HEADS * LANES), BF16),
            jax.ShapeDtypeStruct((nb, ATTN_WIDTH, seq), BF16),
            jax.ShapeDtypeStruct((nb, seq, POOL_WIDTH), BF16),
            jax.ShapeDtypeStruct((nb, seq, 2 * D_MODEL), BF16),
            jax.ShapeDtypeStruct((MAX_WINDOW, POOL_WIDTH), F32),
            jax.ShapeDtypeStruct((N_HEADS, LANES), F32),
            jax.ShapeDtypeStruct((nb * nt, N_HEADS, LANES), F32),
            jax.ShapeDtypeStruct((nb, N_HEADS, seq), F32),
        ],
        scratch_shapes=[
            pltpu.VMEM((MAX_WINDOW + tm, POOL_WIDTH), F32),
            pltpu.VMEM((N_HEADS, LANES), F32),
        ],
        compiler_params=pltpu.CompilerParams(
            dimension_semantics=("arbitrary", "arbitrary"), vmem_limit_bytes=VMEM_LIMIT),
        name="inproj",
    )(x3, p["g_mix"], p["wqkv"], p["wf"], p["bf"], p["wp"], p["wg"], gq, gk,
      p["selq"], p["selk"], p["wpool"], p["pscale"], uprev, fprev)


def _attn_kernel(qta_ref, ka_ref, vt_ref, kam_ref, vtm_ref, bias_ref, kmax_ref, self_ref, ot_ref,
                 m_sc, acc_sc, *, tq):
    qi = pl.program_id(1)
    m_sc[...] = jnp.full(m_sc.shape, -jnp.inf, F32)
    acc_sc[...] = jnp.zeros(acc_sc.shape, F32)

    def rows(h):
        return slice(h * HEAD_ROWS, (h + 1) * HEAD_ROWS)

    def vrows(h):
        return slice(h * HEAD_DIM, (h + 1) * HEAD_DIM)

    def klanes(h):
        return slice(h * LANES, (h + 1) * LANES)

    q_pad = jnp.zeros((LANES - HEAD_ROWS, tq), BF16)

    def scores(h, ka):
        qa = jnp.concatenate([qta_ref[rows(h), :], q_pad], axis=0)
        return jnp.dot(ka, qa, preferred_element_type=F32)

    def new_max(h, s):
        return jnp.maximum(m_sc[h], jnp.max(s, axis=0, keepdims=True))

    def absorb(h, s, m_new, v):
        p = jnp.exp2(s - m_new).astype(BF16)
        va = jnp.concatenate([v, jnp.ones((AUG_ROWS, v.shape[1]), BF16)], axis=0)
        acc_sc[h] = (jnp.exp2(m_sc[h] - m_new) * acc_sc[h]
                     + jnp.dot(va, p, preferred_element_type=F32))
        m_sc[h] = m_new

    def sweep_heads(scores_of, values_of):
        s = {0: scores_of(0), 1: scores_of(1)}
        m = {0: new_max(0, s[0])}
        for h in range(N_HEADS):
            if h + 2 < N_HEADS:
                s[h + 2] = scores_of(h + 2)
            if h + 1 < N_HEADS:
                m[h + 1] = new_max(h + 1, s[h + 1])
            absorb(h, s.pop(h), m.pop(h), values_of(h))

    def sweep_heads_bounded(scores_of, values_of):
        s_next = scores_of(0)
        for h in range(N_HEADS):
            s = s_next
            if h + 1 < N_HEADS:
                s_next = scores_of(h + 1)
            p = jnp.exp2(s - m_sc[h]).astype(BF16)
            v = values_of(h)
            va = jnp.concatenate([v, jnp.ones((AUG_ROWS, v.shape[1]), BF16)], axis=0)
            acc_sc[h] += jnp.dot(va, p, preferred_element_type=F32)

    d0 = pl.multiple_of(qi * tq, tq)
    diag_scores = lambda h: scores(h, jnp.concatenate(
        [ka_ref[pl.ds(d0, tq), klanes(h)], kam_ref[:, klanes(h)]], axis=0)) + bias_ref[...]
    diag_values = lambda h: jnp.concatenate(
        [vt_ref[vrows(h), pl.ds(d0, tq)], vtm_ref[vrows(h), :]], axis=1)

    def tile_scores(i):
        s0 = pl.multiple_of(i * tq, tq)
        return lambda h: scores(h, ka_ref[pl.ds(s0, tq), klanes(h)])

    def tile_values(i):
        s0 = pl.multiple_of(i * tq, tq)
        return lambda h: vt_ref[vrows(h), pl.ds(s0, tq)]

    worst = None
    for h in range(N_HEADS):
        q = qta_ref[h * HEAD_ROWS:h * HEAD_ROWS + HEAD_DIM, :].astype(F32)
        q_norm = jnp.sqrt(jnp.sum(q * q, axis=0, keepdims=True))
        k_norm = jnp.sqrt(jnp.concatenate([kmax_ref[h:h + 1, :]] * (tq // LANES), axis=1))
        gap = q_norm * k_norm * NORM_SLACK - SAFE_EXP2 - self_ref[h:h + 1, :]
        worst = gap if worst is None else jnp.maximum(worst, gap)
    bounded = jnp.max(worst) <= 0.0

    @pl.when(bounded)
    def _():
        for h in range(N_HEADS):
            m_sc[h] = self_ref[h:h + 1, :]
        sweep_heads_bounded(diag_scores, diag_values)

        def body(i, carry):
            sweep_heads_bounded(tile_scores(i), tile_values(i))
            return carry
        lax.fori_loop(0, qi, body, 0)

    @pl.when(jnp.logical_not(bounded))
    def _():
        sweep_heads(diag_scores, diag_values)

        def body(i, carry):
            sweep_heads(tile_scores(i), tile_values(i))
            return carry
        lax.fori_loop(0, qi, body, 0)

    for h in range(N_HEADS):
        acc = acc_sc[h]
        ot_ref[vrows(h), :] = (acc[0:HEAD_DIM, :] / acc[HEAD_DIM:HEAD_DIM + 1, :]).astype(BF16)


def _attention(qta, ka, vt, ka_m, vt_m, k_sq_max, self_logit, *, tq):
    nb, _, seq = qta.shape
    kw = N_HEADS * LANES
    key = np.arange(tq + LANES)[:, None]
    qry = np.arange(tq)[None, :]
    visible = np.where(key < tq, key <= qry, key - tq < N_META)
    bias = jnp.asarray(np.where(visible, 0.0, -np.inf), F32)
    return pl.pallas_call(
        functools.partial(_attn_kernel, tq=tq),
        grid=(nb, seq // tq),
        in_specs=[
            pl.BlockSpec((None, QK_ROWS, tq), lambda b, i: (b, 0, i)),
            pl.BlockSpec((None, seq, kw), lambda b, i: (b, 0, 0)),
            pl.BlockSpec((None, ATTN_WIDTH, seq), lambda b, i: (b, 0, 0)),
            pl.BlockSpec((None, LANES, kw), lambda b, i: (0, 0, 0)),
            pl.BlockSpec((None, ATTN_WIDTH, LANES), lambda b, i: (0, 0, 0)),
            _const_spec((tq + LANES, tq)),
            pl.BlockSpec((None, N_HEADS, LANES), lambda b, i: (b, 0, 0)),
            pl.BlockSpec((None, N_HEADS, tq), lambda b, i: (b, 0, i)),
        ],
        out_specs=pl.BlockSpec((None, ATTN_WIDTH, tq), lambda b, i: (b, 0, i)),
        out_shape=jax.ShapeDtypeStruct((nb, ATTN_WIDTH, seq), BF16),
        scratch_shapes=[
            pltpu.VMEM((N_HEADS, 1, tq), F32),
            pltpu.VMEM((N_HEADS, HEAD_ROWS, tq), F32),
        ],
        compiler_params=pltpu.CompilerParams(
            dimension_semantics=("arbitrary", "arbitrary"), vmem_limit_bytes=VMEM_LIMIT),
        name="fox_attention",
    )(qta, ka, vt, ka_m, vt_m, bias, k_sq_max, self_logit)


def _post_kernel(ot_ref, pooled_ref, gates_ref, x_ref, wua_ref, wup_ref, wout_ref, g_ref,
                 wr_ref, br_ref, h2_ref, plan_ref, cnt_ref, cnt_sc, *, tm):
    @pl.when((pl.program_id(0) == 0) & (pl.program_id(1) == 0))
    def _():
        cnt_sc[...] = jnp.zeros(cnt_sc.shape, F32)

    y_attn = lax.dot_general(ot_ref[...], wua_ref[...], _TN, preferred_element_type=F32)
    y_pool = jnp.dot(pooled_ref[...], wup_ref[...], preferred_element_type=F32)
    merged = (gates_ref[:, 0:D_MODEL].astype(F32) * y_attn
              + gates_ref[:, D_MODEL:2 * D_MODEL].astype(F32) * y_pool)
    h2 = x_ref[...] + jnp.dot(merged.astype(BF16), wout_ref[...], preferred_element_type=F32)
    h2_ref[...] = h2
    ms = jnp.mean(h2 * h2, axis=-1, keepdims=True)
    hn2 = (h2 * lax.rsqrt(ms + RMS_EPS) * g_ref[...]).astype(BF16)

    logits = jnp.dot(hn2, wr_ref[...], preferred_element_type=F32) + br_ref[...]
    lt = logits.T[0:ROUTER_ROWS, :]
    row = lax.broadcasted_iota(jnp.int32, lt.shape, 0)
    neg_inf = jnp.float32(-jnp.inf)

    def softmax_over(mask):
        z = jnp.where(mask, lt, neg_inf)
        e = jnp.exp(z - jnp.max(z, axis=0, keepdims=True))
        return jnp.where(mask, e / jnp.sum(e, axis=0, keepdims=True), -1.0)

    def argtop(prob):
        top = jnp.max(prob, axis=0, keepdims=True)
        return jnp.min(jnp.where(prob == top, row, LANES), axis=0, keepdims=True)

    g_idx = argtop(softmax_over(row < N_GROUPS))
    e_lo = ROUTER_LANE0 + N_PER_GROUP * g_idx
    e_prob = softmax_over((row >= e_lo) & (row < e_lo + N_PER_GROUP))
    i1 = argtop(e_prob)
    i2 = argtop(jnp.where(row == i1, -1.0, e_prob))
    a = jnp.minimum(i1, i2) - e_lo
    b = jnp.maximum(i1, i2) - e_lo
    pair = ((a * (2 * N_PER_GROUP - 1 - a)) >> 1) + (b - a - 1)
    cls = g_idx * PAIRS_PER_GROUP + pair

    crow = lax.broadcasted_iota(jnp.int32, (LANES, tm), 0)
    member = crow == cls
    onehot = jnp.where(member, 1.0, 0.0)
    r_i = lax.broadcasted_iota(jnp.int32, (tm, tm), 0)
    c_i = lax.broadcasted_iota(jnp.int32, (tm, tm), 1)
    earlier = jnp.where(r_i < c_i, 1.0, 0.0).astype(BF16)
    prefix = jnp.dot(onehot.astype(BF16), earlier, preferred_element_type=F32)
    base = jnp.concatenate([cnt_sc[...]] * (tm // LANES), axis=1)
    rank = jnp.sum(jnp.where(member, prefix + base, 0.0), axis=0, keepdims=True)
    total = cnt_sc[...] + jnp.broadcast_to(jnp.sum(onehot, axis=1, keepdims=True), cnt_sc.shape)
    cnt_sc[...] = total
    cnt_ref[...] = total

    prow = lax.broadcasted_iota(jnp.int32, (8, tm), 0)
    plan_ref[...] = jnp.where(prow == PLAN_CLASS, cls.astype(F32),
                              jnp.where(prow == PLAN_RANK, rank, 0.0))


def _post(ot, pooled, gates, x, p, *, tm):
    nb, seq, _ = x.shape
    nt = seq // tm
    tok = lambda width: pl.BlockSpec((None, tm, width), lambda b, j: (b, j, 0))
    return pl.pallas_call(
        functools.partial(_post_kernel, tm=tm),
        grid=(nb, nt),
        in_specs=[
            pl.BlockSpec((None, ATTN_WIDTH, tm), lambda b, j: (b, 0, j)),
            tok(POOL_WIDTH), tok(2 * D_MODEL), tok(D_MODEL),
            _const_spec((ATTN_WIDTH, D_MODEL)),
            _const_spec((POOL_WIDTH, D_MODEL)),
            _const_spec((D_MODEL, D_MODEL)),
            _const_spec((1, D_MODEL)),
            _const_spec((D_MODEL, LANES)),
            _const_spec((1, LANES)),
        ],
        out_specs=[tok(D_MODEL),
                   pl.BlockSpec((None, 8, tm), lambda b, j: (b * nt + j, 0, 0)),
                   pl.BlockSpec((LANES, LANES), lambda b, j: (0, 0))],
        out_shape=[
            jax.ShapeDtypeStruct((nb, seq, D_MODEL), F32),
            jax.ShapeDtypeStruct((nb * nt, 8, tm), F32),
            jax.ShapeDtypeStruct((LANES, LANES), F32),
        ],
        scratch_shapes=[pltpu.VMEM((LANES, LANES), F32)],
        compiler_params=pltpu.CompilerParams(
            dimension_semantics=("arbitrary", "arbitrary"), vmem_limit_bytes=VMEM_LIMIT),
        name="post_router",
    )(ot, pooled, gates, x, p["wua"], p["wup"], p["wout"], p["g_ffn"], p["wr"], p["br"])


def _to_slots(dst, values):
    n = values.shape[0]
    for c in range(SLOT_ROWS):
        dst[pl.ds(c, n, stride=SLOT_ROWS), :] = values[:, c * LANES:(c + 1) * LANES]


def _from_slots(src, n):
    return jnp.concatenate(
        [src[pl.ds(c, n, stride=SLOT_ROWS), :] for c in range(SLOT_ROWS)], axis=1)


def _dispatch_kernel(pos_ref, h2_ref, xs_hbm, buf, zeros, sem, zsem, *, tm, n_slots):
    j = pl.program_id(0)
    slot = j % 2

    def drain(s):
        pltpu.make_async_copy(buf.at[s], xs_hbm.at[pl.ds(0, tm * SLOT_ROWS), :], sem.at[s]).wait()

    @pl.when(j >= 2)
    def _():
        drain(slot)

    _to_slots(buf.at[slot], h2_ref[...])

    def body(g, c):
        for u in range(DMA_UNROLL):
            r = g * DMA_UNROLL + u
            src = pl.multiple_of(r * SLOT_ROWS, SLOT_ROWS)
            dst = pl.multiple_of(pos_ref[0, r] * SLOT_ROWS, SLOT_ROWS)
            pltpu.make_async_copy(buf.at[slot, pl.ds(src, SLOT_ROWS), :],
                                  xs_hbm.at[pl.ds(dst, SLOT_ROWS), :],
                                  sem.at[slot]).start(priority=u % DMA_QUEUES)
        return c
    lax.fori_loop(0, tm // DMA_UNROLL, body, 0)

    @pl.when(j == pl.num_programs(0) - 1)
    def _():
        zeros[...] = jnp.zeros(zeros.shape, F32)
        pad = pltpu.make_async_copy(
            zeros, xs_hbm.at[pl.ds(n_slots * SLOT_ROWS, MOE_ROWS * SLOT_ROWS), :], zsem.at[0])
        pad.start()
        drain(slot)
        drain(1 - slot)
        pad.wait()


def _dispatch(h2, pos, *, tm):
    t = h2.shape[0]
    nt = t // tm
    return pl.pallas_call(
        functools.partial(_dispatch_kernel, tm=tm, n_slots=t),
        grid=(nt,),
        in_specs=[
            pl.BlockSpec((None, 1, tm), lambda j: (j, 0, 0), memory_space=pltpu.SMEM),
            pl.BlockSpec((tm, D_MODEL), lambda j: (j, 0)),
        ],
        out_specs=pl.BlockSpec(memory_space=pl.ANY),
        out_shape=jax.ShapeDtypeStruct(((t + MOE_ROWS) * SLOT_ROWS, LANES), F32),
        scratch_shapes=[
            pltpu.VMEM((2, tm * SLOT_ROWS, LANES), F32),
            pltpu.VMEM((MOE_ROWS * SLOT_ROWS, LANES), F32),
            pltpu.SemaphoreType.DMA((2,)),
            pltpu.SemaphoreType.DMA((1,)),
        ],
        compiler_params=pltpu.CompilerParams(
            dimension_semantics=("arbitrary",), vmem_limit_bytes=VMEM_LIMIT),
        name="moe_dispatch",
    )(pos.reshape(nt, 1, tm), h2)


def _moe_kernel(start_ref, nrows_ref, ea_ref, eb_ref, nitems_ref,
                g_ref, wr_ref, br_ref, w1a_ref, w3a_ref, w2a_ref, w1b_ref, w3b_ref, w2b_ref, xs_hbm,
                ys_hbm, xbuf, ybuf, isem, osem):
    k = pl.program_id(0)
    n_items = nitems_ref[0]
    slot = k % 2
    rows = MOE_ROWS * SLOT_ROWS

    def window(kk):
        return pl.ds(pl.multiple_of(start_ref[kk] * SLOT_ROWS, SLOT_ROWS), rows)

    def in_copy(kk, s):
        return pltpu.make_async_copy(xs_hbm.at[window(kk), :], xbuf.at[s], isem.at[s])

    def out_copy(kk, s):
        return pltpu.make_async_copy(ybuf.at[s], ys_hbm.at[window(kk), :], osem.at[s])

    @pl.when(k == 0)
    def _():
        in_copy(0, 0).start()
        ybuf[1] = jnp.zeros(ybuf.shape[1:], F32)
        pad = pltpu.make_async_copy(
            ybuf.at[1], ys_hbm.at[pl.ds(ys_hbm.shape[0] - rows, rows), :], osem.at[1])
        pad.start()
        pad.wait()

    @pl.when(k + 1 < n_items)
    def _():
        in_copy(k + 1, 1 - slot).start()

    @pl.when(k < n_items)
    def _():
        in_copy(k, slot).wait()
        ea, eb = ea_ref[k], eb_ref[k]
        sub = MOE_ROWS // MOE_CHAINS
        for chain in range(MOE_CHAINS):
            block = pl.ds(chain * sub * SLOT_ROWS, sub * SLOT_ROWS)
            h2 = _from_slots(xbuf.at[slot, block], sub)
            ms = jnp.mean(h2 * h2, axis=-1, keepdims=True)
            xn = (h2 * lax.rsqrt(ms + RMS_EPS) * g_ref[...]).astype(BF16)

            logits = jnp.dot(xn, wr_ref[...], preferred_element_type=F32) + br_ref[...]
            lane = lax.broadcasted_iota(jnp.int32, logits.shape, 1)
            pick = lambda idx: jnp.sum(jnp.where(lane == idx, logits, 0.0), axis=1,
                                       keepdims=True)
            gl = jnp.where(lane < N_GROUPS, logits, -jnp.inf)
            gmax = jnp.max(gl, axis=1, keepdims=True)
            p_group = (jnp.exp(pick(ea // N_PER_GROUP) - gmax)
                       / jnp.sum(jnp.exp(gl - gmax), axis=1, keepdims=True))
            la, lb = pick(ROUTER_LANE0 + ea), pick(ROUTER_LANE0 + eb)
            row = chain * sub + lax.broadcasted_iota(jnp.int32, (sub, 1), 0)
            valid = row < nrows_ref[k]
            w_lo = jnp.where(valid, p_group / (1.0 + jnp.exp(lb - la)), 0.0)
            w_hi = jnp.where(valid, p_group / (1.0 + jnp.exp(la - lb)), 0.0)

            def expert(w1_ref, w3_ref, w2_ref, weight):
                h1 = jnp.dot(xn, w1_ref[...], preferred_element_type=F32)
                h3 = jnp.dot(xn, w3_ref[...], preferred_element_type=F32)
                hh = h1 * jax.nn.sigmoid(h1) * h3 * weight
                return jnp.dot(hh.astype(BF16), w2_ref[...], preferred_element_type=F32)

            y = h2 + (expert(w1a_ref, w3a_ref, w2a_ref, w_lo)
                      + expert(w1b_ref, w3b_ref, w2b_ref, w_hi))
            _to_slots(ybuf.at[slot, block], y)

        @pl.when(k >= 1)
        def _():
            out_copy(k - 1, 1 - slot).wait()

        out_copy(k, slot).start()

        @pl.when(k == n_items - 1)
        def _():
            out_copy(k, slot).wait()


def _class_experts():
    ea, eb = [], []
    for g in range(N_GROUPS):
        for a in range(N_PER_GROUP):
            for b in range(a + 1, N_PER_GROUP):
                ea.append(g * N_PER_GROUP + a)
                eb.append(g * N_PER_GROUP + b)
    fill = LANES - len(ea)
    return (jnp.array(ea + [ea[-1]] * fill, jnp.int32),
            jnp.array(eb + [eb[-1]] * fill, jnp.int32))


def _combine_kernel(pos_ref, pos_next_ref, ys_hbm, out_ref, buf, sem, *, tm):
    j = pl.program_id(0)
    slot = j % 2

    def start_tile(pos, s):
        def body(g, c):
            for u in range(DMA_UNROLL):
                r = g * DMA_UNROLL + u
                src = pl.multiple_of(pos[0, r] * SLOT_ROWS, SLOT_ROWS)
                dst = pl.multiple_of(r * SLOT_ROWS, SLOT_ROWS)
                pltpu.make_async_copy(ys_hbm.at[pl.ds(src, SLOT_ROWS), :],
                                      buf.at[s, pl.ds(dst, SLOT_ROWS), :],
                                      sem.at[s]).start(priority=u % DMA_QUEUES)
            return c
        lax.fori_loop(0, tm // DMA_UNROLL, body, 0)

    @pl.when(j == 0)
    def _():
        start_tile(pos_ref, 0)

    @pl.when(j + 1 < pl.num_programs(0))
    def _():
        start_tile(pos_next_ref, 1 - slot)

    pltpu.make_async_copy(ys_hbm.at[pl.ds(0, tm * SLOT_ROWS), :], buf.at[slot], sem.at[slot]).wait()
    out_ref[...] = _from_slots(buf.at[slot], tm)


def _combine(ys, pos, *, tm):
    t = pos.shape[0]
    nt = t // tm
    pos3 = pos.reshape(nt, 1, tm)
    smem_tile = lambda fn: pl.BlockSpec((None, 1, tm), fn, memory_space=pltpu.SMEM)
    return pl.pallas_call(
        functools.partial(_combine_kernel, tm=tm),
        grid=(nt,),
        in_specs=[
            smem_tile(lambda j: (j, 0, 0)),
            smem_tile(lambda j: (jnp.minimum(j + 1, nt - 1), 0, 0)),
            pl.BlockSpec(memory_space=pl.ANY),
        ],
        out_specs=pl.BlockSpec((tm, D_MODEL), lambda j: (j, 0)),
        out_shape=jax.ShapeDtypeStruct((t, D_MODEL), F32),
        scratch_shapes=[
            pltpu.VMEM((2, tm * SLOT_ROWS, LANES), F32),
            pltpu.SemaphoreType.DMA((2,)),
        ],
        compiler_params=pltpu.CompilerParams(
            dimension_semantics=("arbitrary",), vmem_limit_bytes=VMEM_LIMIT),
        name="moe_combine",
    )(pos3, pos3, ys)


def _moe(h2, plan, counts, p, *, tm):
    t = h2.shape[0]
    assert t % tm == 0 and t // tm >= 2
    max_items = -(-t // MOE_ROWS) + N_CLASSES
    class_ids = jnp.arange(LANES, dtype=jnp.int32)
    c_end = jnp.cumsum(counts)
    c_start = c_end - counts
    cls = plan[:, PLAN_CLASS, :].reshape(t).astype(jnp.int32)
    rank = plan[:, PLAN_RANK, :].reshape(t).astype(jnp.int32)
    pos = jnp.sum(jnp.where(cls[:, None] == class_ids[None, :], c_start[None, :], 0),
                  axis=1) + rank
    n_items_c = (counts + MOE_ROWS - 1) // MOE_ROWS
    item_end = jnp.cumsum(n_items_c)
    item_start = item_end - n_items_c
    n_items = item_end[-1]
    item_ids = jnp.minimum(jnp.arange(max_items, dtype=jnp.int32), n_items - 1)
    icls = jnp.sum(item_end[None, :] <= item_ids[:, None], axis=1, dtype=jnp.int32)
    pick = lambda table: jnp.sum(
        jnp.where(icls[:, None] == class_ids[None, :], table[None, :], 0), axis=1)
    offset = (item_ids - pick(item_start)) * MOE_ROWS
    start = pick(c_start) + offset
    nrows = jnp.minimum(pick(counts) - offset, MOE_ROWS)
    ea_tab, eb_tab = _class_experts()
    ea, eb = pick(ea_tab), pick(eb_tab)

    xs = _dispatch(h2, pos, tm=tm)

    first = lambda ea, eb: ea
    second = lambda ea, eb: eb
    w13 = lambda sel: pl.BlockSpec((None, D_MODEL, EXPERT_HIDDEN),
                                   lambda k, st, nr, ea, eb, n: (sel(ea, eb)[k], 0, 0))
    w2 = lambda sel: pl.BlockSpec((None, EXPERT_HIDDEN, D_MODEL),
                                  lambda k, st, nr, ea, eb, n: (sel(ea, eb)[k], 0, 0))
    const = lambda shape: pl.BlockSpec(shape, lambda k, *_: (0, 0))
    ys = pl.pallas_call(
        _moe_kernel,
        grid_spec=pltpu.PrefetchScalarGridSpec(
            num_scalar_prefetch=5,
            grid=(max_items,),
            in_specs=[
                const((1, D_MODEL)), const((D_MODEL, LANES)), const((1, LANES)),
                w13(first), w13(first), w2(first), w13(second), w13(second), w2(second),
                pl.BlockSpec(memory_space=pl.ANY),
            ],
            out_specs=pl.BlockSpec(memory_space=pl.ANY),
            scratch_shapes=[
                pltpu.VMEM((2, MOE_ROWS * SLOT_ROWS, LANES), F32),
                pltpu.VMEM((2, MOE_ROWS * SLOT_ROWS, LANES), F32),
                pltpu.SemaphoreType.DMA((2,)),
                pltpu.SemaphoreType.DMA((2,)),
            ],
        ),
        out_shape=jax.ShapeDtypeStruct(((t + MOE_ROWS) * SLOT_ROWS, LANES), F32),
        compiler_params=pltpu.CompilerParams(
            dimension_semantics=("arbitrary",), vmem_limit_bytes=VMEM_LIMIT),
        name="moe",
    )(start, nrows, ea, eb, n_items.reshape(1), p["g_ffn"], p["wr"], p["br"],
      p["w1"], p["w3"], p["w2"], p["w1"], p["w3"], p["w2"], xs)
    return _combine(ys, pos, tm=tm)


def _bias_selectors():
    selq = [[0.0] * LANES for _ in range(LANES)]
    selk = [[0.0] * LANES for _ in range(LANES)]
    ones_row = 3 * N_HEADS
    for h in range(N_HEADS):
        for part in range(3):
            selq[h * AUG_ROWS + part][part * N_HEADS + h] = 1.0
            selq[h * AUG_ROWS + 3 + part][ones_row] = 1.0
            selk[h * AUG_ROWS + part][ones_row] = 1.0
            selk[h * AUG_ROWS + 3 + part][part * N_HEADS + h] = -1.0
    return jnp.array(selq, BF16), jnp.array(selk, BF16)


def _prepare_params(norm_mix_g, w_in, b_forget, q_norm_g, k_norm_g, w_up_attn, w_pool,
                    pool_scale, w_up_pool, w_out, norm_ffn_g, w_group, b_group, w_router,
                    b_router, w1, w3, w2):
    aw, pw = ATTN_WIDTH, POOL_WIDTH
    f_off = 3 * aw
    p_off = f_off + N_HEADS
    g_off = p_off + pw
    selq, selk = _bias_selectors()
    pad_lanes = lambda a: jnp.pad(a, ((0, 0), (0, LANES - a.shape[1])))
    wpool = jnp.zeros((pw, pw), F32)
    for g in range(len(POOL_WINDOWS)):
        sl = slice(g * POOL_GROUP_DIM, (g + 1) * POOL_GROUP_DIM)
        wpool = wpool.at[sl, sl].set(w_pool[g])
    d, f = D_MODEL, EXPERT_HIDDEN
    return {
        "g_mix": norm_mix_g.reshape(1, d),
        "wqkv": w_in[:, 0:f_off].astype(BF16),
        "wf": pad_lanes(w_in[:, f_off:p_off]).astype(BF16),
        "bf": pad_lanes(b_forget.reshape(1, N_HEADS)),
        "wp": w_in[:, p_off:g_off].astype(BF16),
        "wg": w_in[:, g_off:].astype(BF16),
        "gq": q_norm_g, "gk": k_norm_g,
        "selq": selq, "selk": selk,
        "wpool": wpool.astype(BF16),
        "pscale": pool_scale.reshape(1, pw),
        "wua": w_up_attn.astype(BF16),
        "wup": w_up_pool.astype(BF16),
        "wout": w_out.astype(BF16),
        "g_ffn": norm_ffn_g.reshape(1, d),
        "wr": pad_lanes(jnp.concatenate([w_group, w_router], axis=1)).astype(BF16),
        "br": pad_lanes(jnp.concatenate([b_group, b_router]).reshape(1, -1)),
        "w1": w1.reshape(N_EXPERTS, d, f).astype(BF16),
        "w3": w3.reshape(N_EXPERTS, d, f).astype(BF16),
        "w2": w2.reshape(N_EXPERTS, f, d).astype(BF16),
    }


def kernel(x, meta_tokens, norm_mix_g, w_in, b_forget, q_norm_g, k_norm_g, w_up_attn, w_pool,
           pool_scale, w_up_pool, w_out, norm_ffn_g, w_group, b_group, w_router, b_router,
           w1, w3, w2):
    nb, seq, d = x.shape
    p = _prepare_params(norm_mix_g[0], w_in[0], b_forget[0], q_norm_g[0], k_norm_g[0],
                        w_up_attn[0], w_pool[0], pool_scale[0], w_up_pool[0], w_out[0],
                        norm_ffn_g[0], w_group[0], b_group[0], w_router[0], b_router[0],
                        w1[0], w3[0], w2[0])

    meta = jnp.pad(meta_tokens.astype(x.dtype), ((0, LANES - N_META), (0, 0)))[None]
    _, ka_m, vt_m, _, _, u_meta, f_meta, _, _ = _inproj(
        meta, p, jnp.zeros((MAX_WINDOW, POOL_WIDTH), F32), jnp.zeros((N_HEADS, LANES), F32),
        tm=LANES, n_valid=N_META, first_pos=0)

    tm = 512
    qta, ka, vt, pooled, gates, _, _, k_sq, self_logit = _inproj(
        x, p, u_meta, f_meta, tm=tm, n_valid=tm, first_pos=N_META)
    k_sq_max = jnp.max(k_sq.reshape(nb, seq // tm, N_HEADS, LANES), axis=1)
    ot = _attention(qta, ka, vt, ka_m, vt_m, k_sq_max, self_logit, tq=512)
    h2, plan, cnt = _post(ot, pooled, gates, x, p, tm=512)
    out = _moe(h2.reshape(nb * seq, d), plan, cnt[:, 0].astype(jnp.int32), p, tm=512)
    return out.reshape(nb, seq, d)
```

```python
import functools
import math

import jax
import jax.numpy as jnp
import numpy as np
from jax import lax
from jax.experimental import pallas as pl
from jax.experimental.pallas import tpu as pltpu

F32 = jnp.float32
BF16 = jnp.bfloat16

D_MODEL = 1024
N_META = 16
N_HEADS = 8
HEAD_DIM = 64
ATTN_WIDTH = N_HEADS * HEAD_DIM
POOL_WINDOWS = (2, 4, 8, 16)
POOL_WIDTH = 512
POOL_GROUP_DIM = POOL_WIDTH // len(POOL_WINDOWS)
MAX_WINDOW = max(POOL_WINDOWS)
N_GROUPS = 4
N_PER_GROUP = 8
N_EXPERTS = N_GROUPS * N_PER_GROUP
EXPERT_HIDDEN = 256
RMS_EPS = 1e-6
LOG2_E = math.log2(math.e)

LANES = 128
AUG_ROWS = 16
HEAD_ROWS = HEAD_DIM + AUG_ROWS
QK_ROWS = N_HEADS * HEAD_ROWS
ROUTER_LANE0 = N_GROUPS
PAIRS_PER_GROUP = N_PER_GROUP * (N_PER_GROUP - 1) // 2
N_CLASSES = N_GROUPS * PAIRS_PER_GROUP
ROUTER_ROWS = 40
PLAN_CLASS, PLAN_RANK = 0, 1
MOE_ROWS = 352
SLOT_ROWS = D_MODEL // LANES
DMA_UNROLL = 8
DMA_QUEUES = 2
MOE_CHAINS = 2
INPROJ_CHAINS = 1
POST_CHAINS = 1
SAFE_EXP2 = 64.0
NORM_SLACK = 1.02
VMEM_LIMIT = 56 * 1024 * 1024

_TN = (((0,), (0,)), ((), ()))


def _const_spec(shape):
    zeros = (0,) * len(shape)
    return pl.BlockSpec(shape, lambda *_: zeros, pipeline_mode=pl.Buffered(1))


def _split3(x):
    hi = x.astype(BF16).astype(F32)
    r = x - hi
    mid = r.astype(BF16).astype(F32)
    lo = (r - mid).astype(BF16).astype(F32)
    return hi, mid, lo


def _inproj_kernel(x_ref, g_ref, wqkv_ref, wf_ref, bf_ref, wp_ref, wg_ref, gq_ref, gk_ref,
                   selq_ref, selk_ref, wpool_ref, pscale_ref, uprev_ref, fprev_ref,
                   qta_ref, ka_ref, vt_ref, pooled_ref, gates_ref, utail_ref, ftail_ref, kmax_ref,
                   self_ref,
                   ubuf, fcarry, *, tm, n_valid, first_pos, chains):
    j = pl.program_id(1)

    @pl.when(j == 0)
    def _():
        ubuf[0:MAX_WINDOW, :] = uprev_ref[...]
        fcarry[...] = fprev_ref[...]

    sub = tm // chains
    r_i = lax.broadcasted_iota(jnp.int32, (sub, sub), 0)
    c_i = lax.broadcasted_iota(jnp.int32, (sub, sub), 1)
    tri = jnp.where(r_i <= c_i, 1.0, 0.0).astype(BF16)
    k_pad = jnp.zeros((LANES - HEAD_ROWS, sub), F32)
    carry = fcarry[...]
    k_sq_max = [None] * N_HEADS

    def head_norm(src, gain_ref, h):
        xh = src[h * HEAD_DIM:(h + 1) * HEAD_DIM, :]
        ssq = jnp.mean(xh * xh, axis=0, keepdims=True)
        return xh * lax.rsqrt(ssq + RMS_EPS) * gain_ref[:, 0:sub]

    for chain in range(chains):
        t0 = chain * sub
        toks = slice(t0, t0 + sub)
        x = x_ref[toks, :]
        ms = jnp.mean(x * x, axis=-1, keepdims=True)
        hn = (x * lax.rsqrt(ms + RMS_EPS) * g_ref[...]).astype(BF16)

        f = jnp.dot(hn, wf_ref[...], preferred_element_type=F32) + bf_ref[...]
        z = f.T[0:N_HEADS, :]
        logf = (jnp.minimum(z, 0.0) - jnp.log1p(jnp.exp(-jnp.abs(z)))) * LOG2_E
        parts = jnp.concatenate(_split3(logf), axis=0).astype(BF16)
        cs3 = jnp.dot(parts, tri, preferred_element_type=F32)
        cs = cs3[0:8] + cs3[8:16] + cs3[16:24]
        fc = jnp.concatenate([carry] * (sub // LANES), axis=1) + cs
        if n_valid - 1 >= t0:
            last_col = min(n_valid, t0 + sub) - 1 - t0
            carry = jnp.broadcast_to(fc[:, last_col:last_col + 1], (N_HEADS, LANES))

        pieces = jnp.concatenate(
            _split3(fc) + (jnp.ones((8, sub), F32), jnp.zeros((LANES - 32, sub), F32)),
            axis=0).astype(BF16)
        fq = jnp.dot(selq_ref[...], pieces, preferred_element_type=F32)
        fk = jnp.dot(selk_ref[...], pieces, preferred_element_type=F32)

        qkv = jnp.dot(hn, wqkv_ref[...], preferred_element_type=F32)
        qt = qkv[:, 0:ATTN_WIDTH].T
        kt = qkv[:, ATTN_WIDTH:2 * ATTN_WIDTH].T
        vt_ref[:, toks] = qkv[:, 2 * ATTN_WIDTH:3 * ATTN_WIDTH].T.astype(BF16)

        for h in range(N_HEADS):
            aug = slice(h * AUG_ROWS, (h + 1) * AUG_ROWS)
            r0 = h * HEAD_ROWS
            qn = head_norm(qt, gq_ref, h)
            qta_ref[r0:r0 + HEAD_DIM, toks] = qn.astype(BF16)
            qta_ref[r0 + HEAD_DIM:r0 + HEAD_ROWS, toks] = fq[aug, :].astype(BF16)
            kn = head_norm(kt, gk_ref, h)
            self_ref[h:h + 1, toks] = jnp.sum(qn * kn, axis=0, keepdims=True)
            k_blk = jnp.concatenate([kn, fk[aug, :], k_pad], axis=0)
            ka_ref[toks, h * LANES:(h + 1) * LANES] = k_blk.T.astype(BF16)
            k_sq = jnp.max(jnp.sum(kn * kn, axis=0, keepdims=True), axis=1, keepdims=True)
            k_sq_max[h] = k_sq if k_sq_max[h] is None else jnp.maximum(k_sq_max[h], k_sq)

        u = jnp.dot(hn, wp_ref[...], preferred_element_type=F32)
        u0 = MAX_WINDOW + t0
        ubuf[u0:u0 + sub, :] = u
        mixed = []
        for g, w in enumerate(POOL_WINDOWS):
            c0 = g * POOL_GROUP_DIM
            acc = u[:, c0:c0 + POOL_GROUP_DIM]
            for s in range(1, w):
                acc = acc + ubuf[u0 - s:u0 - s + sub, c0:c0 + POOL_GROUP_DIM]
            if first_pos + 1 >= w:
                mean = acc * (1.0 / w)
            else:
                pos = (first_pos + j * tm + t0
                       + lax.broadcasted_iota(jnp.int32, (sub, POOL_GROUP_DIM), 0))
                mean = acc / jnp.minimum(pos + 1, w).astype(F32)
            mixed.append(mean - u[:, c0:c0 + POOL_GROUP_DIM])
        mixed = jnp.concatenate(mixed, axis=1).astype(BF16)
        y = jnp.dot(mixed, wpool_ref[...], preferred_element_type=F32) * pscale_ref[...]
        pooled_ref[toks, :] = y.astype(BF16)

        gl = jnp.dot(hn, wg_ref[...], preferred_element_type=F32)
        gates_ref[toks, :] = jax.nn.sigmoid(gl).astype(BF16)

    fcarry[...] = carry
    ftail_ref[...] = carry
    for h in range(N_HEADS):
        kmax_ref[h:h + 1, :] = jnp.broadcast_to(k_sq_max[h], (1, LANES))
    tail = ubuf[n_valid:n_valid + MAX_WINDOW, :]
    ubuf[0:MAX_WINDOW, :] = tail
    utail_ref[...] = tail


def _inproj(x3, p, uprev, fprev, *, tm, n_valid, first_pos, chains):
    nb, seq, _ = x3.shape
    nt = seq // tm
    gq = jnp.broadcast_to((p["gq"] * (LOG2_E / math.sqrt(HEAD_DIM)))[:, None], (HEAD_DIM, tm))
    gk = jnp.broadcast_to(p["gk"][:, None], (HEAD_DIM, tm))
    tok = lambda width: pl.BlockSpec((None, tm, width), lambda b, j: (b, j, 0))
    chan = lambda rows: pl.BlockSpec((None, rows, tm), lambda b, j: (b, 0, j))
    kern = functools.partial(_inproj_kernel, tm=tm, n_valid=n_valid, first_pos=first_pos,
                             chains=chains)
    return pl.pallas_call(
        kern,
        grid=(nb, nt),
        in_specs=[
            tok(D_MODEL),
            _const_spec((1, D_MODEL)),
            _const_spec((D_MODEL, 3 * ATTN_WIDTH)),
            _const_spec((D_MODEL, LANES)),
            _const_spec((1, LANES)),
            _const_spec((D_MODEL, POOL_WIDTH)),
            _const_spec((D_MODEL, 2 * D_MODEL)),
            _const_spec((HEAD_DIM, tm)),
            _const_spec((HEAD_DIM, tm)),
            _const_spec((LANES, LANES)),
            _const_spec((LANES, LANES)),
            _const_spec((POOL_WIDTH, POOL_WIDTH)),
            _const_spec((1, POOL_WIDTH)),
            _const_spec((MAX_WINDOW, POOL_WIDTH)),
            _const_spec((N_HEADS, LANES)),
        ],
        out_specs=[
            chan(QK_ROWS), tok(N_HEADS * LANES), chan(ATTN_WIDTH),
            tok(POOL_WIDTH), tok(2 * D_MODEL),
            pl.BlockSpec((MAX_WINDOW, POOL_WIDTH), lambda b, j: (0, 0)),
            pl.BlockSpec((N_HEADS, LANES), lambda b, j: (0, 0)),
            pl.BlockSpec((None, N_HEADS, LANES), lambda b, j: (b * nt + j, 0, 0)),
            chan(N_HEADS),
        ],
        out_shape=[
            jax.ShapeDtypeStruct((nb, QK_ROWS, seq), BF16),
            jax.ShapeDtypeStruct((nb, seq, N_HEADS * LANES), BF16),
            jax.ShapeDtypeStruct((nb, ATTN_WIDTH, seq), BF16),
            jax.ShapeDtypeStruct((nb, seq, POOL_WIDTH), BF16),
            jax.ShapeDtypeStruct((nb, seq, 2 * D_MODEL), BF16),
            jax.ShapeDtypeStruct((MAX_WINDOW, POOL_WIDTH), F32),
            jax.ShapeDtypeStruct((N_HEADS, LANES), F32),
            jax.ShapeDtypeStruct((nb * nt, N_HEADS, LANES), F32),
            jax.ShapeDtypeStruct((nb, N_HEADS, seq), F32),
        ],
        scratch_shapes=[
            pltpu.VMEM((MAX_WINDOW + tm, POOL_WIDTH), F32),
            pltpu.VMEM((N_HEADS, LANES), F32),
        ],
        compiler_params=pltpu.CompilerParams(
            dimension_semantics=("arbitrary", "arbitrary"), vmem_limit_bytes=VMEM_LIMIT),
        name="inproj",
    )(x3, p["g_mix"], p["wqkv"], p["wf"], p["bf"], p["wp"], p["wg"], gq, gk,
      p["selq"], p["selk"], p["wpool"], p["pscale"], uprev, fprev)


def _attn_kernel(qta_ref, ka_ref, vt_ref, kam_ref, vtm_ref, bias_ref, kmax_ref, self_ref, ot_ref,
                 m_sc, acc_sc, *, tq):
    qi = pl.program_id(1)
    m_sc[...] = jnp.full(m_sc.shape, -jnp.inf, F32)
    acc_sc[...] = jnp.zeros(acc_sc.shape, F32)

    def rows(h):
        return slice(h * HEAD_ROWS, (h + 1) * HEAD_ROWS)

    def vrows(h):
        return slice(h * HEAD_DIM, (h + 1) * HEAD_DIM)

    def klanes(h):
        return slice(h * LANES, (h + 1) * LANES)

    q_pad = jnp.zeros((LANES - HEAD_ROWS, tq), BF16)

    def scores(h, ka):
        qa = jnp.concatenate([qta_ref[rows(h), :], q_pad], axis=0)
        return jnp.dot(ka, qa, preferred_element_type=F32)

    def new_max(h, s):
        return jnp.maximum(m_sc[h], jnp.max(s, axis=0, keepdims=True))

    def absorb(h, s, m_new, v):
        p = jnp.exp2(s - m_new).astype(BF16)
        va = jnp.concatenate([v, jnp.ones((AUG_ROWS, v.shape[1]), BF16)], axis=0)
        acc_sc[h] = (jnp.exp2(m_sc[h] - m_new) * acc_sc[h]
                     + jnp.dot(va, p, preferred_element_type=F32))
        m_sc[h] = m_new

    def sweep_heads(scores_of, values_of):
        s = {0: scores_of(0), 1: scores_of(1)}
        m = {0: new_max(0, s[0])}
        for h in range(N_HEADS):
            if h + 2 < N_HEADS:
                s[h + 2] = scores_of(h + 2)
            if h + 1 < N_HEADS:
                m[h + 1] = new_max(h + 1, s[h + 1])
            absorb(h, s.pop(h), m.pop(h), values_of(h))

    def sweep_heads_bounded(*tiles):
        items = [(h, s_of, v_of) for s_of, v_of in tiles for h in range(N_HEADS)]
        s_next = items[0][1](items[0][0])
        for n, (h, _, values_of) in enumerate(items):
            s = s_next
            if n + 1 < len(items):
                s_next = items[n + 1][1](items[n + 1][0])
            p = jnp.exp2(s - m_sc[h]).astype(BF16)
            v = values_of(h)
            va = jnp.concatenate([v, jnp.ones((AUG_ROWS, v.shape[1]), BF16)], axis=0)
            acc_sc[h] += jnp.dot(va, p, preferred_element_type=F32)

    d0 = pl.multiple_of(qi * tq, tq)
    diag_scores = lambda h: scores(h, jnp.concatenate(
        [ka_ref[pl.ds(d0, tq), klanes(h)], kam_ref[:, klanes(h)]], axis=0)) + bias_ref[...]
    diag_values = lambda h: jnp.concatenate(
        [vt_ref[vrows(h), pl.ds(d0, tq)], vtm_ref[vrows(h), :]], axis=1)

    def tile_scores(i):
        s0 = pl.multiple_of(i * tq, tq)
        return lambda h: scores(h, ka_ref[pl.ds(s0, tq), klanes(h)])

    def tile_values(i):
        s0 = pl.multiple_of(i * tq, tq)
        return lambda h: vt_ref[vrows(h), pl.ds(s0, tq)]

    worst = None
    for h in range(N_HEADS):
        q = qta_ref[h * HEAD_ROWS:h * HEAD_ROWS + HEAD_DIM, :].astype(F32)
        q_norm = jnp.sqrt(jnp.sum(q * q, axis=0, keepdims=True))
        k_norm = jnp.sqrt(jnp.concatenate([kmax_ref[h:h + 1, :]] * (tq // LANES), axis=1))
        gap = q_norm * k_norm * NORM_SLACK - SAFE_EXP2 - self_ref[h:h + 1, :]
        worst = gap if worst is None else jnp.maximum(worst, gap)
    bounded = jnp.max(worst) <= 0.0

    @pl.when(bounded)
    def _():
        for h in range(N_HEADS):
            m_sc[h] = self_ref[h:h + 1, :]
        sweep_heads_bounded((diag_scores, diag_values))

        def body(i, carry):
            sweep_heads_bounded((tile_scores(2 * i), tile_values(2 * i)),
                                (tile_scores(2 * i + 1), tile_values(2 * i + 1)))
            return carry
        lax.fori_loop(0, qi // 2, body, 0)

        @pl.when(qi % 2 == 1)
        def _():
            sweep_heads_bounded((tile_scores(qi - 1), tile_values(qi - 1)))

    @pl.when(jnp.logical_not(bounded))
    def _():
        sweep_heads(diag_scores, diag_values)

        def body(i, carry):
            sweep_heads(tile_scores(i), tile_values(i))
            return carry
        lax.fori_loop(0, qi, body, 0)

    for h in range(N_HEADS):
        acc = acc_sc[h]
        ot_ref[vrows(h), :] = (acc[0:HEAD_DIM, :] / acc[HEAD_DIM:HEAD_DIM + 1, :]).astype(BF16)


def _attention(qta, ka, vt, ka_m, vt_m, k_sq_max, self_logit, *, tq):
    nb, _, seq = qta.shape
    kw = N_HEADS * LANES
    key = np.arange(tq + LANES)[:, None]
    qry = np.arange(tq)[None, :]
    visible = np.where(key < tq, key <= qry, key - tq < N_META)
    bias = jnp.asarray(np.where(visible, 0.0, -np.inf), F32)
    return pl.pallas_call(
        functools.partial(_attn_kernel, tq=tq),
        grid=(nb, seq // tq),
        in_specs=[
            pl.BlockSpec((None, QK_ROWS, tq), lambda b, i: (b, 0, i)),
            pl.BlockSpec((None, seq, kw), lambda b, i: (b, 0, 0)),
            pl.BlockSpec((None, ATTN_WIDTH, seq), lambda b, i: (b, 0, 0)),
            pl.BlockSpec((None, LANES, kw), lambda b, i: (0, 0, 0)),
            pl.BlockSpec((None, ATTN_WIDTH, LANES), lambda b, i: (0, 0, 0)),
            _const_spec((tq + LANES, tq)),
            pl.BlockSpec((None, N_HEADS, LANES), lambda b, i: (b, 0, 0)),
            pl.BlockSpec((None, N_HEADS, tq), lambda b, i: (b, 0, i)),
        ],
        out_specs=pl.BlockSpec((None, ATTN_WIDTH, tq), lambda b, i: (b, 0, i)),
        out_shape=jax.ShapeDtypeStruct((nb, ATTN_WIDTH, seq), BF16),
        scratch_shapes=[
            pltpu.VMEM((N_HEADS, 1, tq), F32),
            pltpu.VMEM((N_HEADS, HEAD_ROWS, tq), F32),
        ],
        compiler_params=pltpu.CompilerParams(
            dimension_semantics=("arbitrary", "arbitrary"), vmem_limit_bytes=VMEM_LIMIT),
        name="fox_attention",
    )(qta, ka, vt, ka_m, vt_m, bias, k_sq_max, self_logit)


def _post_kernel(ot_ref, pooled_ref, gates_ref, x_ref, wua_ref, wup_ref, wout_ref, g_ref,
                 wr_ref, br_ref, h2_ref, plan_ref, cnt_ref, cnt_sc, *, tm):
    @pl.when((pl.program_id(0) == 0) & (pl.program_id(1) == 0))
    def _():
        cnt_sc[...] = jnp.zeros(cnt_sc.shape, F32)

    sub = tm // POST_CHAINS
    row = lax.broadcasted_iota(jnp.int32, (ROUTER_ROWS, sub), 0)
    crow = lax.broadcasted_iota(jnp.int32, (LANES, sub), 0)
    prow = lax.broadcasted_iota(jnp.int32, (8, sub), 0)
    r_i = lax.broadcasted_iota(jnp.int32, (sub, sub), 0)
    c_i = lax.broadcasted_iota(jnp.int32, (sub, sub), 1)
    earlier = jnp.where(r_i < c_i, 1.0, 0.0).astype(BF16)
    neg_inf = jnp.float32(-jnp.inf)
    counts = cnt_sc[...]

    for chain in range(POST_CHAINS):
        toks = slice(chain * sub, (chain + 1) * sub)
        y_attn = lax.dot_general(ot_ref[:, toks], wua_ref[...], _TN, preferred_element_type=F32)
        y_pool = jnp.dot(pooled_ref[toks, :], wup_ref[...], preferred_element_type=F32)
        merged = (gates_ref[toks, 0:D_MODEL].astype(F32) * y_attn
                  + gates_ref[toks, D_MODEL:2 * D_MODEL].astype(F32) * y_pool)
        h2 = x_ref[toks, :] + jnp.dot(merged.astype(BF16), wout_ref[...],
                                      preferred_element_type=F32)
        h2_ref[toks, :] = h2
        ms = jnp.mean(h2 * h2, axis=-1, keepdims=True)
        hn2 = (h2 * lax.rsqrt(ms + RMS_EPS) * g_ref[...]).astype(BF16)

        logits = jnp.dot(hn2, wr_ref[...], preferred_element_type=F32) + br_ref[...]
        lt = logits.T[0:ROUTER_ROWS, :]

        def softmax_over(mask):
            z = jnp.where(mask, lt, neg_inf)
            e = jnp.exp(z - jnp.max(z, axis=0, keepdims=True))
            return jnp.where(mask, e / jnp.sum(e, axis=0, keepdims=True), -1.0)

        def argtop(prob):
            top = jnp.max(prob, axis=0, keepdims=True)
            return jnp.min(jnp.where(prob == top, row, LANES), axis=0, keepdims=True)

        g_idx = argtop(softmax_over(row < N_GROUPS))
        e_lo = ROUTER_LANE0 + N_PER_GROUP * g_idx
        e_prob = softmax_over((row >= e_lo) & (row < e_lo + N_PER_GROUP))
        i1 = argtop(e_prob)
        i2 = argtop(jnp.where(row == i1, -1.0, e_prob))
        a = jnp.minimum(i1, i2) - e_lo
        b = jnp.maximum(i1, i2) - e_lo
        pair = ((a * (2 * N_PER_GROUP - 1 - a)) >> 1) + (b - a - 1)
        cls = g_idx * PAIRS_PER_GROUP + pair

        member = crow == cls
        onehot = jnp.where(member, 1.0, 0.0)
        prefix = jnp.dot(onehot.astype(BF16), earlier, preferred_element_type=F32)
        base = jnp.concatenate([counts] * (sub // LANES), axis=1)
        rank = jnp.sum(jnp.where(member, prefix + base, 0.0), axis=0, keepdims=True)
        counts = counts + jnp.broadcast_to(jnp.sum(onehot, axis=1, keepdims=True), counts.shape)
        plan_ref[:, toks] = jnp.where(prow == PLAN_CLASS, cls.astype(F32),
                                      jnp.where(prow == PLAN_RANK, rank, 0.0))

    cnt_sc[...] = counts
    cnt_ref[...] = counts


def _post(ot, pooled, gates, x, p, *, tm):
    nb, seq, _ = x.shape
    nt = seq // tm
    tok = lambda width: pl.BlockSpec((None, tm, width), lambda b, j: (b, j, 0))
    return pl.pallas_call(
        functools.partial(_post_kernel, tm=tm),
        grid=(nb, nt),
        in_specs=[
            pl.BlockSpec((None, ATTN_WIDTH, tm), lambda b, j: (b, 0, j)),
            tok(POOL_WIDTH), tok(2 * D_MODEL), tok(D_MODEL),
            _const_spec((ATTN_WIDTH, D_MODEL)),
            _const_spec((POOL_WIDTH, D_MODEL)),
            _const_spec((D_MODEL, D_MODEL)),
            _const_spec((1, D_MODEL)),
            _const_spec((D_MODEL, LANES)),
            _const_spec((1, LANES)),
        ],
        out_specs=[tok(D_MODEL),
                   pl.BlockSpec((None, 8, tm), lambda b, j: (b * nt + j, 0, 0)),
                   pl.BlockSpec((LANES, LANES), lambda b, j: (0, 0))],
        out_shape=[
            jax.ShapeDtypeStruct((nb, seq, D_MODEL), F32),
            jax.ShapeDtypeStruct((nb * nt, 8, tm), F32),
            jax.ShapeDtypeStruct((LANES, LANES), F32),
        ],
        scratch_shapes=[pltpu.VMEM((LANES, LANES), F32)],
        compiler_params=pltpu.CompilerParams(
            dimension_semantics=("arbitrary", "arbitrary"), vmem_limit_bytes=VMEM_LIMIT),
        name="post_router",
    )(ot, pooled, gates, x, p["wua"], p["wup"], p["wout"], p["g_ffn"], p["wr"], p["br"])


def _to_slots(dst, values):
    n = values.shape[0]
    for c in range(SLOT_ROWS):
        dst[pl.ds(c, n, stride=SLOT_ROWS), :] = values[:, c * LANES:(c + 1) * LANES]


def _from_slots(src, n):
    return jnp.concatenate(
        [src[pl.ds(c, n, stride=SLOT_ROWS), :] for c in range(SLOT_ROWS)], axis=1)


def _dispatch_kernel(pos_ref, h2_ref, xs_hbm, buf, zeros, sem, zsem, *, tm, n_slots):
    j = pl.program_id(0)
    slot = j % 2

    def drain(s):
        pltpu.make_async_copy(buf.at[s], xs_hbm.at[pl.ds(0, tm * SLOT_ROWS), :], sem.at[s]).wait()

    @pl.when(j >= 2)
    def _():
        drain(slot)

    _to_slots(buf.at[slot], h2_ref[...])

    def body(g, c):
        for u in range(DMA_UNROLL):
            r = g * DMA_UNROLL + u
            src = pl.multiple_of(r * SLOT_ROWS, SLOT_ROWS)
            dst = pl.multiple_of(pos_ref[0, r] * SLOT_ROWS, SLOT_ROWS)
            pltpu.make_async_copy(buf.at[slot, pl.ds(src, SLOT_ROWS), :],
                                  xs_hbm.at[pl.ds(dst, SLOT_ROWS), :],
                                  sem.at[slot]).start(priority=u % DMA_QUEUES)
        return c
    lax.fori_loop(0, tm // DMA_UNROLL, body, 0)

    @pl.when(j == pl.num_programs(0) - 1)
    def _():
        zeros[...] = jnp.zeros(zeros.shape, F32)
        pad = pltpu.make_async_copy(
            zeros, xs_hbm.at[pl.ds(n_slots * SLOT_ROWS, MOE_ROWS * SLOT_ROWS), :], zsem.at[0])
        pad.start()
        drain(slot)
        drain(1 - slot)
        pad.wait()


def _dispatch(h2, pos, *, tm):
    t = h2.shape[0]
    nt = t // tm
    return pl.pallas_call(
        functools.partial(_dispatch_kernel, tm=tm, n_slots=t),
        grid=(nt,),
        in_specs=[
            pl.BlockSpec((None, 1, tm), lambda j: (j, 0, 0), memory_space=pltpu.SMEM),
            pl.BlockSpec((tm, D_MODEL), lambda j: (j, 0)),
        ],
        out_specs=pl.BlockSpec(memory_space=pl.ANY),
        out_shape=jax.ShapeDtypeStruct(((t + MOE_ROWS) * SLOT_ROWS, LANES), F32),
        scratch_shapes=[
            pltpu.VMEM((2, tm * SLOT_ROWS, LANES), F32),
            pltpu.VMEM((MOE_ROWS * SLOT_ROWS, LANES), F32),
            pltpu.SemaphoreType.DMA((2,)),
            pltpu.SemaphoreType.DMA((1,)),
        ],
        compiler_params=pltpu.CompilerParams(
            dimension_semantics=("arbitrary",), vmem_limit_bytes=VMEM_LIMIT),
        name="moe_dispatch",
    )(pos.reshape(nt, 1, tm), h2)


def _moe_kernel(start_ref, nrows_ref, ea_ref, eb_ref, nitems_ref,
                g_ref, wr_ref, br_ref, w1a_ref, w3a_ref, w2a_ref, w1b_ref, w3b_ref, w2b_ref, xs_hbm,
                ys_hbm, xbuf, ybuf, isem, osem):
    k = pl.program_id(0)
    n_items = nitems_ref[0]
    slot = k % 2
    rows = MOE_ROWS * SLOT_ROWS

    def window(kk):
        return pl.ds(pl.multiple_of(start_ref[kk] * SLOT_ROWS, SLOT_ROWS), rows)

    def in_copy(kk, s):
        return pltpu.make_async_copy(xs_hbm.at[window(kk), :], xbuf.at[s], isem.at[s])

    def out_copy(kk, s):
        return pltpu.make_async_copy(ybuf.at[s], ys_hbm.at[window(kk), :], osem.at[s])

    @pl.when(k == 0)
    def _():
        in_copy(0, 0).start()
        ybuf[1] = jnp.zeros(ybuf.shape[1:], F32)
        pad = pltpu.make_async_copy(
            ybuf.at[1], ys_hbm.at[pl.ds(ys_hbm.shape[0] - rows, rows), :], osem.at[1])
        pad.start()
        pad.wait()

    @pl.when(k + 1 < n_items)
    def _():
        in_copy(k + 1, 1 - slot).start()

    @pl.when(k < n_items)
    def _():
        in_copy(k, slot).wait()
        ea, eb = ea_ref[k], eb_ref[k]
        sub = MOE_ROWS // MOE_CHAINS
        for chain in range(MOE_CHAINS):
            block = pl.ds(chain * sub * SLOT_ROWS, sub * SLOT_ROWS)
            h2 = _from_slots(xbuf.at[slot, block], sub)
            ms = jnp.mean(h2 * h2, axis=-1, keepdims=True)
            xn = (h2 * lax.rsqrt(ms + RMS_EPS) * g_ref[...]).astype(BF16)

            logits = jnp.dot(xn, wr_ref[...], preferred_element_type=F32) + br_ref[...]
            lane = lax.broadcasted_iota(jnp.int32, logits.shape, 1)
            pick = lambda idx: jnp.sum(jnp.where(lane == idx, logits, 0.0), axis=1,
                                       keepdims=True)
            gl = jnp.where(lane < N_GROUPS, logits, -jnp.inf)
            gmax = jnp.max(gl, axis=1, keepdims=True)
            p_group = (jnp.exp(pick(ea // N_PER_GROUP) - gmax)
                       / jnp.sum(jnp.exp(gl - gmax), axis=1, keepdims=True))
            la, lb = pick(ROUTER_LANE0 + ea), pick(ROUTER_LANE0 + eb)
            row = chain * sub + lax.broadcasted_iota(jnp.int32, (sub, 1), 0)
            valid = row < nrows_ref[k]
            w_lo = jnp.where(valid, p_group / (1.0 + jnp.exp(lb - la)), 0.0)
            w_hi = jnp.where(valid, p_group / (1.0 + jnp.exp(la - lb)), 0.0)

            def expert(w1_ref, w3_ref, w2_ref, weight):
                h1 = jnp.dot(xn, w1_ref[...], preferred_element_type=F32)
                h3 = jnp.dot(xn, w3_ref[...], preferred_element_type=F32)
                hh = h1 * jax.nn.sigmoid(h1) * h3 * weight
                return jnp.dot(hh.astype(BF16), w2_ref[...], preferred_element_type=F32)

            y = h2 + (expert(w1a_ref, w3a_ref, w2a_ref, w_lo)
                      + expert(w1b_ref, w3b_ref, w2b_ref, w_hi))
            _to_slots(ybuf.at[slot, block], y)

        @pl.when(k >= 1)
        def _():
            out_copy(k - 1, 1 - slot).wait()

        out_copy(k, slot).start()

        @pl.when(k == n_items - 1)
        def _():
            out_copy(k, slot).wait()


def _class_experts():
    ea, eb = [], []
    for g in range(N_GROUPS):
        for a in range(N_PER_GROUP):
            for b in range(a + 1, N_PER_GROUP):
                ea.append(g * N_PER_GROUP + a)
                eb.append(g * N_PER_GROUP + b)
    fill = LANES - len(ea)
    return (jnp.array(ea + [ea[-1]] * fill, jnp.int32),
            jnp.array(eb + [eb[-1]] * fill, jnp.int32))


def _combine_kernel(pos_ref, pos_next_ref, ys_hbm, out_ref, buf, sem, *, tm):
    j = pl.program_id(0)
    slot = j % 2

    def start_tile(pos, s):
        def body(g, c):
            for u in range(DMA_UNROLL):
                r = g * DMA_UNROLL + u
                src = pl.multiple_of(pos[0, r] * SLOT_ROWS, SLOT_ROWS)
                dst = pl.multiple_of(r * SLOT_ROWS, SLOT_ROWS)
                pltpu.make_async_copy(ys_hbm.at[pl.ds(src, SLOT_ROWS), :],
                                      buf.at[s, pl.ds(dst, SLOT_ROWS), :],
                                      sem.at[s]).start(priority=u % DMA_QUEUES)
            return c
        lax.fori_loop(0, tm // DMA_UNROLL, body, 0)

    @pl.when(j == 0)
    def _():
        start_tile(pos_ref, 0)

    @pl.when(j + 1 < pl.num_programs(0))
    def _():
        start_tile(pos_next_ref, 1 - slot)

    pltpu.make_async_copy(ys_hbm.at[pl.ds(0, tm * SLOT_ROWS), :], buf.at[slot], sem.at[slot]).wait()
    out_ref[...] = _from_slots(buf.at[slot], tm)


def _combine(ys, pos, *, tm):
    t = pos.shape[0]
    nt = t // tm
    pos3 = pos.reshape(nt, 1, tm)
    smem_tile = lambda fn: pl.BlockSpec((None, 1, tm), fn, memory_space=pltpu.SMEM)
    return pl.pallas_call(
        functools.partial(_combine_kernel, tm=tm),
        grid=(nt,),
        in_specs=[
            smem_tile(lambda j: (j, 0, 0)),
            smem_tile(lambda j: (jnp.minimum(j + 1, nt - 1), 0, 0)),
            pl.BlockSpec(memory_space=pl.ANY),
        ],
        out_specs=pl.BlockSpec((tm, D_MODEL), lambda j: (j, 0)),
        out_shape=jax.ShapeDtypeStruct((t, D_MODEL), F32),
        scratch_shapes=[
            pltpu.VMEM((2, tm * SLOT_ROWS, LANES), F32),
            pltpu.SemaphoreType.DMA((2,)),
        ],
        compiler_params=pltpu.CompilerParams(
            dimension_semantics=("arbitrary",), vmem_limit_bytes=VMEM_LIMIT),
        name="moe_combine",
    )(pos3, pos3, ys)


def _moe(h2, plan, counts, p, *, tm):
    t = h2.shape[0]
    assert t % tm == 0 and t // tm >= 2
    max_items = -(-t // MOE_ROWS) + N_CLASSES
    class_ids = jnp.arange(LANES, dtype=jnp.int32)
    c_end = jnp.cumsum(counts)
    c_start = c_end - counts
    cls = plan[:, PLAN_CLASS, :].reshape(t).astype(jnp.int32)
    rank = plan[:, PLAN_RANK, :].reshape(t).astype(jnp.int32)
    pos = jnp.sum(jnp.where(cls[:, None] == class_ids[None, :], c_start[None, :], 0),
                  axis=1) + rank
    n_items_c = (counts + MOE_ROWS - 1) // MOE_ROWS
    item_end = jnp.cumsum(n_items_c)
    item_start = item_end - n_items_c
    n_items = item_end[-1]
    item_ids = jnp.minimum(jnp.arange(max_items, dtype=jnp.int32), n_items - 1)
    icls = jnp.sum(item_end[None, :] <= item_ids[:, None], axis=1, dtype=jnp.int32)
    pick = lambda table: jnp.sum(
        jnp.where(icls[:, None] == class_ids[None, :], table[None, :], 0), axis=1)
    offset = (item_ids - pick(item_start)) * MOE_ROWS
    start = pick(c_start) + offset
    nrows = jnp.minimum(pick(counts) - offset, MOE_ROWS)
    ea_tab, eb_tab = _class_experts()
    ea, eb = pick(ea_tab), pick(eb_tab)

    xs = _dispatch(h2, pos, tm=tm)

    first = lambda ea, eb: ea
    second = lambda ea, eb: eb
    w13 = lambda sel: pl.BlockSpec((None, D_MODEL, EXPERT_HIDDEN),
                                   lambda k, st, nr, ea, eb, n: (sel(ea, eb)[k], 0, 0))
    w2 = lambda sel: pl.BlockSpec((None, EXPERT_HIDDEN, D_MODEL),
                                  lambda k, st, nr, ea, eb, n: (sel(ea, eb)[k], 0, 0))
    const = lambda shape: pl.BlockSpec(shape, lambda k, *_: (0, 0))
    ys = pl.pallas_call(
        _moe_kernel,
        grid_spec=pltpu.PrefetchScalarGridSpec(
            num_scalar_prefetch=5,
            grid=(max_items,),
            in_specs=[
                const((1, D_MODEL)), const((D_MODEL, LANES)), const((1, LANES)),
                w13(first), w13(first), w2(first), w13(second), w13(second), w2(second),
                pl.BlockSpec(memory_space=pl.ANY),
            ],
            out_specs=pl.BlockSpec(memory_space=pl.ANY),
            scratch_shapes=[
                pltpu.VMEM((2, MOE_ROWS * SLOT_ROWS, LANES), F32),
                pltpu.VMEM((2, MOE_ROWS * SLOT_ROWS, LANES), F32),
                pltpu.SemaphoreType.DMA((2,)),
                pltpu.SemaphoreType.DMA((2,)),
            ],
        ),
        out_shape=jax.ShapeDtypeStruct(((t + MOE_ROWS) * SLOT_ROWS, LANES), F32),
        compiler_params=pltpu.CompilerParams(
            dimension_semantics=("arbitrary",), vmem_limit_bytes=VMEM_LIMIT),
        name="moe",
    )(start, nrows, ea, eb, n_items.reshape(1), p["g_ffn"], p["wr"], p["br"],
      p["w1"], p["w3"], p["w2"], p["w1"], p["w3"], p["w2"], xs)
    return _combine(ys, pos, tm=tm)


def _bias_selectors():
    selq = [[0.0] * LANES for _ in range(LANES)]
    selk = [[0.0] * LANES for _ in range(LANES)]
    ones_row = 3 * N_HEADS
    for h in range(N_HEADS):
        for part in range(3):
            selq[h * AUG_ROWS + part][part * N_HEADS + h] = 1.0
            selq[h * AUG_ROWS + 3 + part][ones_row] = 1.0
            selk[h * AUG_ROWS + part][ones_row] = 1.0
            selk[h * AUG_ROWS + 3 + part][part * N_HEADS + h] = -1.0
    return jnp.array(selq, BF16), jnp.array(selk, BF16)


def _prepare_params(norm_mix_g, w_in, b_forget, q_norm_g, k_norm_g, w_up_attn, w_pool,
                    pool_scale, w_up_pool, w_out, norm_ffn_g, w_group, b_group, w_router,
                    b_router, w1, w3, w2):
    aw, pw = ATTN_WIDTH, POOL_WIDTH
    f_off = 3 * aw
    p_off = f_off + N_HEADS
    g_off = p_off + pw
    selq, selk = _bias_selectors()
    pad_lanes = lambda a: jnp.pad(a, ((0, 0), (0, LANES - a.shape[1])))
    wpool = jnp.zeros((pw, pw), F32)
    for g in range(len(POOL_WINDOWS)):
        sl = slice(g * POOL_GROUP_DIM, (g + 1) * POOL_GROUP_DIM)
        wpool = wpool.at[sl, sl].set(w_pool[g])
    d, f = D_MODEL, EXPERT_HIDDEN
    return {
        "g_mix": norm_mix_g.reshape(1, d),
        "wqkv": w_in[:, 0:f_off].astype(BF16),
        "wf": pad_lanes(w_in[:, f_off:p_off]).astype(BF16),
        "bf": pad_lanes(b_forget.reshape(1, N_HEADS)),
        "wp": w_in[:, p_off:g_off].astype(BF16),
        "wg": w_in[:, g_off:].astype(BF16),
        "gq": q_norm_g, "gk": k_norm_g,
        "selq": selq, "selk": selk,
        "wpool": wpool.astype(BF16),
        "pscale": pool_scale.reshape(1, pw),
        "wua": w_up_attn.astype(BF16),
        "wup": w_up_pool.astype(BF16),
        "wout": w_out.astype(BF16),
        "g_ffn": norm_ffn_g.reshape(1, d),
        "wr": pad_lanes(jnp.concatenate([w_group, w_router], axis=1)).astype(BF16),
        "br": pad_lanes(jnp.concatenate([b_group, b_router]).reshape(1, -1)),
        "w1": w1.reshape(N_EXPERTS, d, f).astype(BF16),
        "w3": w3.reshape(N_EXPERTS, d, f).astype(BF16),
        "w2": w2.reshape(N_EXPERTS, f, d).astype(BF16),
    }


def kernel(x, meta_tokens, norm_mix_g, w_in, b_forget, q_norm_g, k_norm_g, w_up_attn, w_pool,
           pool_scale, w_up_pool, w_out, norm_ffn_g, w_group, b_group, w_router, b_router,
           w1, w3, w2):
    nb, seq, d = x.shape
    p = _prepare_params(norm_mix_g[0], w_in[0], b_forget[0], q_norm_g[0], k_norm_g[0],
                        w_up_attn[0], w_pool[0], pool_scale[0], w_up_pool[0], w_out[0],
                        norm_ffn_g[0], w_group[0], b_group[0], w_router[0], b_router[0],
                        w1[0], w3[0], w2[0])

    meta = jnp.pad(meta_tokens.astype(x.dtype), ((0, LANES - N_META), (0, 0)))[None]
    _, ka_m, vt_m, _, _, u_meta, f_meta, _, _ = _inproj(
        meta, p, jnp.zeros((MAX_WINDOW, POOL_WIDTH), F32), jnp.zeros((N_HEADS, LANES), F32),
        tm=LANES, n_valid=N_META, first_pos=0, chains=1)

    tm = 512
    qta, ka, vt, pooled, gates, _, _, k_sq, self_logit = _inproj(
        x, p, u_meta, f_meta, tm=tm, n_valid=tm, first_pos=N_META, chains=INPROJ_CHAINS)
    k_sq_max = jnp.max(k_sq.reshape(nb, seq // tm, N_HEADS, LANES), axis=1)
    ot = _attention(qta, ka, vt, ka_m, vt_m, k_sq_max, self_logit, tq=512)
    h2, plan, cnt = _post(ot, pooled, gates, x, p, tm=1024)
    out = _moe(h2.reshape(nb * seq, d), plan, cnt[:, 0].astype(jnp.int32), p, tm=512)
    return out.reshape(nb, seq, d)
```

```python
import functools
import math

import jax
import jax.numpy as jnp
import numpy as np
from jax import lax
from jax.experimental import pallas as pl
from jax.experimental.pallas import tpu as pltpu

F32 = jnp.float32
BF16 = jnp.bfloat16

D_MODEL = 1024
N_META = 16
N_HEADS = 8
HEAD_DIM = 64
ATTN_WIDTH = N_HEADS * HEAD_DIM
POOL_WINDOWS = (2, 4, 8, 16)
POOL_WIDTH = 512
POOL_GROUP_DIM = POOL_WIDTH // len(POOL_WINDOWS)
MAX_WINDOW = max(POOL_WINDOWS)
N_GROUPS = 4
N_PER_GROUP = 8
N_EXPERTS = N_GROUPS * N_PER_GROUP
EXPERT_HIDDEN = 256
RMS_EPS = 1e-6
LOG2_E = math.log2(math.e)

LANES = 128
AUG_ROWS = 16
HEAD_ROWS = HEAD_DIM + AUG_ROWS
QK_ROWS = N_HEADS * HEAD_ROWS
ROUTER_LANE0 = N_GROUPS
PAIRS_PER_GROUP = N_PER_GROUP * (N_PER_GROUP - 1) // 2
N_CLASSES = N_GROUPS * PAIRS_PER_GROUP
ROUTER_ROWS = 40
PLAN_CLASS, PLAN_RANK = 0, 1
MOE_ROWS = 352
SLOT_ROWS = D_MODEL // LANES
DMA_UNROLL = 8
DMA_QUEUES = 2
MOE_CHAINS = 2
INPROJ_CHAINS = 1
POST_CHAINS = 1
INPROJ_TILE = 1024
ATTN_TILE = 512
POST_TILE = 1024
ROW_MOVE_TILE = 512
SAFE_EXP2 = 64.0
NORM_SLACK = 1.02
VMEM_LIMIT = 56 * 1024 * 1024

_TN = (((0,), (0,)), ((), ()))


def _const_spec(shape):
    zeros = (0,) * len(shape)
    return pl.BlockSpec(shape, lambda *_: zeros, pipeline_mode=pl.Buffered(1))


def _split3(x):
    hi = x.astype(BF16).astype(F32)
    r = x - hi
    mid = r.astype(BF16).astype(F32)
    lo = (r - mid).astype(BF16).astype(F32)
    return hi, mid, lo


def _inproj_kernel(x_ref, g_ref, wqkv_ref, wf_ref, bf_ref, wp_ref, wg_ref, gq_ref, gk_ref,
                   selq_ref, selk_ref, wpool_ref, pscale_ref, uprev_ref, fprev_ref,
                   qta_ref, ka_ref, vt_ref, pooled_ref, gates_ref, utail_ref, ftail_ref, kmax_ref,
                   self_ref,
                   ubuf, fcarry, *, tm, n_valid, first_pos, chains):
    j = pl.program_id(1)

    @pl.when(j == 0)
    def _():
        ubuf[0:MAX_WINDOW, :] = uprev_ref[...]
        fcarry[...] = fprev_ref[...]

    sub = tm // chains
    r_i = lax.broadcasted_iota(jnp.int32, (sub, sub), 0)
    c_i = lax.broadcasted_iota(jnp.int32, (sub, sub), 1)
    tri = jnp.where(r_i <= c_i, 1.0, 0.0).astype(BF16)
    k_pad = jnp.zeros((LANES - HEAD_ROWS, sub), F32)
    carry = fcarry[...]
    k_sq_max = [None] * N_HEADS

    def head_norm(src, gain_ref, h):
        xh = src[h * HEAD_DIM:(h + 1) * HEAD_DIM, :]
        ssq = jnp.mean(xh * xh, axis=0, keepdims=True)
        return xh * lax.rsqrt(ssq + RMS_EPS) * gain_ref[:, 0:sub]

    for chain in range(chains):
        t0 = chain * sub
        toks = slice(t0, t0 + sub)
        x = x_ref[toks, :]
        ms = jnp.mean(x * x, axis=-1, keepdims=True)
        hn = (x * lax.rsqrt(ms + RMS_EPS) * g_ref[...]).astype(BF16)

        f = jnp.dot(hn, wf_ref[...], preferred_element_type=F32) + bf_ref[...]
        z = f.T[0:N_HEADS, :]
        logf = (jnp.minimum(z, 0.0) - jnp.log1p(jnp.exp(-jnp.abs(z)))) * LOG2_E
        parts = jnp.concatenate(_split3(logf), axis=0).astype(BF16)
        cs3 = jnp.dot(parts, tri, preferred_element_type=F32)
        cs = cs3[0:8] + cs3[8:16] + cs3[16:24]
        fc = jnp.concatenate([carry] * (sub // LANES), axis=1) + cs
        if n_valid - 1 >= t0:
            last_col = min(n_valid, t0 + sub) - 1 - t0
            carry = jnp.broadcast_to(fc[:, last_col:last_col + 1], (N_HEADS, LANES))

        pieces = jnp.concatenate(
            _split3(fc) + (jnp.ones((8, sub), F32), jnp.zeros((LANES - 32, sub), F32)),
            axis=0).astype(BF16)
        fq = jnp.dot(selq_ref[...], pieces, preferred_element_type=F32)
        fk = jnp.dot(selk_ref[...], pieces, preferred_element_type=F32)

        qkv = jnp.dot(hn, wqkv_ref[...], preferred_element_type=F32)
        qt = qkv[:, 0:ATTN_WIDTH].T
        kt = qkv[:, ATTN_WIDTH:2 * ATTN_WIDTH].T
        vt_ref[:, toks] = qkv[:, 2 * ATTN_WIDTH:3 * ATTN_WIDTH].T.astype(BF16)

        for h in range(N_HEADS):
            aug = slice(h * AUG_ROWS, (h + 1) * AUG_ROWS)
            r0 = h * HEAD_ROWS
            qn = head_norm(qt, gq_ref, h)
            qta_ref[r0:r0 + HEAD_DIM, toks] = qn.astype(BF16)
            qta_ref[r0 + HEAD_DIM:r0 + HEAD_ROWS, toks] = fq[aug, :].astype(BF16)
            kn = head_norm(kt, gk_ref, h)
            self_ref[h:h + 1, toks] = jnp.sum(qn * kn, axis=0, keepdims=True)
            k_blk = jnp.concatenate([kn, fk[aug, :], k_pad], axis=0)
            ka_ref[toks, h * LANES:(h + 1) * LANES] = k_blk.T.astype(BF16)
            k_sq = jnp.max(jnp.sum(kn * kn, axis=0, keepdims=True), axis=1, keepdims=True)
            k_sq_max[h] = k_sq if k_sq_max[h] is None else jnp.maximum(k_sq_max[h], k_sq)

        u = jnp.dot(hn, wp_ref[...], preferred_element_type=F32)
        u0 = MAX_WINDOW + t0
        ubuf[u0:u0 + sub, :] = u
        mixed = []
        for g, w in enumerate(POOL_WINDOWS):
            c0 = g * POOL_GROUP_DIM
            acc = u[:, c0:c0 + POOL_GROUP_DIM]
            for s in range(1, w):
                acc = acc + ubuf[u0 - s:u0 - s + sub, c0:c0 + POOL_GROUP_DIM]
            if first_pos + 1 >= w:
                mean = acc * (1.0 / w)
            else:
                pos = (first_pos + j * tm + t0
                       + lax.broadcasted_iota(jnp.int32, (sub, POOL_GROUP_DIM), 0))
                mean = acc / jnp.minimum(pos + 1, w).astype(F32)
            mixed.append(mean - u[:, c0:c0 + POOL_GROUP_DIM])
        mixed = jnp.concatenate(mixed, axis=1).astype(BF16)
        y = jnp.dot(mixed, wpool_ref[...], preferred_element_type=F32) * pscale_ref[...]
        pooled_ref[toks, :] = y.astype(BF16)

        gl = jnp.dot(hn, wg_ref[...], preferred_element_type=F32)
        gates_ref[toks, :] = jax.nn.sigmoid(gl).astype(BF16)

    fcarry[...] = carry
    ftail_ref[...] = carry
    for h in range(N_HEADS):
        kmax_ref[h:h + 1, :] = jnp.broadcast_to(k_sq_max[h], (1, LANES))
    tail = ubuf[n_valid:n_valid + MAX_WINDOW, :]
    ubuf[0:MAX_WINDOW, :] = tail
    utail_ref[...] = tail


def _inproj(x3, p, uprev, fprev, *, tm, n_valid, first_pos, chains):
    nb, seq, _ = x3.shape
    nt = seq // tm
    gq = jnp.broadcast_to((p["gq"] * (LOG2_E / math.sqrt(HEAD_DIM)))[:, None], (HEAD_DIM, tm))
    gk = jnp.broadcast_to(p["gk"][:, None], (HEAD_DIM, tm))
    tok = lambda width: pl.BlockSpec((None, tm, width), lambda b, j: (b, j, 0))
    chan = lambda rows: pl.BlockSpec((None, rows, tm), lambda b, j: (b, 0, j))
    kern = functools.partial(_inproj_kernel, tm=tm, n_valid=n_valid, first_pos=first_pos,
                             chains=chains)
    return pl.pallas_call(
        kern,
        grid=(nb, nt),
        in_specs=[
            tok(D_MODEL),
            _const_spec((1, D_MODEL)),
            _const_spec((D_MODEL, 3 * ATTN_WIDTH)),
            _const_spec((D_MODEL, LANES)),
            _const_spec((1, LANES)),
            _const_spec((D_MODEL, POOL_WIDTH)),
            _const_spec((D_MODEL, 2 * D_MODEL)),
            _const_spec((HEAD_DIM, tm)),
            _const_spec((HEAD_DIM, tm)),
            _const_spec((LANES, LANES)),
            _const_spec((LANES, LANES)),
            _const_spec((POOL_WIDTH, POOL_WIDTH)),
            _const_spec((1, POOL_WIDTH)),
            _const_spec((MAX_WINDOW, POOL_WIDTH)),
            _const_spec((N_HEADS, LANES)),
        ],
        out_specs=[
            chan(QK_ROWS), tok(N_HEADS * LANES), chan(ATTN_WIDTH),
            tok(POOL_WIDTH), tok(2 * D_MODEL),
            pl.BlockSpec((MAX_WINDOW, POOL_WIDTH), lambda b, j: (0, 0)),
            pl.BlockSpec((N_HEADS, LANES), lambda b, j: (0, 0)),
            pl.BlockSpec((None, N_HEADS, LANES), lambda b, j: (b * nt + j, 0, 0)),
            chan(N_HEADS),
        ],
        out_shape=[
            jax.ShapeDtypeStruct((nb, QK_ROWS, seq), BF16),
            jax.ShapeDtypeStruct((nb, seq, N_HEADS * LANES), BF16),
            jax.ShapeDtypeStruct((nb, ATTN_WIDTH, seq), BF16),
            jax.ShapeDtypeStruct((nb, seq, POOL_WIDTH), BF16),
            jax.ShapeDtypeStruct((nb, seq, 2 * D_MODEL), BF16),
            jax.ShapeDtypeStruct((MAX_WINDOW, POOL_WIDTH), F32),
            jax.ShapeDtypeStruct((N_HEADS, LANES), F32),
            jax.ShapeDtypeStruct((nb * nt, N_HEADS, LANES), F32),
            jax.ShapeDtypeStruct((nb, N_HEADS, seq), F32),
        ],
        scratch_shapes=[
            pltpu.VMEM((MAX_WINDOW + tm, POOL_WIDTH), F32),
            pltpu.VMEM((N_HEADS, LANES), F32),
        ],
        compiler_params=pltpu.CompilerParams(
            dimension_semantics=("arbitrary", "arbitrary"), vmem_limit_bytes=VMEM_LIMIT),
        name="inproj",
    )(x3, p["g_mix"], p["wqkv"], p["wf"], p["bf"], p["wp"], p["wg"], gq, gk,
      p["selq"], p["selk"], p["wpool"], p["pscale"], uprev, fprev)


def _attn_kernel(qta_ref, ka_ref, vt_ref, kam_ref, vtm_ref, bias_ref, kmax_ref, self_ref, ot_ref,
                 m_sc, acc_sc, *, tq):
    qi = pl.program_id(1)
    m_sc[...] = jnp.full(m_sc.shape, -jnp.inf, F32)
    acc_sc[...] = jnp.zeros(acc_sc.shape, F32)

    def rows(h):
        return slice(h * HEAD_ROWS, (h + 1) * HEAD_ROWS)

    def vrows(h):
        return slice(h * HEAD_DIM, (h + 1) * HEAD_DIM)

    def klanes(h):
        return slice(h * LANES, (h + 1) * LANES)

    q_pad = jnp.zeros((LANES - HEAD_ROWS, tq), BF16)

    def scores(h, ka):
        qa = jnp.concatenate([qta_ref[rows(h), :], q_pad], axis=0)
        return jnp.dot(ka, qa, preferred_element_type=F32)

    def new_max(h, s):
        return jnp.maximum(m_sc[h], jnp.max(s, axis=0, keepdims=True))

    def absorb(h, s, m_new, v):
        p = jnp.exp2(s - m_new).astype(BF16)
        va = jnp.concatenate([v, jnp.ones((AUG_ROWS, v.shape[1]), BF16)], axis=0)
        acc_sc[h] = (jnp.exp2(m_sc[h] - m_new) * acc_sc[h]
                     + jnp.dot(va, p, preferred_element_type=F32))
        m_sc[h] = m_new

    def sweep_heads(scores_of, values_of):
        s = {0: scores_of(0), 1: scores_of(1)}
        m = {0: new_max(0, s[0])}
        for h in range(N_HEADS):
            if h + 2 < N_HEADS:
                s[h + 2] = scores_of(h + 2)
            if h + 1 < N_HEADS:
                m[h + 1] = new_max(h + 1, s[h + 1])
            absorb(h, s.pop(h), m.pop(h), values_of(h))

    def sweep_heads_bounded(*tiles):
        items = [(h, s_of, v_of) for s_of, v_of in tiles for h in range(N_HEADS)]
        s_next = items[0][1](items[0][0])
        for n, (h, _, values_of) in enumerate(items):
            s = s_next
            if n + 1 < len(items):
                s_next = items[n + 1][1](items[n + 1][0])
            p = jnp.exp2(s - m_sc[h]).astype(BF16)
            v = values_of(h)
            va = jnp.concatenate([v, jnp.ones((AUG_ROWS, v.shape[1]), BF16)], axis=0)
            acc_sc[h] += jnp.dot(va, p, preferred_element_type=F32)

    d0 = pl.multiple_of(qi * tq, tq)
    diag_scores = lambda h: scores(h, jnp.concatenate(
        [ka_ref[pl.ds(d0, tq), klanes(h)], kam_ref[:, klanes(h)]], axis=0)) + bias_ref[...]
    diag_values = lambda h: jnp.concatenate(
        [vt_ref[vrows(h), pl.ds(d0, tq)], vtm_ref[vrows(h), :]], axis=1)

    def tile_scores(i):
        s0 = pl.multiple_of(i * tq, tq)
        return lambda h: scores(h, ka_ref[pl.ds(s0, tq), klanes(h)])

    def tile_values(i):
        s0 = pl.multiple_of(i * tq, tq)
        return lambda h: vt_ref[vrows(h), pl.ds(s0, tq)]

    worst = None
    for h in range(N_HEADS):
        q = qta_ref[h * HEAD_ROWS:h * HEAD_ROWS + HEAD_DIM, :].astype(F32)
        q_norm = jnp.sqrt(jnp.sum(q * q, axis=0, keepdims=True))
        k_norm = jnp.sqrt(jnp.concatenate([kmax_ref[h:h + 1, :]] * (tq // LANES), axis=1))
        gap = q_norm * k_norm * NORM_SLACK - SAFE_EXP2 - self_ref[h:h + 1, :]
        worst = gap if worst is None else jnp.maximum(worst, gap)
    bounded = jnp.max(worst) <= 0.0

    @pl.when(bounded)
    def _():
        for h in range(N_HEADS):
            m_sc[h] = self_ref[h:h + 1, :]
        sweep_heads_bounded((diag_scores, diag_values))

        def body(i, carry):
            sweep_heads_bounded((tile_scores(2 * i), tile_values(2 * i)),
                                (tile_scores(2 * i + 1), tile_values(2 * i + 1)))
            return carry
        lax.fori_loop(0, qi // 2, body, 0)

        @pl.when(qi % 2 == 1)
        def _():
            sweep_heads_bounded((tile_scores(qi - 1), tile_values(qi - 1)))

    @pl.when(jnp.logical_not(bounded))
    def _():
        sweep_heads(diag_scores, diag_values)

        def body(i, carry):
            sweep_heads(tile_scores(i), tile_values(i))
            return carry
        lax.fori_loop(0, qi, body, 0)

    for h in range(N_HEADS):
        acc = acc_sc[h]
        ot_ref[vrows(h), :] = (acc[0:HEAD_DIM, :] / acc[HEAD_DIM:HEAD_DIM + 1, :]).astype(BF16)


def _attention(qta, ka, vt, ka_m, vt_m, k_sq_max, self_logit, *, tq):
    nb, _, seq = qta.shape
    kw = N_HEADS * LANES
    key = np.arange(tq + LANES)[:, None]
    qry = np.arange(tq)[None, :]
    visible = np.where(key < tq, key <= qry, key - tq < N_META)
    bias = jnp.asarray(np.where(visible, 0.0, -np.inf), F32)
    return pl.pallas_call(
        functools.partial(_attn_kernel, tq=tq),
        grid=(nb, seq // tq),
        in_specs=[
            pl.BlockSpec((None, QK_ROWS, tq), lambda b, i: (b, 0, i)),
            pl.BlockSpec((None, seq, kw), lambda b, i: (b, 0, 0)),
            pl.BlockSpec((None, ATTN_WIDTH, seq), lambda b, i: (b, 0, 0)),
            pl.BlockSpec((None, LANES, kw), lambda b, i: (0, 0, 0)),
            pl.BlockSpec((None, ATTN_WIDTH, LANES), lambda b, i: (0, 0, 0)),
            _const_spec((tq + LANES, tq)),
            pl.BlockSpec((None, N_HEADS, LANES), lambda b, i: (b, 0, 0)),
            pl.BlockSpec((None, N_HEADS, tq), lambda b, i: (b, 0, i)),
        ],
        out_specs=pl.BlockSpec((None, ATTN_WIDTH, tq), lambda b, i: (b, 0, i)),
        out_shape=jax.ShapeDtypeStruct((nb, ATTN_WIDTH, seq), BF16),
        scratch_shapes=[
            pltpu.VMEM((N_HEADS, 1, tq), F32),
            pltpu.VMEM((N_HEADS, HEAD_ROWS, tq), F32),
        ],
        compiler_params=pltpu.CompilerParams(
            dimension_semantics=("arbitrary", "arbitrary"), vmem_limit_bytes=VMEM_LIMIT),
        name="fox_attention",
    )(qta, ka, vt, ka_m, vt_m, bias, k_sq_max, self_logit)


def _post_kernel(ot_ref, pooled_ref, gates_ref, x_ref, wua_ref, wup_ref, wout_ref, g_ref,
                 wr_ref, br_ref, h2_ref, plan_ref, cnt_ref, cnt_sc, *, tm):
    @pl.when((pl.program_id(0) == 0) & (pl.program_id(1) == 0))
    def _():
        cnt_sc[...] = jnp.zeros(cnt_sc.shape, F32)

    sub = tm // POST_CHAINS
    row = lax.broadcasted_iota(jnp.int32, (ROUTER_ROWS, sub), 0)
    crow = lax.broadcasted_iota(jnp.int32, (LANES, sub), 0)
    prow = lax.broadcasted_iota(jnp.int32, (8, sub), 0)
    r_i = lax.broadcasted_iota(jnp.int32, (sub, sub), 0)
    c_i = lax.broadcasted_iota(jnp.int32, (sub, sub), 1)
    earlier = jnp.where(r_i < c_i, 1.0, 0.0).astype(BF16)
    neg_inf = jnp.float32(-jnp.inf)
    counts = cnt_sc[...]

    for chain in range(POST_CHAINS):
        toks = slice(chain * sub, (chain + 1) * sub)
        y_attn = lax.dot_general(ot_ref[:, toks], wua_ref[...], _TN, preferred_element_type=F32)
        y_pool = jnp.dot(pooled_ref[toks, :], wup_ref[...], preferred_element_type=F32)
        merged = (gates_ref[toks, 0:D_MODEL].astype(F32) * y_attn
                  + gates_ref[toks, D_MODEL:2 * D_MODEL].astype(F32) * y_pool)
        h2 = x_ref[toks, :] + jnp.dot(merged.astype(BF16), wout_ref[...],
                                      preferred_element_type=F32)
        h2_ref[toks, :] = h2
        ms = jnp.mean(h2 * h2, axis=-1, keepdims=True)
        hn2 = (h2 * lax.rsqrt(ms + RMS_EPS) * g_ref[...]).astype(BF16)

        logits = jnp.dot(hn2, wr_ref[...], preferred_element_type=F32) + br_ref[...]
        lt = logits.T[0:ROUTER_ROWS, :]

        def softmax_over(mask):
            z = jnp.where(mask, lt, neg_inf)
            e = jnp.exp(z - jnp.max(z, axis=0, keepdims=True))
            return jnp.where(mask, e / jnp.sum(e, axis=0, keepdims=True), -1.0)

        def argtop(prob):
            top = jnp.max(prob, axis=0, keepdims=True)
            return jnp.min(jnp.where(prob == top, row, LANES), axis=0, keepdims=True)

        g_idx = argtop(softmax_over(row < N_GROUPS))
        e_lo = ROUTER_LANE0 + N_PER_GROUP * g_idx
        e_prob = softmax_over((row >= e_lo) & (row < e_lo + N_PER_GROUP))
        i1 = argtop(e_prob)
        i2 = argtop(jnp.where(row == i1, -1.0, e_prob))
        a = jnp.minimum(i1, i2) - e_lo
        b = jnp.maximum(i1, i2) - e_lo
        pair = ((a * (2 * N_PER_GROUP - 1 - a)) >> 1) + (b - a - 1)
        cls = g_idx * PAIRS_PER_GROUP + pair

        member = crow == cls
        onehot = jnp.where(member, 1.0, 0.0)
        prefix = jnp.dot(onehot.astype(BF16), earlier, preferred_element_type=F32)
        base = jnp.concatenate([counts] * (sub // LANES), axis=1)
        rank = jnp.sum(jnp.where(member, prefix + base, 0.0), axis=0, keepdims=True)
        counts = counts + jnp.broadcast_to(jnp.sum(onehot, axis=1, keepdims=True), counts.shape)
        plan_ref[:, toks] = jnp.where(prow == PLAN_CLASS, cls.astype(F32),
                                      jnp.where(prow == PLAN_RANK, rank, 0.0))

    cnt_sc[...] = counts
    cnt_ref[...] = counts


def _post(ot, pooled, gates, x, p, *, tm):
    nb, seq, _ = x.shape
    nt = seq // tm
    tok = lambda width: pl.BlockSpec((None, tm, width), lambda b, j: (b, j, 0))
    return pl.pallas_call(
        functools.partial(_post_kernel, tm=tm),
        grid=(nb, nt),
        in_specs=[
            pl.BlockSpec((None, ATTN_WIDTH, tm), lambda b, j: (b, 0, j)),
            tok(POOL_WIDTH), tok(2 * D_MODEL), tok(D_MODEL),
            _const_spec((ATTN_WIDTH, D_MODEL)),
            _const_spec((POOL_WIDTH, D_MODEL)),
            _const_spec((D_MODEL, D_MODEL)),
            _const_spec((1, D_MODEL)),
            _const_spec((D_MODEL, LANES)),
            _const_spec((1, LANES)),
        ],
        out_specs=[tok(D_MODEL),
                   pl.BlockSpec((None, 8, tm), lambda b, j: (b * nt + j, 0, 0)),
                   pl.BlockSpec((LANES, LANES), lambda b, j: (0, 0))],
        out_shape=[
            jax.ShapeDtypeStruct((nb, seq, D_MODEL), F32),
            jax.ShapeDtypeStruct((nb * nt, 8, tm), F32),
            jax.ShapeDtypeStruct((LANES, LANES), F32),
        ],
        scratch_shapes=[pltpu.VMEM((LANES, LANES), F32)],
        compiler_params=pltpu.CompilerParams(
            dimension_semantics=("arbitrary", "arbitrary"), vmem_limit_bytes=VMEM_LIMIT),
        name="post_router",
    )(ot, pooled, gates, x, p["wua"], p["wup"], p["wout"], p["g_ffn"], p["wr"], p["br"])


def _to_slots(dst, values):
    n = values.shape[0]
    for c in range(SLOT_ROWS):
        dst[pl.ds(c, n, stride=SLOT_ROWS), :] = values[:, c * LANES:(c + 1) * LANES]


def _from_slots(src, n):
    return jnp.concatenate(
        [src[pl.ds(c, n, stride=SLOT_ROWS), :] for c in range(SLOT_ROWS)], axis=1)


def _dispatch_kernel(pos_ref, h2_ref, xs_hbm, buf, zeros, sem, zsem, *, tm, n_slots):
    j = pl.program_id(0)
    slot = j % 2

    def drain(s):
        pltpu.make_async_copy(buf.at[s], xs_hbm.at[pl.ds(0, tm * SLOT_ROWS), :], sem.at[s]).wait()

    @pl.when(j >= 2)
    def _():
        drain(slot)

    _to_slots(buf.at[slot], h2_ref[...])

    def body(g, c):
        for u in range(DMA_UNROLL):
            r = g * DMA_UNROLL + u
            src = pl.multiple_of(r * SLOT_ROWS, SLOT_ROWS)
            dst = pl.multiple_of(pos_ref[0, r] * SLOT_ROWS, SLOT_ROWS)
            pltpu.make_async_copy(buf.at[slot, pl.ds(src, SLOT_ROWS), :],
                                  xs_hbm.at[pl.ds(dst, SLOT_ROWS), :],
                                  sem.at[slot]).start(priority=u % DMA_QUEUES)
        return c
    lax.fori_loop(0, tm // DMA_UNROLL, body, 0)

    @pl.when(j == pl.num_programs(0) - 1)
    def _():
        zeros[...] = jnp.zeros(zeros.shape, F32)
        pad = pltpu.make_async_copy(
            zeros, xs_hbm.at[pl.ds(n_slots * SLOT_ROWS, MOE_ROWS * SLOT_ROWS), :], zsem.at[0])
        pad.start()
        drain(slot)
        drain(1 - slot)
        pad.wait()


def _dispatch(h2, pos, *, tm):
    t = h2.shape[0]
    nt = t // tm
    return pl.pallas_call(
        functools.partial(_dispatch_kernel, tm=tm, n_slots=t),
        grid=(nt,),
        in_specs=[
            pl.BlockSpec((None, 1, tm), lambda j: (j, 0, 0), memory_space=pltpu.SMEM),
            pl.BlockSpec((tm, D_MODEL), lambda j: (j, 0)),
        ],
        out_specs=pl.BlockSpec(memory_space=pl.ANY),
        out_shape=jax.ShapeDtypeStruct(((t + MOE_ROWS) * SLOT_ROWS, LANES), F32),
        scratch_shapes=[
            pltpu.VMEM((2, tm * SLOT_ROWS, LANES), F32),
            pltpu.VMEM((MOE_ROWS * SLOT_ROWS, LANES), F32),
            pltpu.SemaphoreType.DMA((2,)),
            pltpu.SemaphoreType.DMA((1,)),
        ],
        compiler_params=pltpu.CompilerParams(
            dimension_semantics=("arbitrary",), vmem_limit_bytes=VMEM_LIMIT),
        name="moe_dispatch",
    )(pos.reshape(nt, 1, tm), h2)


def _moe_kernel(start_ref, nrows_ref, ea_ref, eb_ref, nitems_ref,
                g_ref, wr_ref, br_ref, w1a_ref, w3a_ref, w2a_ref, w1b_ref, w3b_ref, w2b_ref, xs_hbm,
                ys_hbm, xbuf, ybuf, isem, osem):
    k = pl.program_id(0)
    n_items = nitems_ref[0]
    slot = k % 2
    rows = MOE_ROWS * SLOT_ROWS

    def window(kk):
        return pl.ds(pl.multiple_of(start_ref[kk] * SLOT_ROWS, SLOT_ROWS), rows)

    def in_copy(kk, s):
        return pltpu.make_async_copy(xs_hbm.at[window(kk), :], xbuf.at[s], isem.at[s])

    def out_copy(kk, s):
        return pltpu.make_async_copy(ybuf.at[s], ys_hbm.at[window(kk), :], osem.at[s])

    @pl.when(k == 0)
    def _():
        in_copy(0, 0).start()
        ybuf[1] = jnp.zeros(ybuf.shape[1:], F32)
        pad = pltpu.make_async_copy(
            ybuf.at[1], ys_hbm.at[pl.ds(ys_hbm.shape[0] - rows, rows), :], osem.at[1])
        pad.start()
        pad.wait()

    @pl.when(k + 1 < n_items)
    def _():
        in_copy(k + 1, 1 - slot).start()

    @pl.when(k < n_items)
    def _():
        in_copy(k, slot).wait()
        ea, eb = ea_ref[k], eb_ref[k]
        sub = MOE_ROWS // MOE_CHAINS
        for chain in range(MOE_CHAINS):
            block = pl.ds(chain * sub * SLOT_ROWS, sub * SLOT_ROWS)
            h2 = _from_slots(xbuf.at[slot, block], sub)
            ms = jnp.mean(h2 * h2, axis=-1, keepdims=True)
            xn = (h2 * lax.rsqrt(ms + RMS_EPS) * g_ref[...]).astype(BF16)

            logits = jnp.dot(xn, wr_ref[...], preferred_element_type=F32) + br_ref[...]
            lane = lax.broadcasted_iota(jnp.int32, logits.shape, 1)
            pick = lambda idx: jnp.sum(jnp.where(lane == idx, logits, 0.0), axis=1,
                                       keepdims=True)
            gl = jnp.where(lane < N_GROUPS, logits, -jnp.inf)
            gmax = jnp.max(gl, axis=1, keepdims=True)
            p_group = (jnp.exp(pick(ea // N_PER_GROUP) - gmax)
                       / jnp.sum(jnp.exp(gl - gmax), axis=1, keepdims=True))
            la, lb = pick(ROUTER_LANE0 + ea), pick(ROUTER_LANE0 + eb)
            row = chain * sub + lax.broadcasted_iota(jnp.int32, (sub, 1), 0)
            valid = row < nrows_ref[k]
            w_lo = jnp.where(valid, p_group / (1.0 + jnp.exp(lb - la)), 0.0)
            w_hi = jnp.where(valid, p_group / (1.0 + jnp.exp(la - lb)), 0.0)

            def expert(w1_ref, w3_ref, w2_ref, weight):
                h1 = jnp.dot(xn, w1_ref[...], preferred_element_type=F32)
                h3 = jnp.dot(xn, w3_ref[...], preferred_element_type=F32)
                hh = h1 * jax.nn.sigmoid(h1) * h3 * weight
                return jnp.dot(hh.astype(BF16), w2_ref[...], preferred_element_type=F32)

            y = h2 + (expert(w1a_ref, w3a_ref, w2a_ref, w_lo)
                      + expert(w1b_ref, w3b_ref, w2b_ref, w_hi))
            _to_slots(ybuf.at[slot, block], y)

        @pl.when(k >= 1)
        def _():
            out_copy(k - 1, 1 - slot).wait()

        out_copy(k, slot).start()

        @pl.when(k == n_items - 1)
        def _():
            out_copy(k, slot).wait()


def _class_experts():
    ea, eb = [], []
    for g in range(N_GROUPS):
        for a in range(N_PER_GROUP):
            for b in range(a + 1, N_PER_GROUP):
                ea.append(g * N_PER_GROUP + a)
                eb.append(g * N_PER_GROUP + b)
    fill = LANES - len(ea)
    return (jnp.array(ea + [ea[-1]] * fill, jnp.int32),
            jnp.array(eb + [eb[-1]] * fill, jnp.int32))


def _combine_kernel(pos_ref, pos_next_ref, ys_hbm, out_ref, buf, sem, *, tm):
    j = pl.program_id(0)
    slot = j % 2

    def start_tile(pos, s):
        def body(g, c):
            for u in range(DMA_UNROLL):
                r = g * DMA_UNROLL + u
                src = pl.multiple_of(pos[0, r] * SLOT_ROWS, SLOT_ROWS)
                dst = pl.multiple_of(r * SLOT_ROWS, SLOT_ROWS)
                pltpu.make_async_copy(ys_hbm.at[pl.ds(src, SLOT_ROWS), :],
                                      buf.at[s, pl.ds(dst, SLOT_ROWS), :],
                                      sem.at[s]).start(priority=u % DMA_QUEUES)
            return c
        lax.fori_loop(0, tm // DMA_UNROLL, body, 0)

    @pl.when(j == 0)
    def _():
        start_tile(pos_ref, 0)

    @pl.when(j + 1 < pl.num_programs(0))
    def _():
        start_tile(pos_next_ref, 1 - slot)

    pltpu.make_async_copy(ys_hbm.at[pl.ds(0, tm * SLOT_ROWS), :], buf.at[slot], sem.at[slot]).wait()
    out_ref[...] = _from_slots(buf.at[slot], tm)


def _combine(ys, pos, *, tm):
    t = pos.shape[0]
    nt = t // tm
    pos3 = pos.reshape(nt, 1, tm)
    smem_tile = lambda fn: pl.BlockSpec((None, 1, tm), fn, memory_space=pltpu.SMEM)
    return pl.pallas_call(
        functools.partial(_combine_kernel, tm=tm),
        grid=(nt,),
        in_specs=[
            smem_tile(lambda j: (j, 0, 0)),
            smem_tile(lambda j: (jnp.minimum(j + 1, nt - 1), 0, 0)),
            pl.BlockSpec(memory_space=pl.ANY),
        ],
        out_specs=pl.BlockSpec((tm, D_MODEL), lambda j: (j, 0)),
        out_shape=jax.ShapeDtypeStruct((t, D_MODEL), F32),
        scratch_shapes=[
            pltpu.VMEM((2, tm * SLOT_ROWS, LANES), F32),
            pltpu.SemaphoreType.DMA((2,)),
        ],
        compiler_params=pltpu.CompilerParams(
            dimension_semantics=("arbitrary",), vmem_limit_bytes=VMEM_LIMIT),
        name="moe_combine",
    )(pos3, pos3, ys)


def _moe(h2, plan, counts, p, *, tm):
    t = h2.shape[0]
    assert t % tm == 0 and t // tm >= 2
    max_items = -(-t // MOE_ROWS) + N_CLASSES
    class_ids = jnp.arange(LANES, dtype=jnp.int32)
    c_end = jnp.cumsum(counts)
    c_start = c_end - counts
    cls = plan[:, PLAN_CLASS, :].reshape(t).astype(jnp.int32)
    rank = plan[:, PLAN_RANK, :].reshape(t).astype(jnp.int32)
    pos = jnp.sum(jnp.where(cls[:, None] == class_ids[None, :], c_start[None, :], 0),
                  axis=1) + rank
    n_items_c = (counts + MOE_ROWS - 1) // MOE_ROWS
    item_end = jnp.cumsum(n_items_c)
    item_start = item_end - n_items_c
    n_items = item_end[-1]
    item_ids = jnp.minimum(jnp.arange(max_items, dtype=jnp.int32), n_items - 1)
    icls = jnp.sum(item_end[None, :] <= item_ids[:, None], axis=1, dtype=jnp.int32)
    pick = lambda table: jnp.sum(
        jnp.where(icls[:, None] == class_ids[None, :], table[None, :], 0), axis=1)
    offset = (item_ids - pick(item_start)) * MOE_ROWS
    start = pick(c_start) + offset
    nrows = jnp.minimum(pick(counts) - offset, MOE_ROWS)
    ea_tab, eb_tab = _class_experts()
    ea, eb = pick(ea_tab), pick(eb_tab)

    xs = _dispatch(h2, pos, tm=tm)

    first = lambda ea, eb: ea
    second = lambda ea, eb: eb
    w13 = lambda sel: pl.BlockSpec((None, D_MODEL, EXPERT_HIDDEN),
                                   lambda k, st, nr, ea, eb, n: (sel(ea, eb)[k], 0, 0))
    w2 = lambda sel: pl.BlockSpec((None, EXPERT_HIDDEN, D_MODEL),
                                  lambda k, st, nr, ea, eb, n: (sel(ea, eb)[k], 0, 0))
    const = lambda shape: pl.BlockSpec(shape, lambda k, *_: (0, 0))
    ys = pl.pallas_call(
        _moe_kernel,
        grid_spec=pltpu.PrefetchScalarGridSpec(
            num_scalar_prefetch=5,
            grid=(max_items,),
            in_specs=[
                const((1, D_MODEL)), const((D_MODEL, LANES)), const((1, LANES)),
                w13(first), w13(first), w2(first), w13(second), w13(second), w2(second),
                pl.BlockSpec(memory_space=pl.ANY),
            ],
            out_specs=pl.BlockSpec(memory_space=pl.ANY),
            scratch_shapes=[
                pltpu.VMEM((2, MOE_ROWS * SLOT_ROWS, LANES), F32),
                pltpu.VMEM((2, MOE_ROWS * SLOT_ROWS, LANES), F32),
                pltpu.SemaphoreType.DMA((2,)),
                pltpu.SemaphoreType.DMA((2,)),
            ],
        ),
        out_shape=jax.ShapeDtypeStruct(((t + MOE_ROWS) * SLOT_ROWS, LANES), F32),
        compiler_params=pltpu.CompilerParams(
            dimension_semantics=("arbitrary",), vmem_limit_bytes=VMEM_LIMIT),
        name="moe",
    )(start, nrows, ea, eb, n_items.reshape(1), p["g_ffn"], p["wr"], p["br"],
      p["w1"], p["w3"], p["w2"], p["w1"], p["w3"], p["w2"], xs)
    return _combine(ys, pos, tm=tm)


def _bias_selectors():
    selq = [[0.0] * LANES for _ in range(LANES)]
    selk = [[0.0] * LANES for _ in range(LANES)]
    ones_row = 3 * N_HEADS
    for h in range(N_HEADS):
        for part in range(3):
            selq[h * AUG_ROWS + part][part * N_HEADS + h] = 1.0
            selq[h * AUG_ROWS + 3 + part][ones_row] = 1.0
            selk[h * AUG_ROWS + part][ones_row] = 1.0
            selk[h * AUG_ROWS + 3 + part][part * N_HEADS + h] = -1.0
    return jnp.array(selq, BF16), jnp.array(selk, BF16)


def _prepare_params(norm_mix_g, w_in, b_forget, q_norm_g, k_norm_g, w_up_attn, w_pool,
                    pool_scale, w_up_pool, w_out, norm_ffn_g, w_group, b_group, w_router,
                    b_router, w1, w3, w2):
    aw, pw = ATTN_WIDTH, POOL_WIDTH
    f_off = 3 * aw
    p_off = f_off + N_HEADS
    g_off = p_off + pw
    selq, selk = _bias_selectors()
    pad_lanes = lambda a: jnp.pad(a, ((0, 0), (0, LANES - a.shape[1])))
    wpool = jnp.zeros((pw, pw), F32)
    for g in range(len(POOL_WINDOWS)):
        sl = slice(g * POOL_GROUP_DIM, (g + 1) * POOL_GROUP_DIM)
        wpool = wpool.at[sl, sl].set(w_pool[g])
    d, f = D_MODEL, EXPERT_HIDDEN
    return {
        "g_mix": norm_mix_g.reshape(1, d),
        "wqkv": w_in[:, 0:f_off].astype(BF16),
        "wf": pad_lanes(w_in[:, f_off:p_off]).astype(BF16),
        "bf": pad_lanes(b_forget.reshape(1, N_HEADS)),
        "wp": w_in[:, p_off:g_off].astype(BF16),
        "wg": w_in[:, g_off:].astype(BF16),
        "gq": q_norm_g, "gk": k_norm_g,
        "selq": selq, "selk": selk,
        "wpool": wpool.astype(BF16),
        "pscale": pool_scale.reshape(1, pw),
        "wua": w_up_attn.astype(BF16),
        "wup": w_up_pool.astype(BF16),
        "wout": w_out.astype(BF16),
        "g_ffn": norm_ffn_g.reshape(1, d),
        "wr": pad_lanes(jnp.concatenate([w_group, w_router], axis=1)).astype(BF16),
        "br": pad_lanes(jnp.concatenate([b_group, b_router]).reshape(1, -1)),
        "w1": w1.reshape(N_EXPERTS, d, f).astype(BF16),
        "w3": w3.reshape(N_EXPERTS, d, f).astype(BF16),
        "w2": w2.reshape(N_EXPERTS, f, d).astype(BF16),
    }


def kernel(x, meta_tokens, norm_mix_g, w_in, b_forget, q_norm_g, k_norm_g, w_up_attn, w_pool,
           pool_scale, w_up_pool, w_out, norm_ffn_g, w_group, b_group, w_router, b_router,
           w1, w3, w2):
    nb, seq, d = x.shape
    p = _prepare_params(norm_mix_g[0], w_in[0], b_forget[0], q_norm_g[0], k_norm_g[0],
                        w_up_attn[0], w_pool[0], pool_scale[0], w_up_pool[0], w_out[0],
                        norm_ffn_g[0], w_group[0], b_group[0], w_router[0], b_router[0],
                        w1[0], w3[0], w2[0])

    meta = jnp.pad(meta_tokens.astype(x.dtype), ((0, LANES - N_META), (0, 0)))[None]
    _, ka_m, vt_m, _, _, u_meta, f_meta, _, _ = _inproj(
        meta, p, jnp.zeros((MAX_WINDOW, POOL_WIDTH), F32), jnp.zeros((N_HEADS, LANES), F32),
        tm=LANES, n_valid=N_META, first_pos=0, chains=1)

    tm = INPROJ_TILE
    qta, ka, vt, pooled, gates, _, _, k_sq, self_logit = _inproj(
        x, p, u_meta, f_meta, tm=tm, n_valid=tm, first_pos=N_META, chains=INPROJ_CHAINS)
    k_sq_max = jnp.max(k_sq.reshape(nb, seq // tm, N_HEADS, LANES), axis=1)
    ot = _attention(qta, ka, vt, ka_m, vt_m, k_sq_max, self_logit, tq=ATTN_TILE)
    h2, plan, cnt = _post(ot, pooled, gates, x, p, tm=POST_TILE)
    out = _moe(h2.reshape(nb * seq, d), plan, cnt[:, 0].astype(jnp.int32), p, tm=ROW_MOVE_TILE)
    return out.reshape(nb, seq, d)
```

```python
import functools
import math

import jax
import jax.numpy as jnp
import numpy as np
from jax import lax
from jax.experimental import pallas as pl
from jax.experimental.pallas import tpu as pltpu

F32 = jnp.float32
BF16 = jnp.bfloat16

D_MODEL = 1024
N_META = 16
N_HEADS = 8
HEAD_DIM = 64
ATTN_WIDTH = N_HEADS * HEAD_DIM
POOL_WINDOWS = (2, 4, 8, 16)
POOL_WIDTH = 512
POOL_GROUP_DIM = POOL_WIDTH // len(POOL_WINDOWS)
MAX_WINDOW = max(POOL_WINDOWS)
N_GROUPS = 4
N_PER_GROUP = 8
N_EXPERTS = N_GROUPS * N_PER_GROUP
EXPERT_HIDDEN = 256
RMS_EPS = 1e-6
LOG2_E = math.log2(math.e)

LANES = 128
AUG_ROWS = 16
HEAD_ROWS = HEAD_DIM + AUG_ROWS
QK_ROWS = N_HEADS * HEAD_ROWS
ROUTER_LANE0 = N_GROUPS
PAIRS_PER_GROUP = N_PER_GROUP * (N_PER_GROUP - 1) // 2
N_CLASSES = N_GROUPS * PAIRS_PER_GROUP
ROUTER_ROWS = 40
PLAN_CLASS, PLAN_RANK = 0, 1
MOE_ROWS = 352
SLOT_ROWS = D_MODEL // LANES
DMA_UNROLL = 8
DMA_QUEUES = 2
MOE_CHAINS = 2
INPROJ_CHAINS = 1
POST_CHAINS = 1
INPROJ_TILE = 1024
ATTN_TILE = 512
POST_TILE = 1024
ROW_MOVE_TILE = 512
SAFE_EXP2 = 64.0
NORM_SLACK = 1.02
VMEM_LIMIT = 56 * 1024 * 1024

_TN = (((0,), (0,)), ((), ()))


def _const_spec(shape):
    zeros = (0,) * len(shape)
    return pl.BlockSpec(shape, lambda *_: zeros, pipeline_mode=pl.Buffered(1))


def _split3(x):
    hi = x.astype(BF16).astype(F32)
    r = x - hi
    mid = r.astype(BF16).astype(F32)
    lo = (r - mid).astype(BF16).astype(F32)
    return hi, mid, lo


def _inproj_kernel(x_ref, g_ref, wqkv_ref, wf_ref, bf_ref, wp_ref, wg_ref, gq_ref, gk_ref,
                   selq_ref, selk_ref, wpool_ref, pscale_ref, uprev_ref, fprev_ref,
                   qta_ref, ka_ref, vt_ref, pooled_ref, gates_ref, utail_ref, ftail_ref, kmax_ref,
                   self_ref,
                   ubuf, fcarry, *, tm, n_valid, first_pos, chains):
    j = pl.program_id(1)

    @pl.when(j == 0)
    def _():
        ubuf[0:MAX_WINDOW, :] = uprev_ref[...]
        fcarry[...] = fprev_ref[...]

    sub = tm // chains
    r_i = lax.broadcasted_iota(jnp.int32, (sub, sub), 0)
    c_i = lax.broadcasted_iota(jnp.int32, (sub, sub), 1)
    tri = jnp.where(r_i <= c_i, 1.0, 0.0).astype(BF16)
    k_pad = jnp.zeros((LANES - HEAD_ROWS, sub), F32)
    carry = fcarry[...]
    k_sq_max = [None] * N_HEADS

    def head_norm(src, gain_ref, h):
        xh = src[h * HEAD_DIM:(h + 1) * HEAD_DIM, :]
        ssq = jnp.mean(xh * xh, axis=0, keepdims=True)
        return xh * lax.rsqrt(ssq + RMS_EPS) * gain_ref[:, 0:sub]

    for chain in range(chains):
        t0 = chain * sub
        toks = slice(t0, t0 + sub)
        x = x_ref[toks, :]
        ms = jnp.mean(x * x, axis=-1, keepdims=True)
        hn = (x * lax.rsqrt(ms + RMS_EPS) * g_ref[...]).astype(BF16)

        f = jnp.dot(hn, wf_ref[...], preferred_element_type=F32) + bf_ref[...]
        z = f.T[0:N_HEADS, :]
        logf = (jnp.minimum(z, 0.0) - jnp.log1p(jnp.exp(-jnp.abs(z)))) * LOG2_E
        parts = jnp.concatenate(_split3(logf), axis=0).astype(BF16)
        cs3 = jnp.dot(parts, tri, preferred_element_type=F32)
        cs = cs3[0:8] + cs3[8:16] + cs3[16:24]
        fc = jnp.concatenate([carry] * (sub // LANES), axis=1) + cs
        if n_valid - 1 >= t0:
            last_col = min(n_valid, t0 + sub) - 1 - t0
            carry = jnp.broadcast_to(fc[:, last_col:last_col + 1], (N_HEADS, LANES))

        pieces = jnp.concatenate(
            _split3(fc) + (jnp.ones((8, sub), F32), jnp.zeros((LANES - 32, sub), F32)),
            axis=0).astype(BF16)
        fq = jnp.dot(selq_ref[...], pieces, preferred_element_type=F32)
        fk = jnp.dot(selk_ref[...], pieces, preferred_element_type=F32)

        qkv = jnp.dot(hn, wqkv_ref[...], preferred_element_type=F32)
        qt = qkv[:, 0:ATTN_WIDTH].T
        kt = qkv[:, ATTN_WIDTH:2 * ATTN_WIDTH].T
        vt_ref[:, toks] = qkv[:, 2 * ATTN_WIDTH:3 * ATTN_WIDTH].T.astype(BF16)

        for h in range(N_HEADS):
            aug = slice(h * AUG_ROWS, (h + 1) * AUG_ROWS)
            r0 = h * HEAD_ROWS
            qn = head_norm(qt, gq_ref, h)
            qta_ref[r0:r0 + HEAD_DIM, toks] = qn.astype(BF16)
            qta_ref[r0 + HEAD_DIM:r0 + HEAD_ROWS, toks] = fq[aug, :].astype(BF16)
            kn = head_norm(kt, gk_ref, h)
            self_ref[h:h + 1, toks] = jnp.sum(qn * kn, axis=0, keepdims=True)
            k_blk = jnp.concatenate([kn, fk[aug, :], k_pad], axis=0)
            ka_ref[toks, h * LANES:(h + 1) * LANES] = k_blk.T.astype(BF16)
            k_sq = jnp.max(jnp.sum(kn * kn, axis=0, keepdims=True), axis=1, keepdims=True)
            k_sq_max[h] = k_sq if k_sq_max[h] is None else jnp.maximum(k_sq_max[h], k_sq)

        u = jnp.dot(hn, wp_ref[...], preferred_element_type=F32)
        u0 = MAX_WINDOW + t0
        ubuf[u0:u0 + sub, :] = u
        mixed = []
        for g, w in enumerate(POOL_WINDOWS):
            c0 = g * POOL_GROUP_DIM
            acc = u[:, c0:c0 + POOL_GROUP_DIM]
            for s in range(1, w):
                acc = acc + ubuf[u0 - s:u0 - s + sub, c0:c0 + POOL_GROUP_DIM]
            if first_pos + 1 >= w:
                mean = acc * (1.0 / w)
            else:
                pos = (first_pos + j * tm + t0
                       + lax.broadcasted_iota(jnp.int32, (sub, POOL_GROUP_DIM), 0))
                mean = acc / jnp.minimum(pos + 1, w).astype(F32)
            mixed.append(mean - u[:, c0:c0 + POOL_GROUP_DIM])
        mixed = jnp.concatenate(mixed, axis=1).astype(BF16)
        y = jnp.dot(mixed, wpool_ref[...], preferred_element_type=F32) * pscale_ref[...]
        pooled_ref[toks, :] = y.astype(BF16)

        gl = jnp.dot(hn, wg_ref[...], preferred_element_type=F32)
        gates_ref[toks, :] = jax.nn.sigmoid(gl).astype(BF16)

    fcarry[...] = carry
    ftail_ref[...] = carry
    for h in range(N_HEADS):
        kmax_ref[h:h + 1, :] = jnp.broadcast_to(k_sq_max[h], (1, LANES))
    tail = ubuf[n_valid:n_valid + MAX_WINDOW, :]
    ubuf[0:MAX_WINDOW, :] = tail
    utail_ref[...] = tail


def _inproj(x3, p, uprev, fprev, *, tm, n_valid, first_pos, chains):
    nb, seq, _ = x3.shape
    nt = seq // tm
    gq = jnp.broadcast_to((p["gq"] * (LOG2_E / math.sqrt(HEAD_DIM)))[:, None], (HEAD_DIM, tm))
    gk = jnp.broadcast_to(p["gk"][:, None], (HEAD_DIM, tm))
    tok = lambda width: pl.BlockSpec((None, tm, width), lambda b, j: (b, j, 0))
    chan = lambda rows: pl.BlockSpec((None, rows, tm), lambda b, j: (b, 0, j))
    kern = functools.partial(_inproj_kernel, tm=tm, n_valid=n_valid, first_pos=first_pos,
                             chains=chains)
    return pl.pallas_call(
        kern,
        grid=(nb, nt),
        in_specs=[
            tok(D_MODEL),
            _const_spec((1, D_MODEL)),
            _const_spec((D_MODEL, 3 * ATTN_WIDTH)),
            _const_spec((D_MODEL, LANES)),
            _const_spec((1, LANES)),
            _const_spec((D_MODEL, POOL_WIDTH)),
            _const_spec((D_MODEL, 2 * D_MODEL)),
            _const_spec((HEAD_DIM, tm)),
            _const_spec((HEAD_DIM, tm)),
            _const_spec((LANES, LANES)),
            _const_spec((LANES, LANES)),
            _const_spec((POOL_WIDTH, POOL_WIDTH)),
            _const_spec((1, POOL_WIDTH)),
            _const_spec((MAX_WINDOW, POOL_WIDTH)),
            _const_spec((N_HEADS, LANES)),
        ],
        out_specs=[
            chan(QK_ROWS), tok(N_HEADS * LANES), chan(ATTN_WIDTH),
            tok(POOL_WIDTH), tok(2 * D_MODEL),
            pl.BlockSpec((MAX_WINDOW, POOL_WIDTH), lambda b, j: (0, 0)),
            pl.BlockSpec((N_HEADS, LANES), lambda b, j: (0, 0)),
            pl.BlockSpec((None, N_HEADS, LANES), lambda b, j: (b * nt + j, 0, 0)),
            chan(N_HEADS),
        ],
        out_shape=[
            jax.ShapeDtypeStruct((nb, QK_ROWS, seq), BF16),
            jax.ShapeDtypeStruct((nb, seq, N_HEADS * LANES), BF16),
            jax.ShapeDtypeStruct((nb, ATTN_WIDTH, seq), BF16),
            jax.ShapeDtypeStruct((nb, seq, POOL_WIDTH), BF16),
            jax.ShapeDtypeStruct((nb, seq, 2 * D_MODEL), BF16),
            jax.ShapeDtypeStruct((MAX_WINDOW, POOL_WIDTH), F32),
            jax.ShapeDtypeStruct((N_HEADS, LANES), F32),
            jax.ShapeDtypeStruct((nb * nt, N_HEADS, LANES), F32),
            jax.ShapeDtypeStruct((nb, N_HEADS, seq), F32),
        ],
        scratch_shapes=[
            pltpu.VMEM((MAX_WINDOW + tm, POOL_WIDTH), F32),
            pltpu.VMEM((N_HEADS, LANES), F32),
        ],
        compiler_params=pltpu.CompilerParams(
            dimension_semantics=("arbitrary", "arbitrary"), vmem_limit_bytes=VMEM_LIMIT),
        name="inproj",
    )(x3, p["g_mix"], p["wqkv"], p["wf"], p["bf"], p["wp"], p["wg"], gq, gk,
      p["selq"], p["selk"], p["wpool"], p["pscale"], uprev, fprev)


def _attn_kernel(qta_ref, ka_ref, vt_ref, kam_ref, vtm_ref, bias_ref, kmax_ref, self_ref, ot_ref,
                 m_sc, acc_sc, *, tq):
    qi = pl.program_id(1)
    m_sc[...] = jnp.full(m_sc.shape, -jnp.inf, F32)
    acc_sc[...] = jnp.zeros(acc_sc.shape, F32)

    def rows(h):
        return slice(h * HEAD_ROWS, (h + 1) * HEAD_ROWS)

    def vrows(h):
        return slice(h * HEAD_DIM, (h + 1) * HEAD_DIM)

    def klanes(h):
        return slice(h * LANES, (h + 1) * LANES)

    q_pad = jnp.zeros((LANES - HEAD_ROWS, tq), BF16)

    def scores(h, ka):
        qa = jnp.concatenate([qta_ref[rows(h), :], q_pad], axis=0)
        return jnp.dot(ka, qa, preferred_element_type=F32)

    def new_max(h, s):
        return jnp.maximum(m_sc[h], jnp.max(s, axis=0, keepdims=True))

    def absorb(h, s, m_new, v):
        p = jnp.exp2(s - m_new).astype(BF16)
        va = jnp.concatenate([v, jnp.ones((AUG_ROWS, v.shape[1]), BF16)], axis=0)
        acc_sc[h] = (jnp.exp2(m_sc[h] - m_new) * acc_sc[h]
                     + jnp.dot(va, p, preferred_element_type=F32))
        m_sc[h] = m_new

    def sweep_heads(scores_of, values_of):
        s = {0: scores_of(0), 1: scores_of(1)}
        m = {0: new_max(0, s[0])}
        for h in range(N_HEADS):
            if h + 2 < N_HEADS:
                s[h + 2] = scores_of(h + 2)
            if h + 1 < N_HEADS:
                m[h + 1] = new_max(h + 1, s[h + 1])
            absorb(h, s.pop(h), m.pop(h), values_of(h))

    def sweep_heads_bounded(*tiles):
        items = [(h, s_of, v_of) for s_of, v_of in tiles for h in range(N_HEADS)]
        s_next = items[0][1](items[0][0])
        for n, (h, _, values_of) in enumerate(items):
            s = s_next
            if n + 1 < len(items):
                s_next = items[n + 1][1](items[n + 1][0])
            p = jnp.exp2(s - m_sc[h]).astype(BF16)
            v = values_of(h)
            va = jnp.concatenate([v, jnp.ones((AUG_ROWS, v.shape[1]), BF16)], axis=0)
            acc_sc[h] += jnp.dot(va, p, preferred_element_type=F32)

    d0 = pl.multiple_of(qi * tq, tq)
    diag_scores = lambda h: scores(h, jnp.concatenate(
        [ka_ref[pl.ds(d0, tq), klanes(h)], kam_ref[:, klanes(h)]], axis=0)) + bias_ref[...]
    diag_values = lambda h: jnp.concatenate(
        [vt_ref[vrows(h), pl.ds(d0, tq)], vtm_ref[vrows(h), :]], axis=1)

    def tile_scores(i):
        s0 = pl.multiple_of(i * tq, tq)
        return lambda h: scores(h, ka_ref[pl.ds(s0, tq), klanes(h)])

    def tile_values(i):
        s0 = pl.multiple_of(i * tq, tq)
        return lambda h: vt_ref[vrows(h), pl.ds(s0, tq)]

    worst = None
    for h in range(N_HEADS):
        q = qta_ref[h * HEAD_ROWS:h * HEAD_ROWS + HEAD_DIM, :].astype(F32)
        q_norm = jnp.sqrt(jnp.sum(q * q, axis=0, keepdims=True))
        k_norm = jnp.sqrt(jnp.concatenate([kmax_ref[h:h + 1, :]] * (tq // LANES), axis=1))
        gap = q_norm * k_norm * NORM_SLACK - SAFE_EXP2 - self_ref[h:h + 1, :]
        worst = gap if worst is None else jnp.maximum(worst, gap)
    bounded = jnp.max(worst) <= 0.0

    @pl.when(bounded)
    def _():
        for h in range(N_HEADS):
            m_sc[h] = self_ref[h:h + 1, :]
        sweep_heads_bounded((diag_scores, diag_values))

        def body(i, carry):
            sweep_heads_bounded((tile_scores(2 * i), tile_values(2 * i)),
                                (tile_scores(2 * i + 1), tile_values(2 * i + 1)))
            return carry
        lax.fori_loop(0, qi // 2, body, 0)

        @pl.when(qi % 2 == 1)
        def _():
            sweep_heads_bounded((tile_scores(qi - 1), tile_values(qi - 1)))

    @pl.when(jnp.logical_not(bounded))
    def _():
        sweep_heads(diag_scores, diag_values)

        def body(i, carry):
            sweep_heads(tile_scores(i), tile_values(i))
            return carry
        lax.fori_loop(0, qi, body, 0)

    for h in range(N_HEADS):
        acc = acc_sc[h]
        ot_ref[vrows(h), :] = (acc[0:HEAD_DIM, :] / acc[HEAD_DIM:HEAD_DIM + 1, :]).astype(BF16)


def _attention(qta, ka, vt, ka_m, vt_m, k_sq_max, self_logit, *, tq):
    nb, _, seq = qta.shape
    kw = N_HEADS * LANES
    key = np.arange(tq + LANES)[:, None]
    qry = np.arange(tq)[None, :]
    visible = np.where(key < tq, key <= qry, key - tq < N_META)
    bias = jnp.asarray(np.where(visible, 0.0, -np.inf), F32)
    return pl.pallas_call(
        functools.partial(_attn_kernel, tq=tq),
        grid=(nb, seq // tq),
        in_specs=[
            pl.BlockSpec((None, QK_ROWS, tq), lambda b, i: (b, 0, i)),
            pl.BlockSpec((None, seq, kw), lambda b, i: (b, 0, 0)),
            pl.BlockSpec((None, ATTN_WIDTH, seq), lambda b, i: (b, 0, 0)),
            pl.BlockSpec((None, LANES, kw), lambda b, i: (0, 0, 0)),
            pl.BlockSpec((None, ATTN_WIDTH, LANES), lambda b, i: (0, 0, 0)),
            _const_spec((tq + LANES, tq)),
            pl.BlockSpec((None, N_HEADS, LANES), lambda b, i: (b, 0, 0)),
            pl.BlockSpec((None, N_HEADS, tq), lambda b, i: (b, 0, i)),
        ],
        out_specs=pl.BlockSpec((None, ATTN_WIDTH, tq), lambda b, i: (b, 0, i)),
        out_shape=jax.ShapeDtypeStruct((nb, ATTN_WIDTH, seq), BF16),
        scratch_shapes=[
            pltpu.VMEM((N_HEADS, 1, tq), F32),
            pltpu.VMEM((N_HEADS, HEAD_ROWS, tq), F32),
        ],
        compiler_params=pltpu.CompilerParams(
            dimension_semantics=("arbitrary", "arbitrary"), vmem_limit_bytes=VMEM_LIMIT),
        name="fox_attention",
    )(qta, ka, vt, ka_m, vt_m, bias, k_sq_max, self_logit)


def _post_kernel(ot_ref, pooled_ref, gates_ref, x_ref, wua_ref, wup_ref, wout_ref, g_ref,
                 wr_ref, br_ref, h2_ref, plan_ref, cnt_ref, cnt_sc, *, tm):
    @pl.when((pl.program_id(0) == 0) & (pl.program_id(1) == 0))
    def _():
        cnt_sc[...] = jnp.zeros(cnt_sc.shape, F32)

    sub = tm // POST_CHAINS
    row = lax.broadcasted_iota(jnp.int32, (ROUTER_ROWS, sub), 0)
    crow = lax.broadcasted_iota(jnp.int32, (LANES, sub), 0)
    prow = lax.broadcasted_iota(jnp.int32, (8, sub), 0)
    r_i = lax.broadcasted_iota(jnp.int32, (sub, sub), 0)
    c_i = lax.broadcasted_iota(jnp.int32, (sub, sub), 1)
    earlier = jnp.where(r_i < c_i, 1.0, 0.0).astype(BF16)
    neg_inf = jnp.float32(-jnp.inf)
    counts = cnt_sc[...]

    for chain in range(POST_CHAINS):
        toks = slice(chain * sub, (chain + 1) * sub)
        y_attn = lax.dot_general(ot_ref[:, toks], wua_ref[...], _TN, preferred_element_type=F32)
        y_pool = jnp.dot(pooled_ref[toks, :], wup_ref[...], preferred_element_type=F32)
        merged = (gates_ref[toks, 0:D_MODEL].astype(F32) * y_attn
                  + gates_ref[toks, D_MODEL:2 * D_MODEL].astype(F32) * y_pool)
        h2 = x_ref[toks, :] + jnp.dot(merged.astype(BF16), wout_ref[...],
                                      preferred_element_type=F32)
        h2_ref[toks, :] = h2
        ms = jnp.mean(h2 * h2, axis=-1, keepdims=True)
        hn2 = (h2 * lax.rsqrt(ms + RMS_EPS) * g_ref[...]).astype(BF16)

        logits = jnp.dot(hn2, wr_ref[...], preferred_element_type=F32) + br_ref[...]
        lt = logits.T[0:ROUTER_ROWS, :]

        def softmax_over(mask):
            z = jnp.where(mask, lt, neg_inf)
            e = jnp.exp(z - jnp.max(z, axis=0, keepdims=True))
            return jnp.where(mask, e / jnp.sum(e, axis=0, keepdims=True), -1.0)

        def argtop(prob):
            top = jnp.max(prob, axis=0, keepdims=True)
            return jnp.min(jnp.where(prob == top, row, LANES), axis=0, keepdims=True)

        g_idx = argtop(softmax_over(row < N_GROUPS))
        e_lo = ROUTER_LANE0 + N_PER_GROUP * g_idx
        e_prob = softmax_over((row >= e_lo) & (row < e_lo + N_PER_GROUP))
        i1 = argtop(e_prob)
        i2 = argtop(jnp.where(row == i1, -1.0, e_prob))
        a = jnp.minimum(i1, i2) - e_lo
        b = jnp.maximum(i1, i2) - e_lo
        pair = ((a * (2 * N_PER_GROUP - 1 - a)) >> 1) + (b - a - 1)
        cls = g_idx * PAIRS_PER_GROUP + pair

        member = crow == cls
        onehot = jnp.where(member, 1.0, 0.0)
        prefix = jnp.dot(onehot.astype(BF16), earlier, preferred_element_type=F32)
        base = jnp.concatenate([counts] * (sub // LANES), axis=1)
        rank = jnp.sum(jnp.where(member, prefix + base, 0.0), axis=0, keepdims=True)
        counts = counts + jnp.broadcast_to(jnp.sum(onehot, axis=1, keepdims=True), counts.shape)
        plan_ref[:, toks] = jnp.where(prow == PLAN_CLASS, cls.astype(F32),
                                      jnp.where(prow == PLAN_RANK, rank, 0.0))

    cnt_sc[...] = counts
    cnt_ref[...] = counts


def _post(ot, pooled, gates, x, p, *, tm):
    nb, seq, _ = x.shape
    nt = seq // tm
    tok = lambda width: pl.BlockSpec((None, tm, width), lambda b, j: (b, j, 0))
    return pl.pallas_call(
        functools.partial(_post_kernel, tm=tm),
        grid=(nb, nt),
        in_specs=[
            pl.BlockSpec((None, ATTN_WIDTH, tm), lambda b, j: (b, 0, j)),
            tok(POOL_WIDTH), tok(2 * D_MODEL), tok(D_MODEL),
            _const_spec((ATTN_WIDTH, D_MODEL)),
            _const_spec((POOL_WIDTH, D_MODEL)),
            _const_spec((D_MODEL, D_MODEL)),
            _const_spec((1, D_MODEL)),
            _const_spec((D_MODEL, LANES)),
            _const_spec((1, LANES)),
        ],
        out_specs=[tok(D_MODEL),
                   pl.BlockSpec((None, 8, tm), lambda b, j: (b * nt + j, 0, 0)),
                   pl.BlockSpec((LANES, LANES), lambda b, j: (0, 0))],
        out_shape=[
            jax.ShapeDtypeStruct((nb, seq, D_MODEL), F32),
            jax.ShapeDtypeStruct((nb * nt, 8, tm), F32),
            jax.ShapeDtypeStruct((LANES, LANES), F32),
        ],
        scratch_shapes=[pltpu.VMEM((LANES, LANES), F32)],
        compiler_params=pltpu.CompilerParams(
            dimension_semantics=("arbitrary", "arbitrary"), vmem_limit_bytes=VMEM_LIMIT),
        name="post_router",
    )(ot, pooled, gates, x, p["wua"], p["wup"], p["wout"], p["g_ffn"], p["wr"], p["br"])


def _to_slots(dst, values):
    n = values.shape[0]
    for c in range(SLOT_ROWS):
        dst[pl.ds(c, n, stride=SLOT_ROWS), :] = values[:, c * LANES:(c + 1) * LANES]


def _from_slots(src, n):
    return jnp.concatenate(
        [src[pl.ds(c, n, stride=SLOT_ROWS), :] for c in range(SLOT_ROWS)], axis=1)


def _dispatch_kernel(pos_ref, h2_ref, xs_hbm, buf, zeros, sem, zsem, *, tm, n_slots):
    j = pl.program_id(0)
    slot = j % 2

    def drain(s):
        pltpu.make_async_copy(buf.at[s], xs_hbm.at[pl.ds(0, tm * SLOT_ROWS), :], sem.at[s]).wait()

    @pl.when(j >= 2)
    def _():
        drain(slot)

    _to_slots(buf.at[slot], h2_ref[...])

    def body(g, c):
        for u in range(DMA_UNROLL):
            r = g * DMA_UNROLL + u
            src = pl.multiple_of(r * SLOT_ROWS, SLOT_ROWS)
            dst = pl.multiple_of(pos_ref[0, r] * SLOT_ROWS, SLOT_ROWS)
            pltpu.make_async_copy(buf.at[slot, pl.ds(src, SLOT_ROWS), :],
                                  xs_hbm.at[pl.ds(dst, SLOT_ROWS), :],
                                  sem.at[slot]).start(priority=u % DMA_QUEUES)
        return c
    lax.fori_loop(0, tm // DMA_UNROLL, body, 0)

    @pl.when(j == pl.num_programs(0) - 1)
    def _():
        zeros[...] = jnp.zeros(zeros.shape, F32)
        pad = pltpu.make_async_copy(
            zeros, xs_hbm.at[pl.ds(n_slots * SLOT_ROWS, MOE_ROWS * SLOT_ROWS), :], zsem.at[0])
        pad.start()
        drain(slot)
        drain(1 - slot)
        pad.wait()


def _dispatch(h2, pos, *, tm):
    t = h2.shape[0]
    nt = t // tm
    return pl.pallas_call(
        functools.partial(_dispatch_kernel, tm=tm, n_slots=t),
        grid=(nt,),
        in_specs=[
            pl.BlockSpec((None, 1, tm), lambda j: (j, 0, 0), memory_space=pltpu.SMEM),
            pl.BlockSpec((tm, D_MODEL), lambda j: (j, 0)),
        ],
        out_specs=pl.BlockSpec(memory_space=pl.ANY),
        out_shape=jax.ShapeDtypeStruct(((t + MOE_ROWS) * SLOT_ROWS, LANES), F32),
        scratch_shapes=[
            pltpu.VMEM((2, tm * SLOT_ROWS, LANES), F32),
            pltpu.VMEM((MOE_ROWS * SLOT_ROWS, LANES), F32),
            pltpu.SemaphoreType.DMA((2,)),
            pltpu.SemaphoreType.DMA((1,)),
        ],
        compiler_params=pltpu.CompilerParams(
            dimension_semantics=("arbitrary",), vmem_limit_bytes=VMEM_LIMIT),
        name="moe_dispatch",
    )(pos.reshape(nt, 1, tm), h2)


def _moe_kernel(start_ref, nrows_ref, ea_ref, eb_ref, nitems_ref,
                g_ref, wr_ref, br_ref, w1a_ref, w3a_ref, w2a_ref, w1b_ref, w3b_ref, w2b_ref, xs_hbm,
                ys_hbm, xbuf, ybuf, isem, osem):
    k = pl.program_id(0)
    n_items = nitems_ref[0]
    slot = k % 2
    rows = MOE_ROWS * SLOT_ROWS

    def window(kk):
        return pl.ds(pl.multiple_of(start_ref[kk] * SLOT_ROWS, SLOT_ROWS), rows)

    def in_copy(kk, s):
        return pltpu.make_async_copy(xs_hbm.at[window(kk), :], xbuf.at[s], isem.at[s])

    def out_copy(kk, s):
        return pltpu.make_async_copy(ybuf.at[s], ys_hbm.at[window(kk), :], osem.at[s])

    @pl.when(k == 0)
    def _():
        in_copy(0, 0).start()
        ybuf[1] = jnp.zeros(ybuf.shape[1:], F32)
        pad = pltpu.make_async_copy(
            ybuf.at[1], ys_hbm.at[pl.ds(ys_hbm.shape[0] - rows, rows), :], osem.at[1])
        pad.start()
        pad.wait()

    @pl.when(k + 1 < n_items)
    def _():
        in_copy(k + 1, 1 - slot).start()

    @pl.when(k < n_items)
    def _():
        in_copy(k, slot).wait()
        ea, eb = ea_ref[k], eb_ref[k]
        sub = MOE_ROWS // MOE_CHAINS
        for chain in range(MOE_CHAINS):
            block = pl.ds(chain * sub * SLOT_ROWS, sub * SLOT_ROWS)
            h2 = _from_slots(xbuf.at[slot, block], sub)
            ms = jnp.mean(h2 * h2, axis=-1, keepdims=True)
            xn = (h2 * lax.rsqrt(ms + RMS_EPS) * g_ref[...]).astype(BF16)

            logits = jnp.dot(xn, wr_ref[...], preferred_element_type=F32) + br_ref[...]
            lane = lax.broadcasted_iota(jnp.int32, logits.shape, 1)
            pick = lambda idx: jnp.sum(jnp.where(lane == idx, logits, 0.0), axis=1,
                                       keepdims=True)
            gl = jnp.where(lane < N_GROUPS, logits, -jnp.inf)
            gmax = jnp.max(gl, axis=1, keepdims=True)
            p_group = (jnp.exp(pick(ea // N_PER_GROUP) - gmax)
                       / jnp.sum(jnp.exp(gl - gmax), axis=1, keepdims=True))
            la, lb = pick(ROUTER_LANE0 + ea), pick(ROUTER_LANE0 + eb)
            row = chain * sub + lax.broadcasted_iota(jnp.int32, (sub, 1), 0)
            valid = row < nrows_ref[k]
            w_lo = jnp.where(valid, p_group / (1.0 + jnp.exp(lb - la)), 0.0)
            w_hi = jnp.where(valid, p_group / (1.0 + jnp.exp(la - lb)), 0.0)

            def expert(w1_ref, w3_ref, w2_ref, weight):
                h1 = jnp.dot(xn, w1_ref[...], preferred_element_type=F32)
                h3 = jnp.dot(xn, w3_ref[...], preferred_element_type=F32)
                hh = h1 * jax.nn.sigmoid(h1) * h3 * weight
                return jnp.dot(hh.astype(BF16), w2_ref[...], preferred_element_type=F32)

            y = h2 + (expert(w1a_ref, w3a_ref, w2a_ref, w_lo)
                      + expert(w1b_ref, w3b_ref, w2b_ref, w_hi))
            _to_slots(ybuf.at[slot, block], y)

        @pl.when(k >= 1)
        def _():
            out_copy(k - 1, 1 - slot).wait()

        out_copy(k, slot).start()

        @pl.when(k == n_items - 1)
        def _():
            out_copy(k, slot).wait()


def _class_experts():
    ea, eb = [], []
    for g in range(N_GROUPS):
        for a in range(N_PER_GROUP):
            for b in range(a + 1, N_PER_GROUP):
                ea.append(g * N_PER_GROUP + a)
                eb.append(g * N_PER_GROUP + b)
    fill = LANES - len(ea)
    return (jnp.array(ea + [ea[-1]] * fill, jnp.int32),
            jnp.array(eb + [eb[-1]] * fill, jnp.int32))


def _combine_kernel(pos_ref, pos_next_ref, ys_hbm, out_ref, buf, sem, *, tm):
    j = pl.program_id(0)
    slot = j % 2

    def start_tile(pos, s):
        def body(g, c):
            for u in range(DMA_UNROLL):
                r = g * DMA_UNROLL + u
                src = pl.multiple_of(pos[0, r] * SLOT_ROWS, SLOT_ROWS)
                dst = pl.multiple_of(r * SLOT_ROWS, SLOT_ROWS)
                pltpu.make_async_copy(ys_hbm.at[pl.ds(src, SLOT_ROWS), :],
                                      buf.at[s, pl.ds(dst, SLOT_ROWS), :],
                                      sem.at[s]).start(priority=u % DMA_QUEUES)
            return c
        lax.fori_loop(0, tm // DMA_UNROLL, body, 0)

    @pl.when(j == 0)
    def _():
        start_tile(pos_ref, 0)

    @pl.when(j + 1 < pl.num_programs(0))
    def _():
        start_tile(pos_next_ref, 1 - slot)

    pltpu.make_async_copy(ys_hbm.at[pl.ds(0, tm * SLOT_ROWS), :], buf.at[slot], sem.at[slot]).wait()
    out_ref[...] = _from_slots(buf.at[slot], tm)


def _combine(ys, pos, *, tm):
    t = pos.shape[0]
    nt = t // tm
    pos3 = pos.reshape(nt, 1, tm)
    smem_tile = lambda fn: pl.BlockSpec((None, 1, tm), fn, memory_space=pltpu.SMEM)
    return pl.pallas_call(
        functools.partial(_combine_kernel, tm=tm),
        grid=(nt,),
        in_specs=[
            smem_tile(lambda j: (j, 0, 0)),
            smem_tile(lambda j: (jnp.minimum(j + 1, nt - 1), 0, 0)),
            pl.BlockSpec(memory_space=pl.ANY),
        ],
        out_specs=pl.BlockSpec((tm, D_MODEL), lambda j: (j, 0)),
        out_shape=jax.ShapeDtypeStruct((t, D_MODEL), F32),
        scratch_shapes=[
            pltpu.VMEM((2, tm * SLOT_ROWS, LANES), F32),
            pltpu.SemaphoreType.DMA((2,)),
        ],
        compiler_params=pltpu.CompilerParams(
            dimension_semantics=("arbitrary",), vmem_limit_bytes=VMEM_LIMIT),
        name="moe_combine",
    )(pos3, pos3, ys)


def _moe(h2, plan, counts, p, *, tm):
    t = h2.shape[0]
    assert t % tm == 0 and t // tm >= 2
    max_items = -(-t // MOE_ROWS) + N_CLASSES
    class_ids = jnp.arange(LANES, dtype=jnp.int32)
    c_end = jnp.cumsum(counts)
    c_start = c_end - counts
    cls = plan[:, PLAN_CLASS, :].reshape(t).astype(jnp.int32)
    rank = plan[:, PLAN_RANK, :].reshape(t).astype(jnp.int32)
    pos = jnp.sum(jnp.where(cls[:, None] == class_ids[None, :], c_start[None, :], 0),
                  axis=1) + rank
    n_items_c = (counts + MOE_ROWS - 1) // MOE_ROWS
    item_end = jnp.cumsum(n_items_c)
    item_start = item_end - n_items_c
    n_items = item_end[-1]
    item_ids = jnp.minimum(jnp.arange(max_items, dtype=jnp.int32), n_items - 1)
    icls = jnp.sum(item_end[None, :] <= item_ids[:, None], axis=1, dtype=jnp.int32)
    pick = lambda table: jnp.sum(
        jnp.where(icls[:, None] == class_ids[None, :], table[None, :], 0), axis=1)
    offset = (item_ids - pick(item_start)) * MOE_ROWS
    start = pick(c_start) + offset
    nrows = jnp.minimum(pick(counts) - offset, MOE_ROWS)
    ea_tab, eb_tab = _class_experts()
    ea, eb = pick(ea_tab), pick(eb_tab)

    xs = _dispatch(h2, pos, tm=tm)

    first = lambda ea, eb: ea
    second = lambda ea, eb: eb
    w13 = lambda sel: pl.BlockSpec((None, D_MODEL, EXPERT_HIDDEN),
                                   lambda k, st, nr, ea, eb, n: (sel(ea, eb)[k], 0, 0))
    w2 = lambda sel: pl.BlockSpec((None, EXPERT_HIDDEN, D_MODEL),
                                  lambda k, st, nr, ea, eb, n: (sel(ea, eb)[k], 0, 0))
    const = lambda shape: pl.BlockSpec(shape, lambda k, *_: (0, 0))
    ys = pl.pallas_call(
        _moe_kernel,
        grid_spec=pltpu.PrefetchScalarGridSpec(
            num_scalar_prefetch=5,
            grid=(max_items,),
            in_specs=[
                const((1, D_MODEL)), const((D_MODEL, LANES)), const((1, LANES)),
                w13(first), w13(first), w2(first), w13(second), w13(second), w2(second),
                pl.BlockSpec(memory_space=pl.ANY),
            ],
            out_specs=pl.BlockSpec(memory_space=pl.ANY),
            scratch_shapes=[
                pltpu.VMEM((2, MOE_ROWS * SLOT_ROWS, LANES), F32),
                pltpu.VMEM((2, MOE_ROWS * SLOT_ROWS, LANES), F32),
                pltpu.SemaphoreType.DMA((2,)),
                pltpu.SemaphoreType.DMA((2,)),
            ],
        ),
        out_shape=jax.ShapeDtypeStruct(((t + MOE_ROWS) * SLOT_ROWS, LANES), F32),
        compiler_params=pltpu.CompilerParams(
            dimension_semantics=("arbitrary",), vmem_limit_bytes=VMEM_LIMIT),
        name="moe",
    )(start, nrows, ea, eb, n_items.reshape(1), p["g_ffn"], p["wr"], p["br"],
      p["w1"], p["w3"], p["w2"], p["w1"], p["w3"], p["w2"], xs)
    return _combine(ys, pos, tm=tm)


def _bias_selectors():
    selq = [[0.0] * LANES for _ in range(LANES)]
    selk = [[0.0] * LANES for _ in range(LANES)]
    ones_row = 3 * N_HEADS
    for h in range(N_HEADS):
        for part in range(3):
            selq[h * AUG_ROWS + part][part * N_HEADS + h] = 1.0
            selq[h * AUG_ROWS + 3 + part][ones_row] = 1.0
            selk[h * AUG_ROWS + part][ones_row] = 1.0
            selk[h * AUG_ROWS + 3 + part][part * N_HEADS + h] = -1.0
    return jnp.array(selq, BF16), jnp.array(selk, BF16)


def _prepare_params(norm_mix_g, w_in, b_forget, q_norm_g, k_norm_g, w_up_attn, w_pool,
                    pool_scale, w_up_pool, w_out, norm_ffn_g, w_group, b_group, w_router,
                    b_router, w1, w3, w2):
    aw, pw = ATTN_WIDTH, POOL_WIDTH
    f_off = 3 * aw
    p_off = f_off + N_HEADS
    g_off = p_off + pw
    selq, selk = _bias_selectors()
    w_in = w_in.astype(BF16)
    pad_lanes = lambda a: jnp.pad(a, ((0, 0), (0, LANES - a.shape[1])))
    wpool = jnp.zeros((pw, pw), F32)
    for g in range(len(POOL_WINDOWS)):
        sl = slice(g * POOL_GROUP_DIM, (g + 1) * POOL_GROUP_DIM)
        wpool = wpool.at[sl, sl].set(w_pool[g])
    d, f = D_MODEL, EXPERT_HIDDEN
    return {
        "g_mix": norm_mix_g.reshape(1, d),
        "wqkv": w_in[:, 0:f_off],
        "wf": pad_lanes(w_in[:, f_off:p_off]),
        "bf": pad_lanes(b_forget.reshape(1, N_HEADS)),
        "wp": w_in[:, p_off:g_off],
        "wg": w_in[:, g_off:],
        "gq": q_norm_g, "gk": k_norm_g,
        "selq": selq, "selk": selk,
        "wpool": wpool.astype(BF16),
        "pscale": pool_scale.reshape(1, pw),
        "wua": w_up_attn.astype(BF16),
        "wup": w_up_pool.astype(BF16),
        "wout": w_out.astype(BF16),
        "g_ffn": norm_ffn_g.reshape(1, d),
        "wr": pad_lanes(jnp.concatenate([w_group, w_router], axis=1)).astype(BF16),
        "br": pad_lanes(jnp.concatenate([b_group, b_router]).reshape(1, -1)),
        "w1": w1.reshape(N_EXPERTS, d, f).astype(BF16),
        "w3": w3.reshape(N_EXPERTS, d, f).astype(BF16),
        "w2": w2.reshape(N_EXPERTS, f, d).astype(BF16),
    }


def kernel(x, meta_tokens, norm_mix_g, w_in, b_forget, q_norm_g, k_norm_g, w_up_attn, w_pool,
           pool_scale, w_up_pool, w_out, norm_ffn_g, w_group, b_group, w_router, b_router,
           w1, w3, w2):
    nb, seq, d = x.shape
    p = _prepare_params(norm_mix_g[0], w_in[0], b_forget[0], q_norm_g[0], k_norm_g[0],
                        w_up_attn[0], w_pool[0], pool_scale[0], w_up_pool[0], w_out[0],
                        norm_ffn_g[0], w_group[0], b_group[0], w_router[0], b_router[0],
                        w1[0], w3[0], w2[0])

    meta = jnp.pad(meta_tokens.astype(x.dtype), ((0, LANES - N_META), (0, 0)))[None]
    _, ka_m, vt_m, _, _, u_meta, f_meta, k_sq_meta, _ = _inproj(
        meta, p, jnp.zeros((MAX_WINDOW, POOL_WIDTH), F32), jnp.zeros((N_HEADS, LANES), F32),
        tm=LANES, n_valid=N_META, first_pos=0, chains=1)

    tm = INPROJ_TILE
    qta, ka, vt, pooled, gates, _, _, k_sq, self_logit = _inproj(
        x, p, u_meta, f_meta, tm=tm, n_valid=tm, first_pos=N_META, chains=INPROJ_CHAINS)
    k_sq_max = jnp.maximum(jnp.max(k_sq.reshape(nb, seq // tm, N_HEADS, LANES), axis=1),
                           k_sq_meta)
    ot = _attention(qta, ka, vt, ka_m, vt_m, k_sq_max, self_logit, tq=ATTN_TILE)
    h2, plan, cnt = _post(ot, pooled, gates, x, p, tm=POST_TILE)
    out = _moe(h2.reshape(nb * seq, d), plan, cnt[:, 0].astype(jnp.int32), p, tm=ROW_MOVE_TILE)
    return out.reshape(nb, seq, d)
```

```python
import functools
import math

import jax
import jax.numpy as jnp
import numpy as np
from jax import lax
from jax.experimental import pallas as pl
from jax.experimental.pallas import tpu as pltpu

F32 = jnp.float32
BF16 = jnp.bfloat16

D_MODEL = 1024
N_META = 16
N_HEADS = 8
HEAD_DIM = 64
ATTN_WIDTH = N_HEADS * HEAD_DIM
POOL_WINDOWS = (2, 4, 8, 16)
POOL_WIDTH = 512
POOL_GROUP_DIM = POOL_WIDTH // len(POOL_WINDOWS)
MAX_WINDOW = max(POOL_WINDOWS)
N_GROUPS = 4
N_PER_GROUP = 8
N_EXPERTS = N_GROUPS * N_PER_GROUP
EXPERT_HIDDEN = 256
RMS_EPS = 1e-6
LOG2_E = math.log2(math.e)

LANES = 128
AUG_ROWS = 16
HEAD_ROWS = HEAD_DIM + AUG_ROWS
QK_ROWS = N_HEADS * HEAD_ROWS
ROUTER_LANE0 = N_GROUPS
PAIRS_PER_GROUP = N_PER_GROUP * (N_PER_GROUP - 1) // 2
N_CLASSES = N_GROUPS * PAIRS_PER_GROUP
ROUTER_ROWS = 40
PLAN_CLASS, PLAN_RANK = 0, 1
MOE_ROWS = 352
SLOT_ROWS = D_MODEL // LANES
DMA_UNROLL = 8
DMA_QUEUES = 2
MOE_CHAINS = 2
INPROJ_CHAINS = 1
POST_CHAINS = 1
INPROJ_TILE = 1024
ATTN_TILE = 512
POST_TILE = 1024
ROW_MOVE_TILE = 512
SAFE_EXP2 = 64.0
NORM_SLACK = 1.02
VMEM_LIMIT = 56 * 1024 * 1024

_TN = (((0,), (0,)), ((), ()))


def _const_spec(shape):
    zeros = (0,) * len(shape)
    return pl.BlockSpec(shape, lambda *_: zeros, pipeline_mode=pl.Buffered(1))


def _split3(x):
    hi = x.astype(BF16).astype(F32)
    r = x - hi
    mid = r.astype(BF16).astype(F32)
    lo = (r - mid).astype(BF16).astype(F32)
    return hi, mid, lo


def _inproj_kernel(x_ref, g_ref, wqkv_ref, wf_ref, bf_ref, wp_ref, wg_ref, gq_ref, gk_ref,
                   selq_ref, selk_ref, wpool_ref, pscale_ref, uprev_ref, fprev_ref,
                   qta_ref, ka_ref, vt_ref, pooled_ref, gates_ref, utail_ref, ftail_ref, kmax_ref,
                   self_ref,
                   ubuf, fcarry, *, tm, n_valid, first_pos, chains):
    j = pl.program_id(1)

    @pl.when(j == 0)
    def _():
        ubuf[0:MAX_WINDOW, :] = uprev_ref[...]
        fcarry[...] = fprev_ref[...]

    sub = tm // chains
    r_i = lax.broadcasted_iota(jnp.int32, (sub, sub), 0)
    c_i = lax.broadcasted_iota(jnp.int32, (sub, sub), 1)
    tri = jnp.where(r_i <= c_i, 1.0, 0.0).astype(BF16)
    k_pad = jnp.zeros((LANES - HEAD_ROWS, sub), F32)
    carry = fcarry[...]
    k_sq_max = [None] * N_HEADS

    def head_norm(src, gain_ref, h):
        xh = src[h * HEAD_DIM:(h + 1) * HEAD_DIM, :]
        ssq = jnp.mean(xh * xh, axis=0, keepdims=True)
        return xh * lax.rsqrt(ssq + RMS_EPS) * gain_ref[:, 0:sub]

    for chain in range(chains):
        t0 = chain * sub
        toks = slice(t0, t0 + sub)
        x = x_ref[toks, :]
        ms = jnp.mean(x * x, axis=-1, keepdims=True)
        hn = (x * lax.rsqrt(ms + RMS_EPS) * g_ref[...]).astype(BF16)

        f = jnp.dot(hn, wf_ref[...], preferred_element_type=F32) + bf_ref[...]
        z = f.T[0:N_HEADS, :]
        logf = (jnp.minimum(z, 0.0) - jnp.log1p(jnp.exp(-jnp.abs(z)))) * LOG2_E
        parts = jnp.concatenate(_split3(logf), axis=0).astype(BF16)
        cs3 = jnp.dot(parts, tri, preferred_element_type=F32)
        cs = cs3[0:8] + cs3[8:16] + cs3[16:24]
        fc = jnp.concatenate([carry] * (sub // LANES), axis=1) + cs
        if n_valid - 1 >= t0:
            last_col = min(n_valid, t0 + sub) - 1 - t0
            carry = jnp.broadcast_to(fc[:, last_col:last_col + 1], (N_HEADS, LANES))

        pieces = jnp.concatenate(
            _split3(fc) + (jnp.ones((8, sub), F32), jnp.zeros((LANES - 32, sub), F32)),
            axis=0).astype(BF16)
        fq = jnp.dot(selq_ref[...], pieces, preferred_element_type=F32)
        fk = jnp.dot(selk_ref[...], pieces, preferred_element_type=F32)

        qkv = jnp.dot(hn, wqkv_ref[...], preferred_element_type=F32)
        qt = qkv[:, 0:ATTN_WIDTH].T
        kt = qkv[:, ATTN_WIDTH:2 * ATTN_WIDTH].T
        vt_ref[:, toks] = qkv[:, 2 * ATTN_WIDTH:3 * ATTN_WIDTH].T.astype(BF16)

        for h in range(N_HEADS):
            aug = slice(h * AUG_ROWS, (h + 1) * AUG_ROWS)
            r0 = h * HEAD_ROWS
            qn = head_norm(qt, gq_ref, h)
            qta_ref[r0:r0 + HEAD_DIM, toks] = qn.astype(BF16)
            qta_ref[r0 + HEAD_DIM:r0 + HEAD_ROWS, toks] = fq[aug, :].astype(BF16)
            kn = head_norm(kt, gk_ref, h)
            self_ref[h:h + 1, toks] = jnp.sum(qn * kn, axis=0, keepdims=True)
            k_blk = jnp.concatenate([kn, fk[aug, :], k_pad], axis=0)
            ka_ref[toks, h * LANES:(h + 1) * LANES] = k_blk.T.astype(BF16)
            k_sq = jnp.max(jnp.sum(kn * kn, axis=0, keepdims=True), axis=1, keepdims=True)
            k_sq_max[h] = k_sq if k_sq_max[h] is None else jnp.maximum(k_sq_max[h], k_sq)

        u = jnp.dot(hn, wp_ref[...], preferred_element_type=F32)
        u0 = MAX_WINDOW + t0
        ubuf[u0:u0 + sub, :] = u
        mixed = []
        for g, w in enumerate(POOL_WINDOWS):
            c0 = g * POOL_GROUP_DIM
            acc = u[:, c0:c0 + POOL_GROUP_DIM]
            for s in range(1, w):
                acc = acc + ubuf[u0 - s:u0 - s + sub, c0:c0 + POOL_GROUP_DIM]
            if first_pos + 1 >= w:
                mean = acc * (1.0 / w)
            else:
                pos = (first_pos + j * tm + t0
                       + lax.broadcasted_iota(jnp.int32, (sub, POOL_GROUP_DIM), 0))
                mean = acc / jnp.minimum(pos + 1, w).astype(F32)
            mixed.append(mean - u[:, c0:c0 + POOL_GROUP_DIM])
        mixed = jnp.concatenate(mixed, axis=1).astype(BF16)
        y = jnp.dot(mixed, wpool_ref[...], preferred_element_type=F32) * pscale_ref[...]
        pooled_ref[toks, :] = y.astype(BF16)

        gl = jnp.dot(hn, wg_ref[...], preferred_element_type=F32)
        gates_ref[toks, :] = jax.nn.sigmoid(gl).astype(BF16)

    fcarry[...] = carry
    ftail_ref[...] = carry
    for h in range(N_HEADS):
        kmax_ref[h:h + 1, :] = jnp.broadcast_to(k_sq_max[h], (1, LANES))
    tail = ubuf[n_valid:n_valid + MAX_WINDOW, :]
    ubuf[0:MAX_WINDOW, :] = tail
    utail_ref[...] = tail


def _inproj(x3, p, uprev, fprev, *, tm, n_valid, first_pos, chains):
    nb, seq, _ = x3.shape
    nt = seq // tm
    gq = jnp.broadcast_to((p["gq"] * (LOG2_E / math.sqrt(HEAD_DIM)))[:, None], (HEAD_DIM, tm))
    gk = jnp.broadcast_to(p["gk"][:, None], (HEAD_DIM, tm))
    tok = lambda width: pl.BlockSpec((None, tm, width), lambda b, j: (b, j, 0))
    chan = lambda rows: pl.BlockSpec((None, rows, tm), lambda b, j: (b, 0, j))
    kern = functools.partial(_inproj_kernel, tm=tm, n_valid=n_valid, first_pos=first_pos,
                             chains=chains)
    return pl.pallas_call(
        kern,
        grid=(nb, nt),
        in_specs=[
            tok(D_MODEL),
            _const_spec((1, D_MODEL)),
            _const_spec((D_MODEL, 3 * ATTN_WIDTH)),
            _const_spec((D_MODEL, LANES)),
            _const_spec((1, LANES)),
            _const_spec((D_MODEL, POOL_WIDTH)),
            _const_spec((D_MODEL, 2 * D_MODEL)),
            _const_spec((HEAD_DIM, tm)),
            _const_spec((HEAD_DIM, tm)),
            _const_spec((LANES, LANES)),
            _const_spec((LANES, LANES)),
            _const_spec((POOL_WIDTH, POOL_WIDTH)),
            _const_spec((1, POOL_WIDTH)),
            _const_spec((MAX_WINDOW, POOL_WIDTH)),
            _const_spec((N_HEADS, LANES)),
        ],
        out_specs=[
            chan(QK_ROWS), tok(N_HEADS * LANES), chan(ATTN_WIDTH),
            tok(POOL_WIDTH), tok(2 * D_MODEL),
            pl.BlockSpec((MAX_WINDOW, POOL_WIDTH), lambda b, j: (0, 0)),
            pl.BlockSpec((N_HEADS, LANES), lambda b, j: (0, 0)),
            pl.BlockSpec((None, N_HEADS, LANES), lambda b, j: (b * nt + j, 0, 0)),
            chan(N_HEADS),
        ],
        out_shape=[
            jax.ShapeDtypeStruct((nb, QK_ROWS, seq), BF16),
            jax.ShapeDtypeStruct((nb, seq, N_HEADS * LANES), BF16),
            jax.ShapeDtypeStruct((nb, ATTN_WIDTH, seq), BF16),
            jax.ShapeDtypeStruct((nb, seq, POOL_WIDTH), BF16),
            jax.ShapeDtypeStruct((nb, seq, 2 * D_MODEL), BF16),
            jax.ShapeDtypeStruct((MAX_WINDOW, POOL_WIDTH), F32),
            jax.ShapeDtypeStruct((N_HEADS, LANES), F32),
            jax.ShapeDtypeStruct((nb * nt, N_HEADS, LANES), F32),
            jax.ShapeDtypeStruct((nb, N_HEADS, seq), F32),
        ],
        scratch_shapes=[
            pltpu.VMEM((MAX_WINDOW + tm, POOL_WIDTH), F32),
            pltpu.VMEM((N_HEADS, LANES), F32),
        ],
        compiler_params=pltpu.CompilerParams(
            dimension_semantics=("arbitrary", "arbitrary"), vmem_limit_bytes=VMEM_LIMIT),
        name="inproj",
    )(x3, p["g_mix"], p["wqkv"], p["wf"], p["bf"], p["wp"], p["wg"], gq, gk,
      p["selq"], p["selk"], p["wpool"], p["pscale"], uprev, fprev)


def _attn_kernel(qta_ref, ka_ref, vt_ref, kam_ref, vtm_ref, bias_ref, kmax_ref, self_ref, ot_ref,
                 m_sc, acc_sc, *, tq):
    qi = pl.program_id(1)
    m_sc[...] = jnp.full(m_sc.shape, -jnp.inf, F32)
    acc_sc[...] = jnp.zeros(acc_sc.shape, F32)

    def rows(h):
        return slice(h * HEAD_ROWS, (h + 1) * HEAD_ROWS)

    def vrows(h):
        return slice(h * HEAD_DIM, (h + 1) * HEAD_DIM)

    def klanes(h):
        return slice(h * LANES, (h + 1) * LANES)

    q_pad = jnp.zeros((LANES - HEAD_ROWS, tq), BF16)

    def scores(h, ka):
        qa = jnp.concatenate([qta_ref[rows(h), :], q_pad], axis=0)
        return jnp.dot(ka, qa, preferred_element_type=F32)

    def new_max(h, s):
        return jnp.maximum(m_sc[h], jnp.max(s, axis=0, keepdims=True))

    def absorb(h, s, m_new, v):
        p = jnp.exp2(s - m_new).astype(BF16)
        va = jnp.concatenate([v, jnp.ones((AUG_ROWS, v.shape[1]), BF16)], axis=0)
        acc_sc[h] = (jnp.exp2(m_sc[h] - m_new) * acc_sc[h]
                     + jnp.dot(va, p, preferred_element_type=F32))
        m_sc[h] = m_new

    def sweep_heads(scores_of, values_of):
        s = {0: scores_of(0), 1: scores_of(1)}
        m = {0: new_max(0, s[0])}
        for h in range(N_HEADS):
            if h + 2 < N_HEADS:
                s[h + 2] = scores_of(h + 2)
            if h + 1 < N_HEADS:
                m[h + 1] = new_max(h + 1, s[h + 1])
            absorb(h, s.pop(h), m.pop(h), values_of(h))

    def sweep_heads_bounded(*tiles):
        items = [(h, s_of, v_of) for s_of, v_of in tiles for h in range(N_HEADS)]
        s_next = items[0][1](items[0][0])
        for n, (h, _, values_of) in enumerate(items):
            s = s_next
            if n + 1 < len(items):
                s_next = items[n + 1][1](items[n + 1][0])
            p = jnp.exp2(s - m_sc[h]).astype(BF16)
            v = values_of(h)
            va = jnp.concatenate([v, jnp.ones((AUG_ROWS, v.shape[1]), BF16)], axis=0)
            acc_sc[h] += jnp.dot(va, p, preferred_element_type=F32)

    d0 = pl.multiple_of(qi * tq, tq)
    diag_scores = lambda h: scores(h, jnp.concatenate(
        [ka_ref[pl.ds(d0, tq), klanes(h)], kam_ref[:, klanes(h)]], axis=0)) + bias_ref[...]
    diag_values = lambda h: jnp.concatenate(
        [vt_ref[vrows(h), pl.ds(d0, tq)], vtm_ref[vrows(h), :]], axis=1)

    def tile_scores(i):
        s0 = pl.multiple_of(i * tq, tq)
        return lambda h: scores(h, ka_ref[pl.ds(s0, tq), klanes(h)])

    def tile_values(i):
        s0 = pl.multiple_of(i * tq, tq)
        return lambda h: vt_ref[vrows(h), pl.ds(s0, tq)]

    worst = None
    for h in range(N_HEADS):
        q = qta_ref[h * HEAD_ROWS:h * HEAD_ROWS + HEAD_DIM, :].astype(F32)
        q_norm = jnp.sqrt(jnp.sum(q * q, axis=0, keepdims=True))
        k_norm = jnp.sqrt(jnp.concatenate([kmax_ref[h:h + 1, :]] * (tq // LANES), axis=1))
        gap = q_norm * k_norm * NORM_SLACK - SAFE_EXP2 - self_ref[h:h + 1, :]
        worst = gap if worst is None else jnp.maximum(worst, gap)
    bounded = jnp.max(worst) <= 0.0

    @pl.when(bounded)
    def _():
        for h in range(N_HEADS):
            m_sc[h] = self_ref[h:h + 1, :]
        sweep_heads_bounded((diag_scores, diag_values))

        def body(i, carry):
            sweep_heads_bounded((tile_scores(2 * i), tile_values(2 * i)),
                                (tile_scores(2 * i + 1), tile_values(2 * i + 1)))
            return carry
        lax.fori_loop(0, qi // 2, body, 0)

        @pl.when(qi % 2 == 1)
        def _():
            sweep_heads_bounded((tile_scores(qi - 1), tile_values(qi - 1)))

    @pl.when(jnp.logical_not(bounded))
    def _():
        sweep_heads(diag_scores, diag_values)

        def body(i, carry):
            sweep_heads(tile_scores(i), tile_values(i))
            return carry
        lax.fori_loop(0, qi, body, 0)

    for h in range(N_HEADS):
        acc = acc_sc[h]
        ot_ref[vrows(h), :] = (acc[0:HEAD_DIM, :] / acc[HEAD_DIM:HEAD_DIM + 1, :]).astype(BF16)


def _attention(qta, ka, vt, ka_m, vt_m, k_sq_max, self_logit, *, tq):
    nb, _, seq = qta.shape
    kw = N_HEADS * LANES
    key = np.arange(tq + LANES)[:, None]
    qry = np.arange(tq)[None, :]
    visible = np.where(key < tq, key <= qry, key - tq < N_META)
    bias = jnp.asarray(np.where(visible, 0.0, -np.inf), F32)
    return pl.pallas_call(
        functools.partial(_attn_kernel, tq=tq),
        grid=(nb, seq // tq),
        in_specs=[
            pl.BlockSpec((None, QK_ROWS, tq), lambda b, i: (b, 0, i)),
            pl.BlockSpec((None, seq, kw), lambda b, i: (b, 0, 0)),
            pl.BlockSpec((None, ATTN_WIDTH, seq), lambda b, i: (b, 0, 0)),
            pl.BlockSpec((None, LANES, kw), lambda b, i: (0, 0, 0)),
            pl.BlockSpec((None, ATTN_WIDTH, LANES), lambda b, i: (0, 0, 0)),
            _const_spec((tq + LANES, tq)),
            pl.BlockSpec((None, N_HEADS, LANES), lambda b, i: (b, 0, 0)),
            pl.BlockSpec((None, N_HEADS, tq), lambda b, i: (b, 0, i)),
        ],
        out_specs=pl.BlockSpec((None, ATTN_WIDTH, tq), lambda b, i: (b, 0, i)),
        out_shape=jax.ShapeDtypeStruct((nb, ATTN_WIDTH, seq), BF16),
        scratch_shapes=[
            pltpu.VMEM((N_HEADS, 1, tq), F32),
            pltpu.VMEM((N_HEADS, HEAD_ROWS, tq), F32),
        ],
        compiler_params=pltpu.CompilerParams(
            dimension_semantics=("arbitrary", "arbitrary"), vmem_limit_bytes=VMEM_LIMIT),
        name="fox_attention",
    )(qta, ka, vt, ka_m, vt_m, bias, k_sq_max, self_logit)


def _post_kernel(ot_ref, pooled_ref, gates_ref, x_ref, wua_ref, wup_ref, wout_ref, g_ref,
                 wr_ref, br_ref, h2_ref, plan_ref, cnt_ref, cnt_sc, *, tm):
    @pl.when((pl.program_id(0) == 0) & (pl.program_id(1) == 0))
    def _():
        cnt_sc[...] = jnp.zeros(cnt_sc.shape, F32)

    sub = tm // POST_CHAINS
    row = lax.broadcasted_iota(jnp.int32, (ROUTER_ROWS, sub), 0)
    crow = lax.broadcasted_iota(jnp.int32, (LANES, sub), 0)
    prow = lax.broadcasted_iota(jnp.int32, (8, sub), 0)
    r_i = lax.broadcasted_iota(jnp.int32, (sub, sub), 0)
    c_i = lax.broadcasted_iota(jnp.int32, (sub, sub), 1)
    earlier = jnp.where(r_i < c_i, 1.0, 0.0).astype(BF16)
    neg_inf = jnp.float32(-jnp.inf)
    counts = cnt_sc[...]

    for chain in range(POST_CHAINS):
        toks = slice(chain * sub, (chain + 1) * sub)
        y_attn = lax.dot_general(ot_ref[:, toks], wua_ref[...], _TN, preferred_element_type=F32)
        y_pool = jnp.dot(pooled_ref[toks, :], wup_ref[...], preferred_element_type=F32)
        merged = (gates_ref[toks, 0:D_MODEL].astype(F32) * y_attn
                  + gates_ref[toks, D_MODEL:2 * D_MODEL].astype(F32) * y_pool)
        h2 = x_ref[toks, :] + jnp.dot(merged.astype(BF16), wout_ref[...],
                                      preferred_element_type=F32)
        h2_ref[toks, :] = h2
        ms = jnp.mean(h2 * h2, axis=-1, keepdims=True)
        hn2 = (h2 * lax.rsqrt(ms + RMS_EPS) * g_ref[...]).astype(BF16)

        logits = jnp.dot(hn2, wr_ref[...], preferred_element_type=F32) + br_ref[...]
        lt = logits.T[0:ROUTER_ROWS, :]

        def softmax_over(mask):
            z = jnp.where(mask, lt, neg_inf)
            e = jnp.exp(z - jnp.max(z, axis=0, keepdims=True))
            return jnp.where(mask, e / jnp.sum(e, axis=0, keepdims=True), -1.0)

        def argtop(prob):
            top = jnp.max(prob, axis=0, keepdims=True)
            return jnp.min(jnp.where(prob == top, row, LANES), axis=0, keepdims=True)

        g_idx = argtop(softmax_over(row < N_GROUPS))
        e_lo = ROUTER_LANE0 + N_PER_GROUP * g_idx
        e_prob = softmax_over((row >= e_lo) & (row < e_lo + N_PER_GROUP))
        i1 = argtop(e_prob)
        i2 = argtop(jnp.where(row == i1, -1.0, e_prob))
        a = jnp.minimum(i1, i2) - e_lo
        b = jnp.maximum(i1, i2) - e_lo
        pair = ((a * (2 * N_PER_GROUP - 1 - a)) >> 1) + (b - a - 1)
        cls = g_idx * PAIRS_PER_GROUP + pair

        member = crow == cls
        onehot = jnp.where(member, 1.0, 0.0)
        prefix = jnp.dot(onehot.astype(BF16), earlier, preferred_element_type=F32)
        base = jnp.concatenate([counts] * (sub // LANES), axis=1)
        rank = jnp.sum(jnp.where(member, prefix + base, 0.0), axis=0, keepdims=True)
        counts = counts + jnp.broadcast_to(jnp.sum(onehot, axis=1, keepdims=True), counts.shape)
        plan_ref[:, toks] = jnp.where(prow == PLAN_CLASS, cls.astype(F32),
                                      jnp.where(prow == PLAN_RANK, rank, 0.0))

    cnt_sc[...] = counts
    cnt_ref[...] = counts


def _post(ot, pooled, gates, x, p, *, tm):
    nb, seq, _ = x.shape
    nt = seq // tm
    tok = lambda width: pl.BlockSpec((None, tm, width), lambda b, j: (b, j, 0))
    return pl.pallas_call(
        functools.partial(_post_kernel, tm=tm),
        grid=(nb, nt),
        in_specs=[
            pl.BlockSpec((None, ATTN_WIDTH, tm), lambda b, j: (b, 0, j)),
            tok(POOL_WIDTH), tok(2 * D_MODEL), tok(D_MODEL),
            _const_spec((ATTN_WIDTH, D_MODEL)),
            _const_spec((POOL_WIDTH, D_MODEL)),
            _const_spec((D_MODEL, D_MODEL)),
            _const_spec((1, D_MODEL)),
            _const_spec((D_MODEL, LANES)),
            _const_spec((1, LANES)),
        ],
        out_specs=[tok(D_MODEL),
                   pl.BlockSpec((None, 8, tm), lambda b, j: (b * nt + j, 0, 0)),
                   pl.BlockSpec((LANES, LANES), lambda b, j: (0, 0))],
        out_shape=[
            jax.ShapeDtypeStruct((nb, seq, D_MODEL), F32),
            jax.ShapeDtypeStruct((nb * nt, 8, tm), F32),
            jax.ShapeDtypeStruct((LANES, LANES), F32),
        ],
        scratch_shapes=[pltpu.VMEM((LANES, LANES), F32)],
        compiler_params=pltpu.CompilerParams(
            dimension_semantics=("arbitrary", "arbitrary"), vmem_limit_bytes=VMEM_LIMIT),
        name="post_router",
    )(ot, pooled, gates, x, p["wua"], p["wup"], p["wout"], p["g_ffn"], p["wr"], p["br"])


def _to_slots(dst, values):
    n = values.shape[0]
    for c in range(SLOT_ROWS):
        dst[pl.ds(c, n, stride=SLOT_ROWS), :] = values[:, c * LANES:(c + 1) * LANES]


def _from_slots(src, n):
    return jnp.concatenate(
        [src[pl.ds(c, n, stride=SLOT_ROWS), :] for c in range(SLOT_ROWS)], axis=1)


def _dispatch_kernel(pos_ref, h2_ref, xs_hbm, buf, zeros, sem, zsem, *, tm, n_slots):
    j = pl.program_id(0)
    slot = j % 2

    def drain(s):
        pltpu.make_async_copy(buf.at[s], xs_hbm.at[pl.ds(0, tm * SLOT_ROWS), :], sem.at[s]).wait()

    @pl.when(j >= 2)
    def _():
        drain(slot)

    _to_slots(buf.at[slot], h2_ref[...])

    def body(g, c):
        for u in range(DMA_UNROLL):
            r = g * DMA_UNROLL + u
            src = pl.multiple_of(r * SLOT_ROWS, SLOT_ROWS)
            dst = pl.multiple_of(pos_ref[0, r] * SLOT_ROWS, SLOT_ROWS)
            pltpu.make_async_copy(buf.at[slot, pl.ds(src, SLOT_ROWS), :],
                                  xs_hbm.at[pl.ds(dst, SLOT_ROWS), :],
                                  sem.at[slot]).start(priority=u % DMA_QUEUES)
        return c
    lax.fori_loop(0, tm // DMA_UNROLL, body, 0)

    @pl.when(j == pl.num_programs(0) - 1)
    def _():
        zeros[...] = jnp.zeros(zeros.shape, F32)
        pad = pltpu.make_async_copy(
            zeros, xs_hbm.at[pl.ds(n_slots * SLOT_ROWS, MOE_ROWS * SLOT_ROWS), :], zsem.at[0])
        pad.start()
        drain(slot)
        drain(1 - slot)
        pad.wait()


def _dispatch(h2, pos, *, tm):
    t = h2.shape[0]
    nt = t // tm
    return pl.pallas_call(
        functools.partial(_dispatch_kernel, tm=tm, n_slots=t),
        grid=(nt,),
        in_specs=[
            pl.BlockSpec((None, 1, tm), lambda j: (j, 0, 0), memory_space=pltpu.SMEM),
            pl.BlockSpec((tm, D_MODEL), lambda j: (j, 0)),
        ],
        out_specs=pl.BlockSpec(memory_space=pl.ANY),
        out_shape=jax.ShapeDtypeStruct(((t + MOE_ROWS) * SLOT_ROWS, LANES), F32),
        scratch_shapes=[
            pltpu.VMEM((2, tm * SLOT_ROWS, LANES), F32),
            pltpu.VMEM((MOE_ROWS * SLOT_ROWS, LANES), F32),
            pltpu.SemaphoreType.DMA((2,)),
            pltpu.SemaphoreType.DMA((1,)),
        ],
        compiler_params=pltpu.CompilerParams(
            dimension_semantics=("arbitrary",), vmem_limit_bytes=VMEM_LIMIT),
        name="moe_dispatch",
    )(pos.reshape(nt, 1, tm), h2)


def _moe_kernel(start_ref, nrows_ref, ea_ref, eb_ref, nitems_ref,
                g_ref, wr_ref, br_ref, w1a_ref, w3a_ref, w2a_ref, w1b_ref, w3b_ref, w2b_ref, xs_hbm,
                ys_hbm, xbuf, ybuf, isem, osem):
    k = pl.program_id(0)
    n_items = nitems_ref[0]
    slot = k % 2
    rows = MOE_ROWS * SLOT_ROWS

    def window(kk):
        return pl.ds(pl.multiple_of(start_ref[kk] * SLOT_ROWS, SLOT_ROWS), rows)

    def in_copy(kk, s):
        return pltpu.make_async_copy(xs_hbm.at[window(kk), :], xbuf.at[s], isem.at[s])

    def out_copy(kk, s):
        return pltpu.make_async_copy(ybuf.at[s], ys_hbm.at[window(kk), :], osem.at[s])

    @pl.when(k == 0)
    def _():
        in_copy(0, 0).start()
        ybuf[1] = jnp.zeros(ybuf.shape[1:], F32)
        pad = pltpu.make_async_copy(
            ybuf.at[1], ys_hbm.at[pl.ds(ys_hbm.shape[0] - rows, rows), :], osem.at[1])
        pad.start()
        pad.wait()

    @pl.when(k + 1 < n_items)
    def _():
        in_copy(k + 1, 1 - slot).start()

    @pl.when(k < n_items)
    def _():
        in_copy(k, slot).wait()
        ea, eb = ea_ref[k], eb_ref[k]
        sub = MOE_ROWS // MOE_CHAINS
        for chain in range(MOE_CHAINS):
            block = pl.ds(chain * sub * SLOT_ROWS, sub * SLOT_ROWS)
            h2 = _from_slots(xbuf.at[slot, block], sub)
            ms = jnp.mean(h2 * h2, axis=-1, keepdims=True)
            xn = (h2 * lax.rsqrt(ms + RMS_EPS) * g_ref[...]).astype(BF16)

            logits = jnp.dot(xn, wr_ref[...], preferred_element_type=F32) + br_ref[...]
            lane = lax.broadcasted_iota(jnp.int32, logits.shape, 1)
            pick = lambda idx: jnp.sum(jnp.where(lane == idx, logits, 0.0), axis=1,
                                       keepdims=True)
            gl = jnp.where(lane < N_GROUPS, logits, -jnp.inf)
            gmax = jnp.max(gl, axis=1, keepdims=True)
            p_group = (jnp.exp(pick(ea // N_PER_GROUP) - gmax)
                       / jnp.sum(jnp.exp(gl - gmax), axis=1, keepdims=True))
            la, lb = pick(ROUTER_LANE0 + ea), pick(ROUTER_LANE0 + eb)
            row = chain * sub + lax.broadcasted_iota(jnp.int32, (sub, 1), 0)
            valid = row < nrows_ref[k]
            w_lo = jnp.where(valid, p_group / (1.0 + jnp.exp(lb - la)), 0.0)
            w_hi = jnp.where(valid, p_group / (1.0 + jnp.exp(la - lb)), 0.0)

            def expert(w1_ref, w3_ref, w2_ref, weight):
                h1 = jnp.dot(xn, w1_ref[...], preferred_element_type=F32)
                h3 = jnp.dot(xn, w3_ref[...], preferred_element_type=F32)
                hh = h1 * jax.nn.sigmoid(h1) * h3 * weight
                return jnp.dot(hh.astype(BF16), w2_ref[...], preferred_element_type=F32)

            y = h2 + (expert(w1a_ref, w3a_ref, w2a_ref, w_lo)
                      + expert(w1b_ref, w3b_ref, w2b_ref, w_hi))
            _to_slots(ybuf.at[slot, block], y)

        @pl.when(k >= 1)
        def _():
            out_copy(k - 1, 1 - slot).wait()

        out_copy(k, slot).start()

        @pl.when(k == n_items - 1)
        def _():
            out_copy(k, slot).wait()


def _class_experts():
    ea, eb = [], []
    for g in range(N_GROUPS):
        for a in range(N_PER_GROUP):
            for b in range(a + 1, N_PER_GROUP):
                ea.append(g * N_PER_GROUP + a)
                eb.append(g * N_PER_GROUP + b)
    fill = LANES - len(ea)
    return (jnp.array(ea + [ea[-1]] * fill, jnp.int32),
            jnp.array(eb + [eb[-1]] * fill, jnp.int32))


def _combine_kernel(pos_ref, pos_next_ref, ys_hbm, out_ref, buf, sem, *, tm):
    j = pl.program_id(0)
    slot = j % 2

    def start_tile(pos, s):
        def body(g, c):
            for u in range(DMA_UNROLL):
                r = g * DMA_UNROLL + u
                src = pl.multiple_of(pos[0, r] * SLOT_ROWS, SLOT_ROWS)
                dst = pl.multiple_of(r * SLOT_ROWS, SLOT_ROWS)
                pltpu.make_async_copy(ys_hbm.at[pl.ds(src, SLOT_ROWS), :],
                                      buf.at[s, pl.ds(dst, SLOT_ROWS), :],
                                      sem.at[s]).start(priority=u % DMA_QUEUES)
            return c
        lax.fori_loop(0, tm // DMA_UNROLL, body, 0)

    @pl.when(j == 0)
    def _():
        start_tile(pos_ref, 0)

    @pl.when(j + 1 < pl.num_programs(0))
    def _():
        start_tile(pos_next_ref, 1 - slot)

    pltpu.make_async_copy(ys_hbm.at[pl.ds(0, tm * SLOT_ROWS), :], buf.at[slot], sem.at[slot]).wait()
    out_ref[...] = _from_slots(buf.at[slot], tm)


def _combine(ys, pos, *, tm):
    t = pos.shape[0]
    nt = t // tm
    pos3 = pos.reshape(nt, 1, tm)
    smem_tile = lambda fn: pl.BlockSpec((None, 1, tm), fn, memory_space=pltpu.SMEM)
    return pl.pallas_call(
        functools.partial(_combine_kernel, tm=tm),
        grid=(nt,),
        in_specs=[
            smem_tile(lambda j: (j, 0, 0)),
            smem_tile(lambda j: (jnp.minimum(j + 1, nt - 1), 0, 0)),
            pl.BlockSpec(memory_space=pl.ANY),
        ],
        out_specs=pl.BlockSpec((tm, D_MODEL), lambda j: (j, 0)),
        out_shape=jax.ShapeDtypeStruct((t, D_MODEL), F32),
        scratch_shapes=[
            pltpu.VMEM((2, tm * SLOT_ROWS, LANES), F32),
            pltpu.SemaphoreType.DMA((2,)),
        ],
        compiler_params=pltpu.CompilerParams(
            dimension_semantics=("arbitrary",), vmem_limit_bytes=VMEM_LIMIT),
        name="moe_combine",
    )(pos3, pos3, ys)


def _moe(h2, plan, counts, p, *, tm):
    t = h2.shape[0]
    assert t % tm == 0 and t // tm >= 2
    max_items = -(-t // MOE_ROWS) + N_CLASSES
    class_ids = jnp.arange(LANES, dtype=jnp.int32)
    c_end = jnp.cumsum(counts)
    c_start = c_end - counts
    cls = plan[:, PLAN_CLASS, :].reshape(t).astype(jnp.int32)
    rank = plan[:, PLAN_RANK, :].reshape(t).astype(jnp.int32)
    pos = jnp.sum(jnp.where(cls[:, None] == class_ids[None, :], c_start[None, :], 0),
                  axis=1) + rank
    n_items_c = (counts + MOE_ROWS - 1) // MOE_ROWS
    item_end = jnp.cumsum(n_items_c)
    item_start = item_end - n_items_c
    n_items = item_end[-1]
    item_ids = jnp.minimum(jnp.arange(max_items, dtype=jnp.int32), n_items - 1)
    icls = jnp.sum(item_end[None, :] <= item_ids[:, None], axis=1, dtype=jnp.int32)
    pick = lambda table: jnp.sum(
        jnp.where(icls[:, None] == class_ids[None, :], table[None, :], 0), axis=1)
    offset = (item_ids - pick(item_start)) * MOE_ROWS
    start = pick(c_start) + offset
    nrows = jnp.minimum(pick(counts) - offset, MOE_ROWS)
    ea_tab, eb_tab = _class_experts()
    ea, eb = pick(ea_tab), pick(eb_tab)

    xs = _dispatch(h2, pos, tm=tm)

    first = lambda ea, eb: ea
    second = lambda ea, eb: eb
    w13 = lambda sel: pl.BlockSpec((None, D_MODEL, EXPERT_HIDDEN),
                                   lambda k, st, nr, ea, eb, n: (sel(ea, eb)[k], 0, 0))
    w2 = lambda sel: pl.BlockSpec((None, EXPERT_HIDDEN, D_MODEL),
                                  lambda k, st, nr, ea, eb, n: (sel(ea, eb)[k], 0, 0))
    const = lambda shape: pl.BlockSpec(shape, lambda k, *_: (0, 0))
    ys = pl.pallas_call(
        _moe_kernel,
        grid_spec=pltpu.PrefetchScalarGridSpec(
            num_scalar_prefetch=5,
            grid=(n_items,),
            in_specs=[
                const((1, D_MODEL)), const((D_MODEL, LANES)), const((1, LANES)),
                w13(first), w13(first), w2(first), w13(second), w13(second), w2(second),
                pl.BlockSpec(memory_space=pl.ANY),
            ],
            out_specs=pl.BlockSpec(memory_space=pl.ANY),
            scratch_shapes=[
                pltpu.VMEM((2, MOE_ROWS * SLOT_ROWS, LANES), F32),
                pltpu.VMEM((2, MOE_ROWS * SLOT_ROWS, LANES), F32),
                pltpu.SemaphoreType.DMA((2,)),
                pltpu.SemaphoreType.DMA((2,)),
            ],
        ),
        out_shape=jax.ShapeDtypeStruct(((t + MOE_ROWS) * SLOT_ROWS, LANES), F32),
        compiler_params=pltpu.CompilerParams(
            dimension_semantics=("arbitrary",), vmem_limit_bytes=VMEM_LIMIT),
        name="moe",
    )(start, nrows, ea, eb, n_items.reshape(1), p["g_ffn"], p["wr"], p["br"],
      p["w1"], p["w3"], p["w2"], p["w1"], p["w3"], p["w2"], xs)
    return _combine(ys, pos, tm=tm)


def _bias_selectors():
    selq = [[0.0] * LANES for _ in range(LANES)]
    selk = [[0.0] * LANES for _ in range(LANES)]
    ones_row = 3 * N_HEADS
    for h in range(N_HEADS):
        for part in range(3):
            selq[h * AUG_ROWS + part][part * N_HEADS + h] = 1.0
            selq[h * AUG_ROWS + 3 + part][ones_row] = 1.0
            selk[h * AUG_ROWS + part][ones_row] = 1.0
            selk[h * AUG_ROWS + 3 + part][part * N_HEADS + h] = -1.0
    return jnp.array(selq, BF16), jnp.array(selk, BF16)


def _prepare_params(norm_mix_g, w_in, b_forget, q_norm_g, k_norm_g, w_up_attn, w_pool,
                    pool_scale, w_up_pool, w_out, norm_ffn_g, w_group, b_group, w_router,
                    b_router, w1, w3, w2):
    aw, pw = ATTN_WIDTH, POOL_WIDTH
    f_off = 3 * aw
    p_off = f_off + N_HEADS
    g_off = p_off + pw
    selq, selk = _bias_selectors()
    w_in = w_in.astype(BF16)
    pad_lanes = lambda a: jnp.pad(a, ((0, 0), (0, LANES - a.shape[1])))
    wpool = jnp.zeros((pw, pw), F32)
    for g in range(len(POOL_WINDOWS)):
        sl = slice(g * POOL_GROUP_DIM, (g + 1) * POOL_GROUP_DIM)
        wpool = wpool.at[sl, sl].set(w_pool[g])
    d, f = D_MODEL, EXPERT_HIDDEN
    return {
        "g_mix": norm_mix_g.reshape(1, d),
        "wqkv": w_in[:, 0:f_off],
        "wf": pad_lanes(w_in[:, f_off:p_off]),
        "bf": pad_lanes(b_forget.reshape(1, N_HEADS)),
        "wp": w_in[:, p_off:g_off],
        "wg": w_in[:, g_off:],
        "gq": q_norm_g, "gk": k_norm_g,
        "selq": selq, "selk": selk,
        "wpool": wpool.astype(BF16),
        "pscale": pool_scale.reshape(1, pw),
        "wua": w_up_attn.astype(BF16),
        "wup": w_up_pool.astype(BF16),
        "wout": w_out.astype(BF16),
        "g_ffn": norm_ffn_g.reshape(1, d),
        "wr": pad_lanes(jnp.concatenate([w_group, w_router], axis=1)).astype(BF16),
        "br": pad_lanes(jnp.concatenate([b_group, b_router]).reshape(1, -1)),
        "w1": w1.reshape(N_EXPERTS, d, f).astype(BF16),
        "w3": w3.reshape(N_EXPERTS, d, f).astype(BF16),
        "w2": w2.reshape(N_EXPERTS, f, d).astype(BF16),
    }


def kernel(x, meta_tokens, norm_mix_g, w_in, b_forget, q_norm_g, k_norm_g, w_up_attn, w_pool,
           pool_scale, w_up_pool, w_out, norm_ffn_g, w_group, b_group, w_router, b_router,
           w1, w3, w2):
    nb, seq, d = x.shape
    p = _prepare_params(norm_mix_g[0], w_in[0], b_forget[0], q_norm_g[0], k_norm_g[0],
                        w_up_attn[0], w_pool[0], pool_scale[0], w_up_pool[0], w_out[0],
                        norm_ffn_g[0], w_group[0], b_group[0], w_router[0], b_router[0],
                        w1[0], w3[0], w2[0])

    meta = jnp.pad(meta_tokens.astype(x.dtype), ((0, LANES - N_META), (0, 0)))[None]
    _, ka_m, vt_m, _, _, u_meta, f_meta, k_sq_meta, _ = _inproj(
        meta, p, jnp.zeros((MAX_WINDOW, POOL_WIDTH), F32), jnp.zeros((N_HEADS, LANES), F32),
        tm=LANES, n_valid=N_META, first_pos=0, chains=1)

    tm = INPROJ_TILE
    qta, ka, vt, pooled, gates, _, _, k_sq, self_logit = _inproj(
        x, p, u_meta, f_meta, tm=tm, n_valid=tm, first_pos=N_META, chains=INPROJ_CHAINS)
    k_sq_max = jnp.maximum(jnp.max(k_sq.reshape(nb, seq // tm, N_HEADS, LANES), axis=1),
                           k_sq_meta)
    ot = _attention(qta, ka, vt, ka_m, vt_m, k_sq_max, self_logit, tq=ATTN_TILE)
    h2, plan, cnt = _post(ot, pooled, gates, x, p, tm=POST_TILE)
    out = _moe(h2.reshape(nb * seq, d), plan, cnt[:, 0].astype(jnp.int32), p, tm=ROW_MOVE_TILE)
    return out.reshape(nb, seq, d)
```

```python
import functools
import math

import jax
import jax.numpy as jnp
import numpy as np
from jax import lax
from jax.experimental import pallas as pl
from jax.experimental.pallas import tpu as pltpu

F32 = jnp.float32
BF16 = jnp.bfloat16

D_MODEL = 1024
N_META = 16
N_HEADS = 8
HEAD_DIM = 64
ATTN_WIDTH = N_HEADS * HEAD_DIM
POOL_WINDOWS = (2, 4, 8, 16)
POOL_WIDTH = 512
POOL_GROUP_DIM = POOL_WIDTH // len(POOL_WINDOWS)
MAX_WINDOW = max(POOL_WINDOWS)
N_GROUPS = 4
N_PER_GROUP = 8
N_EXPERTS = N_GROUPS * N_PER_GROUP
EXPERT_HIDDEN = 256
RMS_EPS = 1e-6
LOG2_E = math.log2(math.e)

LANES = 128
AUG_ROWS = 16
HEAD_ROWS = HEAD_DIM + AUG_ROWS
QK_ROWS = N_HEADS * HEAD_ROWS
ROUTER_LANE0 = N_GROUPS
PAIRS_PER_GROUP = N_PER_GROUP * (N_PER_GROUP - 1) // 2
N_CLASSES = N_GROUPS * PAIRS_PER_GROUP
ROUTER_ROWS = 40
PLAN_CLASS, PLAN_RANK = 0, 1
MOE_ROWS = 352
SLOT_ROWS = D_MODEL // LANES
DMA_UNROLL = 8
DMA_QUEUES = 2
MOE_CHAINS = 2
INPROJ_CHAINS = 1
POST_CHAINS = 1
INPROJ_TILE = 1024
ATTN_TILE = 512
POST_TILE = 1024
ROW_MOVE_TILE = 512
SAFE_EXP2 = 64.0
NORM_SLACK = 1.02
VMEM_LIMIT = 56 * 1024 * 1024

_TN = (((0,), (0,)), ((), ()))


def _const_spec(shape):
    zeros = (0,) * len(shape)
    return pl.BlockSpec(shape, lambda *_: zeros, pipeline_mode=pl.Buffered(1))


def _split3(x):
    hi = x.astype(BF16).astype(F32)
    r = x - hi
    mid = r.astype(BF16).astype(F32)
    lo = (r - mid).astype(BF16).astype(F32)
    return hi, mid, lo


def _inproj_kernel(x_ref, g_ref, wqkv_ref, wf_ref, bf_ref, wp_ref, wg_ref, gq_ref, gk_ref,
                   selq_ref, selk_ref, wpool_ref, pscale_ref, uprev_ref, fprev_ref,
                   qta_ref, ka_ref, vt_ref, pooled_ref, gates_ref, utail_ref, ftail_ref, kmax_ref,
                   self_ref,
                   ubuf, fcarry, *, tm, n_valid, first_pos, chains):
    j = pl.program_id(1)

    @pl.when(j == 0)
    def _():
        ubuf[0:MAX_WINDOW, :] = uprev_ref[...]
        fcarry[...] = fprev_ref[...]

    sub = tm // chains
    r_i = lax.broadcasted_iota(jnp.int32, (sub, sub), 0)
    c_i = lax.broadcasted_iota(jnp.int32, (sub, sub), 1)
    tri = jnp.where(r_i <= c_i, 1.0, 0.0).astype(BF16)
    k_pad = jnp.zeros((LANES - HEAD_ROWS, sub), F32)
    carry = fcarry[...]
    k_sq_max = [None] * N_HEADS

    def head_norm(src, gain_ref, h):
        xh = src[h * HEAD_DIM:(h + 1) * HEAD_DIM, :]
        ssq = jnp.mean(xh * xh, axis=0, keepdims=True)
        return xh * lax.rsqrt(ssq + RMS_EPS) * gain_ref[:, 0:sub]

    for chain in range(chains):
        t0 = chain * sub
        toks = slice(t0, t0 + sub)
        x = x_ref[toks, :]
        ms = jnp.mean(x * x, axis=-1, keepdims=True)
        hn = (x * lax.rsqrt(ms + RMS_EPS) * g_ref[...]).astype(BF16)

        f = jnp.dot(hn, wf_ref[...], preferred_element_type=F32) + bf_ref[...]
        z = f.T[0:N_HEADS, :]
        logf = (jnp.minimum(z, 0.0) - jnp.log1p(jnp.exp(-jnp.abs(z)))) * LOG2_E
        parts = jnp.concatenate(_split3(logf), axis=0).astype(BF16)
        cs3 = jnp.dot(parts, tri, preferred_element_type=F32)
        cs = cs3[0:8] + cs3[8:16] + cs3[16:24]
        fc = jnp.concatenate([carry] * (sub // LANES), axis=1) + cs
        if n_valid - 1 >= t0:
            last_col = min(n_valid, t0 + sub) - 1 - t0
            carry = jnp.broadcast_to(fc[:, last_col:last_col + 1], (N_HEADS, LANES))

        pieces = jnp.concatenate(
            _split3(fc) + (jnp.ones((8, sub), F32), jnp.zeros((LANES - 32, sub), F32)),
            axis=0).astype(BF16)
        fq = jnp.dot(selq_ref[...], pieces, preferred_element_type=F32)
        fk = jnp.dot(selk_ref[...], pieces, preferred_element_type=F32)

        qkv = jnp.dot(hn, wqkv_ref[...], preferred_element_type=F32)
        qt = qkv[:, 0:ATTN_WIDTH].T
        kt = qkv[:, ATTN_WIDTH:2 * ATTN_WIDTH].T
        vt_ref[:, toks] = qkv[:, 2 * ATTN_WIDTH:3 * ATTN_WIDTH].T.astype(BF16)

        for h in range(N_HEADS):
            aug = slice(h * AUG_ROWS, (h + 1) * AUG_ROWS)
            r0 = h * HEAD_ROWS
            qn = head_norm(qt, gq_ref, h)
            qta_ref[r0:r0 + HEAD_DIM, toks] = qn.astype(BF16)
            qta_ref[r0 + HEAD_DIM:r0 + HEAD_ROWS, toks] = fq[aug, :].astype(BF16)
            kn = head_norm(kt, gk_ref, h)
            self_ref[h:h + 1, toks] = jnp.sum(qn * kn, axis=0, keepdims=True)
            k_blk = jnp.concatenate([kn, fk[aug, :], k_pad], axis=0)
            ka_ref[toks, h * LANES:(h + 1) * LANES] = k_blk.T.astype(BF16)
            k_sq = jnp.max(jnp.sum(kn * kn, axis=0, keepdims=True), axis=1, keepdims=True)
            k_sq_max[h] = k_sq if k_sq_max[h] is None else jnp.maximum(k_sq_max[h], k_sq)

        u = jnp.dot(hn, wp_ref[...], preferred_element_type=F32)
        u0 = MAX_WINDOW + t0
        ubuf[u0:u0 + sub, :] = u
        mixed = []
        for g, w in enumerate(POOL_WINDOWS):
            c0 = g * POOL_GROUP_DIM
            acc = u[:, c0:c0 + POOL_GROUP_DIM]
            for s in range(1, w):
                acc = acc + ubuf[u0 - s:u0 - s + sub, c0:c0 + POOL_GROUP_DIM]
            if first_pos + 1 >= w:
                mean = acc * (1.0 / w)
            else:
                pos = (first_pos + j * tm + t0
                       + lax.broadcasted_iota(jnp.int32, (sub, POOL_GROUP_DIM), 0))
                mean = acc / jnp.minimum(pos + 1, w).astype(F32)
            mixed.append(mean - u[:, c0:c0 + POOL_GROUP_DIM])
        mixed = jnp.concatenate(mixed, axis=1).astype(BF16)
        y = jnp.dot(mixed, wpool_ref[...], preferred_element_type=F32) * pscale_ref[...]
        pooled_ref[toks, :] = y.astype(BF16)

        gl = jnp.dot(hn, wg_ref[...], preferred_element_type=F32)
        gates_ref[toks, :] = jax.nn.sigmoid(gl).astype(BF16)

    fcarry[...] = carry
    ftail_ref[...] = carry
    for h in range(N_HEADS):
        kmax_ref[h:h + 1, :] = jnp.broadcast_to(k_sq_max[h], (1, LANES))
    tail = ubuf[n_valid:n_valid + MAX_WINDOW, :]
    ubuf[0:MAX_WINDOW, :] = tail
    utail_ref[...] = tail


def _inproj(x3, p, uprev, fprev, *, tm, n_valid, first_pos, chains):
    nb, seq, _ = x3.shape
    nt = seq // tm
    gq = jnp.broadcast_to((p["gq"] * (LOG2_E / math.sqrt(HEAD_DIM)))[:, None], (HEAD_DIM, tm))
    gk = jnp.broadcast_to(p["gk"][:, None], (HEAD_DIM, tm))
    tok = lambda width: pl.BlockSpec((None, tm, width), lambda b, j: (b, j, 0))
    chan = lambda rows: pl.BlockSpec((None, rows, tm), lambda b, j: (b, 0, j))
    kern = functools.partial(_inproj_kernel, tm=tm, n_valid=n_valid, first_pos=first_pos,
                             chains=chains)
    return pl.pallas_call(
        kern,
        grid=(nb, nt),
        in_specs=[
            tok(D_MODEL),
            _const_spec((1, D_MODEL)),
            _const_spec((D_MODEL, 3 * ATTN_WIDTH)),
            _const_spec((D_MODEL, LANES)),
            _const_spec((1, LANES)),
            _const_spec((D_MODEL, POOL_WIDTH)),
            _const_spec((D_MODEL, 2 * D_MODEL)),
            _const_spec((HEAD_DIM, tm)),
            _const_spec((HEAD_DIM, tm)),
            _const_spec((LANES, LANES)),
            _const_spec((LANES, LANES)),
            _const_spec((POOL_WIDTH, POOL_WIDTH)),
            _const_spec((1, POOL_WIDTH)),
            _const_spec((MAX_WINDOW, POOL_WIDTH)),
            _const_spec((N_HEADS, LANES)),
        ],
        out_specs=[
            chan(QK_ROWS), tok(N_HEADS * LANES), chan(ATTN_WIDTH),
            tok(POOL_WIDTH), tok(2 * D_MODEL),
            pl.BlockSpec((MAX_WINDOW, POOL_WIDTH), lambda b, j: (0, 0)),
            pl.BlockSpec((N_HEADS, LANES), lambda b, j: (0, 0)),
            pl.BlockSpec((None, N_HEADS, LANES), lambda b, j: (b * nt + j, 0, 0)),
            chan(N_HEADS),
        ],
        out_shape=[
            jax.ShapeDtypeStruct((nb, QK_ROWS, seq), BF16),
            jax.ShapeDtypeStruct((nb, seq, N_HEADS * LANES), BF16),
            jax.ShapeDtypeStruct((nb, ATTN_WIDTH, seq), BF16),
            jax.ShapeDtypeStruct((nb, seq, POOL_WIDTH), BF16),
            jax.ShapeDtypeStruct((nb, seq, 2 * D_MODEL), BF16),
            jax.ShapeDtypeStruct((MAX_WINDOW, POOL_WIDTH), F32),
            jax.ShapeDtypeStruct((N_HEADS, LANES), F32),
            jax.ShapeDtypeStruct((nb * nt, N_HEADS, LANES), F32),
            jax.ShapeDtypeStruct((nb, N_HEADS, seq), F32),
        ],
        scratch_shapes=[
            pltpu.VMEM((MAX_WINDOW + tm, POOL_WIDTH), F32),
            pltpu.VMEM((N_HEADS, LANES), F32),
        ],
        compiler_params=pltpu.CompilerParams(
            dimension_semantics=("arbitrary", "arbitrary"), vmem_limit_bytes=VMEM_LIMIT),
        name="inproj",
    )(x3, p["g_mix"], p["wqkv"], p["wf"], p["bf"], p["wp"], p["wg"], gq, gk,
      p["selq"], p["selk"], p["wpool"], p["pscale"], uprev, fprev)


def _attn_kernel(qta_ref, ka_ref, vt_ref, kam_ref, vtm_ref, bias_ref, kmax_ref, self_ref, ot_ref,
                 m_sc, acc_sc, *, tq):
    qi = pl.program_id(1)
    m_sc[...] = jnp.full(m_sc.shape, -jnp.inf, F32)
    acc_sc[...] = jnp.zeros(acc_sc.shape, F32)

    def rows(h):
        return slice(h * HEAD_ROWS, (h + 1) * HEAD_ROWS)

    def vrows(h):
        return slice(h * HEAD_DIM, (h + 1) * HEAD_DIM)

    def klanes(h):
        return slice(h * LANES, (h + 1) * LANES)

    q_pad = jnp.zeros((LANES - HEAD_ROWS, tq), BF16)

    def scores(h, ka):
        qa = jnp.concatenate([qta_ref[rows(h), :], q_pad], axis=0)
        return jnp.dot(ka, qa, preferred_element_type=F32)

    def new_max(h, s):
        return jnp.maximum(m_sc[h], jnp.max(s, axis=0, keepdims=True))

    def absorb(h, s, m_new, v):
        p = jnp.exp2(s - m_new).astype(BF16)
        va = jnp.concatenate([v, jnp.ones((AUG_ROWS, v.shape[1]), BF16)], axis=0)
        acc_sc[h] = (jnp.exp2(m_sc[h] - m_new) * acc_sc[h]
                     + jnp.dot(va, p, preferred_element_type=F32))
        m_sc[h] = m_new

    def sweep_heads(scores_of, values_of):
        s = {0: scores_of(0), 1: scores_of(1)}
        m = {0: new_max(0, s[0])}
        for h in range(N_HEADS):
            if h + 2 < N_HEADS:
                s[h + 2] = scores_of(h + 2)
            if h + 1 < N_HEADS:
                m[h + 1] = new_max(h + 1, s[h + 1])
            absorb(h, s.pop(h), m.pop(h), values_of(h))

    def sweep_heads_bounded(*tiles):
        items = [(h, s_of, v_of) for s_of, v_of in tiles for h in range(N_HEADS)]
        s_next = items[0][1](items[0][0])
        for n, (h, _, values_of) in enumerate(items):
            s = s_next
            if n + 1 < len(items):
                s_next = items[n + 1][1](items[n + 1][0])
            p = jnp.exp2(s - m_sc[h]).astype(BF16)
            v = values_of(h)
            va = jnp.concatenate([v, jnp.ones((AUG_ROWS, v.shape[1]), BF16)], axis=0)
            acc_sc[h] += jnp.dot(va, p, preferred_element_type=F32)

    d0 = pl.multiple_of(qi * tq, tq)
    diag_scores = lambda h: scores(h, jnp.concatenate(
        [ka_ref[pl.ds(d0, tq), klanes(h)], kam_ref[:, klanes(h)]], axis=0)) + bias_ref[...]
    diag_values = lambda h: jnp.concatenate(
        [vt_ref[vrows(h), pl.ds(d0, tq)], vtm_ref[vrows(h), :]], axis=1)

    def tile_scores(i):
        s0 = pl.multiple_of(i * tq, tq)
        return lambda h: scores(h, ka_ref[pl.ds(s0, tq), klanes(h)])

    def tile_values(i):
        s0 = pl.multiple_of(i * tq, tq)
        return lambda h: vt_ref[vrows(h), pl.ds(s0, tq)]

    worst = None
    for h in range(N_HEADS):
        q = qta_ref[h * HEAD_ROWS:h * HEAD_ROWS + HEAD_DIM, :].astype(F32)
        q_norm = jnp.sqrt(jnp.sum(q * q, axis=0, keepdims=True))
        k_norm = jnp.sqrt(jnp.concatenate([kmax_ref[h:h + 1, :]] * (tq // LANES), axis=1))
        gap = q_norm * k_norm * NORM_SLACK - SAFE_EXP2 - self_ref[h:h + 1, :]
        worst = gap if worst is None else jnp.maximum(worst, gap)
    bounded = jnp.max(worst) <= 0.0

    @pl.when(bounded)
    def _():
        for h in range(N_HEADS):
            m_sc[h] = self_ref[h:h + 1, :]
        sweep_heads_bounded((diag_scores, diag_values))

        def body(i, carry):
            sweep_heads_bounded((tile_scores(2 * i), tile_values(2 * i)),
                                (tile_scores(2 * i + 1), tile_values(2 * i + 1)))
            return carry
        lax.fori_loop(0, qi // 2, body, 0)

        @pl.when(qi % 2 == 1)
        def _():
            sweep_heads_bounded((tile_scores(qi - 1), tile_values(qi - 1)))

    @pl.when(jnp.logical_not(bounded))
    def _():
        sweep_heads(diag_scores, diag_values)

        def body(i, carry):
            sweep_heads(tile_scores(i), tile_values(i))
            return carry
        lax.fori_loop(0, qi, body, 0)

    for h in range(N_HEADS):
        acc = acc_sc[h]
        ot_ref[vrows(h), :] = (acc[0:HEAD_DIM, :] / acc[HEAD_DIM:HEAD_DIM + 1, :]).astype(BF16)


def _attention(qta, ka, vt, ka_m, vt_m, k_sq_max, self_logit, *, tq):
    nb, _, seq = qta.shape
    kw = N_HEADS * LANES
    key = np.arange(tq + LANES)[:, None]
    qry = np.arange(tq)[None, :]
    visible = np.where(key < tq, key <= qry, key - tq < N_META)
    bias = jnp.asarray(np.where(visible, 0.0, -np.inf), F32)
    return pl.pallas_call(
        functools.partial(_attn_kernel, tq=tq),
        grid=(nb, seq // tq),
        in_specs=[
            pl.BlockSpec((None, QK_ROWS, tq), lambda b, i: (b, 0, i)),
            pl.BlockSpec((None, seq, kw), lambda b, i: (b, 0, 0)),
            pl.BlockSpec((None, ATTN_WIDTH, seq), lambda b, i: (b, 0, 0)),
            pl.BlockSpec((None, LANES, kw), lambda b, i: (0, 0, 0)),
            pl.BlockSpec((None, ATTN_WIDTH, LANES), lambda b, i: (0, 0, 0)),
            _const_spec((tq + LANES, tq)),
            pl.BlockSpec((None, N_HEADS, LANES), lambda b, i: (b, 0, 0)),
            pl.BlockSpec((None, N_HEADS, tq), lambda b, i: (b, 0, i)),
        ],
        out_specs=pl.BlockSpec((None, ATTN_WIDTH, tq), lambda b, i: (b, 0, i)),
        out_shape=jax.ShapeDtypeStruct((nb, ATTN_WIDTH, seq), BF16),
        scratch_shapes=[
            pltpu.VMEM((N_HEADS, 1, tq), F32),
            pltpu.VMEM((N_HEADS, HEAD_ROWS, tq), F32),
        ],
        compiler_params=pltpu.CompilerParams(
            dimension_semantics=("arbitrary", "arbitrary"), vmem_limit_bytes=VMEM_LIMIT),
        name="fox_attention",
    )(qta, ka, vt, ka_m, vt_m, bias, k_sq_max, self_logit)


def _post_kernel(ot_ref, pooled_ref, gates_ref, x_ref, wua_ref, wup_ref, wout_ref, g_ref,
                 wr_ref, br_ref, h2_ref, plan_ref, cnt_ref, cnt_sc, *, tm):
    @pl.when((pl.program_id(0) == 0) & (pl.program_id(1) == 0))
    def _():
        cnt_sc[...] = jnp.zeros(cnt_sc.shape, F32)

    sub = tm // POST_CHAINS
    row = lax.broadcasted_iota(jnp.int32, (ROUTER_ROWS, sub), 0)
    crow = lax.broadcasted_iota(jnp.int32, (LANES, sub), 0)
    prow = lax.broadcasted_iota(jnp.int32, (8, sub), 0)
    r_i = lax.broadcasted_iota(jnp.int32, (sub, sub), 0)
    c_i = lax.broadcasted_iota(jnp.int32, (sub, sub), 1)
    earlier = jnp.where(r_i < c_i, 1.0, 0.0).astype(BF16)
    neg_inf = jnp.float32(-jnp.inf)
    counts = cnt_sc[...]

    for chain in range(POST_CHAINS):
        toks = slice(chain * sub, (chain + 1) * sub)
        y_attn = lax.dot_general(ot_ref[:, toks], wua_ref[...], _TN, preferred_element_type=F32)
        y_pool = jnp.dot(pooled_ref[toks, :], wup_ref[...], preferred_element_type=F32)
        merged = (gates_ref[toks, 0:D_MODEL].astype(F32) * y_attn
                  + gates_ref[toks, D_MODEL:2 * D_MODEL].astype(F32) * y_pool)
        h2 = x_ref[toks, :] + jnp.dot(merged.astype(BF16), wout_ref[...],
                                      preferred_element_type=F32)
        _to_slots(h2_ref.at[pl.ds(chain * sub * SLOT_ROWS, sub * SLOT_ROWS)], h2)
        ms = jnp.mean(h2 * h2, axis=-1, keepdims=True)
        hn2 = (h2 * lax.rsqrt(ms + RMS_EPS) * g_ref[...]).astype(BF16)

        logits = jnp.dot(hn2, wr_ref[...], preferred_element_type=F32) + br_ref[...]
        lt = logits.T[0:ROUTER_ROWS, :]

        def softmax_over(mask):
            z = jnp.where(mask, lt, neg_inf)
            e = jnp.exp(z - jnp.max(z, axis=0, keepdims=True))
            return jnp.where(mask, e / jnp.sum(e, axis=0, keepdims=True), -1.0)

        def argtop(prob):
            top = jnp.max(prob, axis=0, keepdims=True)
            return jnp.min(jnp.where(prob == top, row, LANES), axis=0, keepdims=True)

        g_idx = argtop(softmax_over(row < N_GROUPS))
        e_lo = ROUTER_LANE0 + N_PER_GROUP * g_idx
        e_prob = softmax_over((row >= e_lo) & (row < e_lo + N_PER_GROUP))
        i1 = argtop(e_prob)
        i2 = argtop(jnp.where(row == i1, -1.0, e_prob))
        a = jnp.minimum(i1, i2) - e_lo
        b = jnp.maximum(i1, i2) - e_lo
        pair = ((a * (2 * N_PER_GROUP - 1 - a)) >> 1) + (b - a - 1)
        cls = g_idx * PAIRS_PER_GROUP + pair

        member = crow == cls
        onehot = jnp.where(member, 1.0, 0.0)
        prefix = jnp.dot(onehot.astype(BF16), earlier, preferred_element_type=F32)
        base = jnp.concatenate([counts] * (sub // LANES), axis=1)
        rank = jnp.sum(jnp.where(member, prefix + base, 0.0), axis=0, keepdims=True)
        counts = counts + jnp.broadcast_to(jnp.sum(onehot, axis=1, keepdims=True), counts.shape)
        plan_ref[:, toks] = jnp.where(prow == PLAN_CLASS, cls.astype(F32),
                                      jnp.where(prow == PLAN_RANK, rank, 0.0))

    cnt_sc[...] = counts
    cnt_ref[...] = counts


def _post(ot, pooled, gates, x, p, *, tm):
    nb, seq, _ = x.shape
    nt = seq // tm
    tok = lambda width: pl.BlockSpec((None, tm, width), lambda b, j: (b, j, 0))
    return pl.pallas_call(
        functools.partial(_post_kernel, tm=tm),
        grid=(nb, nt),
        in_specs=[
            pl.BlockSpec((None, ATTN_WIDTH, tm), lambda b, j: (b, 0, j)),
            tok(POOL_WIDTH), tok(2 * D_MODEL), tok(D_MODEL),
            _const_spec((ATTN_WIDTH, D_MODEL)),
            _const_spec((POOL_WIDTH, D_MODEL)),
            _const_spec((D_MODEL, D_MODEL)),
            _const_spec((1, D_MODEL)),
            _const_spec((D_MODEL, LANES)),
            _const_spec((1, LANES)),
        ],
        out_specs=[pl.BlockSpec((None, tm * SLOT_ROWS, LANES), lambda b, j: (b, j, 0)),
                   pl.BlockSpec((None, 8, tm), lambda b, j: (b * nt + j, 0, 0)),
                   pl.BlockSpec((LANES, LANES), lambda b, j: (0, 0))],
        out_shape=[
            jax.ShapeDtypeStruct((nb, seq * SLOT_ROWS, LANES), F32),
            jax.ShapeDtypeStruct((nb * nt, 8, tm), F32),
            jax.ShapeDtypeStruct((LANES, LANES), F32),
        ],
        scratch_shapes=[pltpu.VMEM((LANES, LANES), F32)],
        compiler_params=pltpu.CompilerParams(
            dimension_semantics=("arbitrary", "arbitrary"), vmem_limit_bytes=VMEM_LIMIT),
        name="post_router",
    )(ot, pooled, gates, x, p["wua"], p["wup"], p["wout"], p["g_ffn"], p["wr"], p["br"])


def _to_slots(dst, values):
    n = values.shape[0]
    for c in range(SLOT_ROWS):
        dst[pl.ds(c, n, stride=SLOT_ROWS), :] = values[:, c * LANES:(c + 1) * LANES]


def _from_slots(src, n):
    return jnp.concatenate(
        [src[pl.ds(c, n, stride=SLOT_ROWS), :] for c in range(SLOT_ROWS)], axis=1)


def _dispatch_kernel(pos_ref, h2_hbm, xs_hbm, zeros, sem, zsem, *, tm, n_slots):
    j = pl.program_id(0)
    slot = j % 2

    def drain(s):
        pltpu.make_async_copy(h2_hbm.at[pl.ds(0, tm * SLOT_ROWS), :],
                              xs_hbm.at[pl.ds(0, tm * SLOT_ROWS), :], sem.at[s]).wait()

    @pl.when(j >= 2)
    def _():
        drain(slot)

    def body(g, c):
        for u in range(DMA_UNROLL):
            r = g * DMA_UNROLL + u
            src = pl.multiple_of((j * tm + r) * SLOT_ROWS, SLOT_ROWS)
            dst = pl.multiple_of(pos_ref[0, r] * SLOT_ROWS, SLOT_ROWS)
            pltpu.make_async_copy(h2_hbm.at[pl.ds(src, SLOT_ROWS), :],
                                  xs_hbm.at[pl.ds(dst, SLOT_ROWS), :],
                                  sem.at[slot]).start(priority=u % DMA_QUEUES)
        return c
    lax.fori_loop(0, tm // DMA_UNROLL, body, 0)

    @pl.when(j == pl.num_programs(0) - 1)
    def _():
        zeros[...] = jnp.zeros(zeros.shape, F32)
        pad = pltpu.make_async_copy(
            zeros, xs_hbm.at[pl.ds(n_slots * SLOT_ROWS, MOE_ROWS * SLOT_ROWS), :], zsem.at[0])
        pad.start()
        drain(slot)
        drain(1 - slot)
        pad.wait()


def _dispatch(h2, pos, *, tm):
    t = pos.shape[0]
    nt = t // tm
    return pl.pallas_call(
        functools.partial(_dispatch_kernel, tm=tm, n_slots=t),
        grid=(nt,),
        in_specs=[
            pl.BlockSpec((None, 1, tm), lambda j: (j, 0, 0), memory_space=pltpu.SMEM),
            pl.BlockSpec(memory_space=pl.ANY),
        ],
        out_specs=pl.BlockSpec(memory_space=pl.ANY),
        out_shape=jax.ShapeDtypeStruct(((t + MOE_ROWS) * SLOT_ROWS, LANES), F32),
        scratch_shapes=[
            pltpu.VMEM((MOE_ROWS * SLOT_ROWS, LANES), F32),
            pltpu.SemaphoreType.DMA((2,)),
            pltpu.SemaphoreType.DMA((1,)),
        ],
        compiler_params=pltpu.CompilerParams(
            dimension_semantics=("arbitrary",), vmem_limit_bytes=VMEM_LIMIT),
        name="moe_dispatch",
    )(pos.reshape(nt, 1, tm), h2)


def _moe_kernel(start_ref, nrows_ref, ea_ref, eb_ref, nitems_ref,
                g_ref, wr_ref, br_ref, w1a_ref, w3a_ref, w2a_ref, w1b_ref, w3b_ref, w2b_ref, xs_hbm,
                ys_hbm, xbuf, ybuf, isem, osem):
    k = pl.program_id(0)
    n_items = nitems_ref[0]
    slot = k % 2
    rows = MOE_ROWS * SLOT_ROWS

    def window(kk):
        return pl.ds(pl.multiple_of(start_ref[kk] * SLOT_ROWS, SLOT_ROWS), rows)

    def in_copy(kk, s):
        return pltpu.make_async_copy(xs_hbm.at[window(kk), :], xbuf.at[s], isem.at[s])

    def out_copy(kk, s):
        return pltpu.make_async_copy(ybuf.at[s], ys_hbm.at[window(kk), :], osem.at[s])

    @pl.when(k == 0)
    def _():
        in_copy(0, 0).start()
        ybuf[1] = jnp.zeros(ybuf.shape[1:], F32)
        pad = pltpu.make_async_copy(
            ybuf.at[1], ys_hbm.at[pl.ds(ys_hbm.shape[0] - rows, rows), :], osem.at[1])
        pad.start()
        pad.wait()

    @pl.when(k + 1 < n_items)
    def _():
        in_copy(k + 1, 1 - slot).start()

    @pl.when(k < n_items)
    def _():
        in_copy(k, slot).wait()
        ea, eb = ea_ref[k], eb_ref[k]
        sub = MOE_ROWS // MOE_CHAINS
        for chain in range(MOE_CHAINS):
            block = pl.ds(chain * sub * SLOT_ROWS, sub * SLOT_ROWS)
            h2 = _from_slots(xbuf.at[slot, block], sub)
            ms = jnp.mean(h2 * h2, axis=-1, keepdims=True)
            xn = (h2 * lax.rsqrt(ms + RMS_EPS) * g_ref[...]).astype(BF16)

            logits = jnp.dot(xn, wr_ref[...], preferred_element_type=F32) + br_ref[...]
            lane = lax.broadcasted_iota(jnp.int32, logits.shape, 1)
            pick = lambda idx: jnp.sum(jnp.where(lane == idx, logits, 0.0), axis=1,
                                       keepdims=True)
            gl = jnp.where(lane < N_GROUPS, logits, -jnp.inf)
            gmax = jnp.max(gl, axis=1, keepdims=True)
            p_group = (jnp.exp(pick(ea // N_PER_GROUP) - gmax)
                       / jnp.sum(jnp.exp(gl - gmax), axis=1, keepdims=True))
            la, lb = pick(ROUTER_LANE0 + ea), pick(ROUTER_LANE0 + eb)
            row = chain * sub + lax.broadcasted_iota(jnp.int32, (sub, 1), 0)
            valid = row < nrows_ref[k]
            w_lo = jnp.where(valid, p_group / (1.0 + jnp.exp(lb - la)), 0.0)
            w_hi = jnp.where(valid, p_group / (1.0 + jnp.exp(la - lb)), 0.0)

            def expert(w1_ref, w3_ref, w2_ref, weight):
                h1 = jnp.dot(xn, w1_ref[...], preferred_element_type=F32)
                h3 = jnp.dot(xn, w3_ref[...], preferred_element_type=F32)
                hh = h1 * jax.nn.sigmoid(h1) * h3 * weight
                return jnp.dot(hh.astype(BF16), w2_ref[...], preferred_element_type=F32)

            y = h2 + (expert(w1a_ref, w3a_ref, w2a_ref, w_lo)
                      + expert(w1b_ref, w3b_ref, w2b_ref, w_hi))
            _to_slots(ybuf.at[slot, block], y)

        @pl.when(k >= 1)
        def _():
            out_copy(k - 1, 1 - slot).wait()

        out_copy(k, slot).start()

        @pl.when(k == n_items - 1)
        def _():
            out_copy(k, slot).wait()


def _class_experts():
    ea, eb = [], []
    for g in range(N_GROUPS):
        for a in range(N_PER_GROUP):
            for b in range(a + 1, N_PER_GROUP):
                ea.append(g * N_PER_GROUP + a)
                eb.append(g * N_PER_GROUP + b)
    fill = LANES - len(ea)
    return (jnp.array(ea + [ea[-1]] * fill, jnp.int32),
            jnp.array(eb + [eb[-1]] * fill, jnp.int32))


def _combine_kernel(pos_ref, pos_next_ref, ys_hbm, out_ref, buf, sem, *, tm):
    j = pl.program_id(0)
    slot = j % 2

    def start_tile(pos, s):
        def body(g, c):
            for u in range(DMA_UNROLL):
                r = g * DMA_UNROLL + u
                src = pl.multiple_of(pos[0, r] * SLOT_ROWS, SLOT_ROWS)
                dst = pl.multiple_of(r * SLOT_ROWS, SLOT_ROWS)
                pltpu.make_async_copy(ys_hbm.at[pl.ds(src, SLOT_ROWS), :],
                                      buf.at[s, pl.ds(dst, SLOT_ROWS), :],
                                      sem.at[s]).start(priority=u % DMA_QUEUES)
            return c
        lax.fori_loop(0, tm // DMA_UNROLL, body, 0)

    @pl.when(j == 0)
    def _():
        start_tile(pos_ref, 0)

    @pl.when(j + 1 < pl.num_programs(0))
    def _():
        start_tile(pos_next_ref, 1 - slot)

    pltpu.make_async_copy(ys_hbm.at[pl.ds(0, tm * SLOT_ROWS), :], buf.at[slot], sem.at[slot]).wait()
    out_ref[...] = _from_slots(buf.at[slot], tm)


def _combine(ys, pos, *, tm):
    t = pos.shape[0]
    nt = t // tm
    pos3 = pos.reshape(nt, 1, tm)
    smem_tile = lambda fn: pl.BlockSpec((None, 1, tm), fn, memory_space=pltpu.SMEM)
    return pl.pallas_call(
        functools.partial(_combine_kernel, tm=tm),
        grid=(nt,),
        in_specs=[
            smem_tile(lambda j: (j, 0, 0)),
            smem_tile(lambda j: (jnp.minimum(j + 1, nt - 1), 0, 0)),
            pl.BlockSpec(memory_space=pl.ANY),
        ],
        out_specs=pl.BlockSpec((tm, D_MODEL), lambda j: (j, 0)),
        out_shape=jax.ShapeDtypeStruct((t, D_MODEL), F32),
        scratch_shapes=[
            pltpu.VMEM((2, tm * SLOT_ROWS, LANES), F32),
            pltpu.SemaphoreType.DMA((2,)),
        ],
        compiler_params=pltpu.CompilerParams(
            dimension_semantics=("arbitrary",), vmem_limit_bytes=VMEM_LIMIT),
        name="moe_combine",
    )(pos3, pos3, ys)


def _moe(h2, plan, counts, p, *, tm):
    t = plan.shape[0] * plan.shape[2]
    assert t % tm == 0 and t // tm >= 2
    max_items = -(-t // MOE_ROWS) + N_CLASSES
    class_ids = jnp.arange(LANES, dtype=jnp.int32)
    c_end = jnp.cumsum(counts)
    c_start = c_end - counts
    cls = plan[:, PLAN_CLASS, :].reshape(t).astype(jnp.int32)
    rank = plan[:, PLAN_RANK, :].reshape(t).astype(jnp.int32)
    pos = jnp.sum(jnp.where(cls[:, None] == class_ids[None, :], c_start[None, :], 0),
                  axis=1) + rank
    n_items_c = (counts + MOE_ROWS - 1) // MOE_ROWS
    item_end = jnp.cumsum(n_items_c)
    item_start = item_end - n_items_c
    n_items = item_end[-1]
    item_ids = jnp.minimum(jnp.arange(max_items, dtype=jnp.int32), n_items - 1)
    icls = jnp.sum(item_end[None, :] <= item_ids[:, None], axis=1, dtype=jnp.int32)
    pick = lambda table: jnp.sum(
        jnp.where(icls[:, None] == class_ids[None, :], table[None, :], 0), axis=1)
    offset = (item_ids - pick(item_start)) * MOE_ROWS
    start = pick(c_start) + offset
    nrows = jnp.minimum(pick(counts) - offset, MOE_ROWS)
    ea_tab, eb_tab = _class_experts()
    ea, eb = pick(ea_tab), pick(eb_tab)

    xs = _dispatch(h2, pos, tm=tm)

    first = lambda ea, eb: ea
    second = lambda ea, eb: eb
    w13 = lambda sel: pl.BlockSpec((None, D_MODEL, EXPERT_HIDDEN),
                                   lambda k, st, nr, ea, eb, n: (sel(ea, eb)[k], 0, 0))
    w2 = lambda sel: pl.BlockSpec((None, EXPERT_HIDDEN, D_MODEL),
                                  lambda k, st, nr, ea, eb, n: (sel(ea, eb)[k], 0, 0))
    const = lambda shape: pl.BlockSpec(shape, lambda k, *_: (0, 0))
    ys = pl.pallas_call(
        _moe_kernel,
        grid_spec=pltpu.PrefetchScalarGridSpec(
            num_scalar_prefetch=5,
            grid=(n_items,),
            in_specs=[
                const((1, D_MODEL)), const((D_MODEL, LANES)), const((1, LANES)),
                w13(first), w13(first), w2(first), w13(second), w13(second), w2(second),
                pl.BlockSpec(memory_space=pl.ANY),
            ],
            out_specs=pl.BlockSpec(memory_space=pl.ANY),
            scratch_shapes=[
                pltpu.VMEM((2, MOE_ROWS * SLOT_ROWS, LANES), F32),
                pltpu.VMEM((2, MOE_ROWS * SLOT_ROWS, LANES), F32),
                pltpu.SemaphoreType.DMA((2,)),
                pltpu.SemaphoreType.DMA((2,)),
            ],
        ),
        out_shape=jax.ShapeDtypeStruct(((t + MOE_ROWS) * SLOT_ROWS, LANES), F32),
        compiler_params=pltpu.CompilerParams(
            dimension_semantics=("arbitrary",), vmem_limit_bytes=VMEM_LIMIT),
        name="moe",
    )(start, nrows, ea, eb, n_items.reshape(1), p["g_ffn"], p["wr"], p["br"],
      p["w1"], p["w3"], p["w2"], p["w1"], p["w3"], p["w2"], xs)
    return _combine(ys, pos, tm=tm)


def _bias_selectors():
    selq = [[0.0] * LANES for _ in range(LANES)]
    selk = [[0.0] * LANES for _ in range(LANES)]
    ones_row = 3 * N_HEADS
    for h in range(N_HEADS):
        for part in range(3):
            selq[h * AUG_ROWS + part][part * N_HEADS + h] = 1.0
            selq[h * AUG_ROWS + 3 + part][ones_row] = 1.0
            selk[h * AUG_ROWS + part][ones_row] = 1.0
            selk[h * AUG_ROWS + 3 + part][part * N_HEADS + h] = -1.0
    return jnp.array(selq, BF16), jnp.array(selk, BF16)


def _prepare_params(norm_mix_g, w_in, b_forget, q_norm_g, k_norm_g, w_up_attn, w_pool,
                    pool_scale, w_up_pool, w_out, norm_ffn_g, w_group, b_group, w_router,
                    b_router, w1, w3, w2):
    aw, pw = ATTN_WIDTH, POOL_WIDTH
    f_off = 3 * aw
    p_off = f_off + N_HEADS
    g_off = p_off + pw
    selq, selk = _bias_selectors()
    w_in = w_in.astype(BF16)
    pad_lanes = lambda a: jnp.pad(a, ((0, 0), (0, LANES - a.shape[1])))
    wpool = jnp.zeros((pw, pw), F32)
    for g in range(len(POOL_WINDOWS)):
        sl = slice(g * POOL_GROUP_DIM, (g + 1) * POOL_GROUP_DIM)
        wpool = wpool.at[sl, sl].set(w_pool[g])
    d, f = D_MODEL, EXPERT_HIDDEN
    return {
        "g_mix": norm_mix_g.reshape(1, d),
        "wqkv": w_in[:, 0:f_off],
        "wf": pad_lanes(w_in[:, f_off:p_off]),
        "bf": pad_lanes(b_forget.reshape(1, N_HEADS)),
        "wp": w_in[:, p_off:g_off],
        "wg": w_in[:, g_off:],
        "gq": q_norm_g, "gk": k_norm_g,
        "selq": selq, "selk": selk,
        "wpool": wpool.astype(BF16),
        "pscale": pool_scale.reshape(1, pw),
        "wua": w_up_attn.astype(BF16),
        "wup": w_up_pool.astype(BF16),
        "wout": w_out.astype(BF16),
        "g_ffn": norm_ffn_g.reshape(1, d),
        "wr": pad_lanes(jnp.concatenate([w_group, w_router], axis=1)).astype(BF16),
        "br": pad_lanes(jnp.concatenate([b_group, b_router]).reshape(1, -1)),
        "w1": w1.reshape(N_EXPERTS, d, f).astype(BF16),
        "w3": w3.reshape(N_EXPERTS, d, f).astype(BF16),
        "w2": w2.reshape(N_EXPERTS, f, d).astype(BF16),
    }


def kernel(x, meta_tokens, norm_mix_g, w_in, b_forget, q_norm_g, k_norm_g, w_up_attn, w_pool,
           pool_scale, w_up_pool, w_out, norm_ffn_g, w_group, b_group, w_router, b_router,
           w1, w3, w2):
    nb, seq, d = x.shape
    p = _prepare_params(norm_mix_g[0], w_in[0], b_forget[0], q_norm_g[0], k_norm_g[0],
                        w_up_attn[0], w_pool[0], pool_scale[0], w_up_pool[0], w_out[0],
                        norm_ffn_g[0], w_group[0], b_group[0], w_router[0], b_router[0],
                        w1[0], w3[0], w2[0])

    meta = jnp.pad(meta_tokens.astype(x.dtype), ((0, LANES - N_META), (0, 0)))[None]
    _, ka_m, vt_m, _, _, u_meta, f_meta, k_sq_meta, _ = _inproj(
        meta, p, jnp.zeros((MAX_WINDOW, POOL_WIDTH), F32), jnp.zeros((N_HEADS, LANES), F32),
        tm=LANES, n_valid=N_META, first_pos=0, chains=1)

    tm = INPROJ_TILE
    qta, ka, vt, pooled, gates, _, _, k_sq, self_logit = _inproj(
        x, p, u_meta, f_meta, tm=tm, n_valid=tm, first_pos=N_META, chains=INPROJ_CHAINS)
    k_sq_max = jnp.maximum(jnp.max(k_sq.reshape(nb, seq // tm, N_HEADS, LANES), axis=1),
                           k_sq_meta)
    ot = _attention(qta, ka, vt, ka_m, vt_m, k_sq_max, self_logit, tq=ATTN_TILE)
    h2, plan, cnt = _post(ot, pooled, gates, x, p, tm=POST_TILE)
    out = _moe(h2.reshape(nb * seq * SLOT_ROWS, LANES), plan, cnt[:, 0].astype(jnp.int32), p,
               tm=ROW_MOVE_TILE)
    return out.reshape(nb, seq, d)
```
